```python
import jax, jax.numpy as jnp
from jax import lax
import numpy as np

D_MODEL = 2048
BATCH = 1
SEQ = 16384
DEPTH = 1

MEM_LEN = 256
EPS = 1e-6
D_FF = 5504

RW_HEADS = 16
RW_HEAD_DIM = 64
RW_WIDTH = RW_HEADS * RW_HEAD_DIM
RW_DECAY_LORA = 64
RW_ICLR_LORA = 64
RW_GATE_LORA = 160
RW_GN_EPS = 64e-5
RW_SPLITS = (RW_WIDTH, RW_WIDTH, RW_WIDTH, RW_DECAY_LORA, RW_ICLR_LORA, RW_GATE_LORA)
RW_COLS = 3 * RW_WIDTH + RW_DECAY_LORA + RW_ICLR_LORA + RW_GATE_LORA

NSA_HEADS = 16
NSA_KV_GROUPS = 4
NSA_HPG = NSA_HEADS // NSA_KV_GROUPS
NSA_HEAD_DIM = 64
NSA_WIDTH = NSA_HEADS * NSA_HEAD_DIM
NSA_KV_WIDTH = NSA_KV_GROUPS * NSA_HEAD_DIM
NSA_SPLITS = (NSA_WIDTH, NSA_KV_WIDTH, NSA_KV_WIDTH, NSA_KV_WIDTH, NSA_KV_WIDTH, NSA_KV_WIDTH, NSA_KV_WIDTH, 3 * NSA_HEADS)
NSA_COLS = NSA_WIDTH + 6 * NSA_KV_WIDTH + 3 * NSA_HEADS
CMP_BLOCK = 32
CMP_STRIDE = 16
CMP_HIDDEN = 128
SEL_BLOCK = 64
SEL_TOPK = 16
WINDOW = 512
Q_BLOCK = 128
FORCE_BONUS = 1000.0

ROPE_THETA = 500000.0
ROPE_DIM = NSA_HEAD_DIM // 4

MIX_WIDTH = RW_WIDTH + NSA_WIDTH
IN_COLS = RW_COLS + NSA_COLS

MEM_HEADS = 4
MEM_HEAD_DIM = 128
MEM_WIDTH = MEM_HEADS * MEM_HEAD_DIM

kernel_name = 'hymba_rwkv7_nsa_macaron_memory'

F32 = jnp.float32


def rmsnorm(x, g):
    xf = x.astype(F32)
    y = xf * lax.rsqrt(jnp.mean(xf * xf, axis=-1, keepdims=True) + EPS)
    return (y * g.astype(F32)).astype(x.dtype)


def swiglu(x, w_gate, w_up, w_down):
    return (jax.nn.silu(x @ w_gate) * (x @ w_up)) @ w_down


def split_cols(x, sizes):
    offs = np.cumsum(np.array(sizes))[:-1].tolist()
    return jnp.split(x, offs, axis=-1)


def token_shift(f, mu):
    prev = jnp.pad(f, ((0, 0), (1, 0), (0, 0)))[:, :-1]
    return f + mu * (prev - f)


def partial_rope(x, pos):
    half = ROPE_DIM // 2
    inv_freq = ROPE_THETA ** (-jnp.arange(half, dtype=F32) * 2.0 / ROPE_DIM)
    ang = pos.astype(F32)[:, None] * inv_freq[None, :]
    cos = jnp.cos(ang)[None, :, None, :]
    sin = jnp.sin(ang)[None, :, None, :]
    x1 = x[..., :half].astype(F32)
    x2 = x[..., half:ROPE_DIM].astype(F32)
    rot = jnp.concatenate([x1 * cos - x2 * sin, x2 * cos + x1 * sin], axis=-1).astype(x.dtype)
    return jnp.concatenate([rot, x[..., ROPE_DIM:]], axis=-1)


def masked_softmax(s, mask):
    s = jnp.where(mask, s.astype(F32), -1e30)
    return jax.nn.softmax(s, axis=-1) * mask


def rwkv7_group(f, mu, w0, w_up, a0, a_up, g_up, k_k, k_a, r_k, ln_w, ln_b):
    B, S, _ = f.shape
    H, N = RW_HEADS, RW_HEAD_DIM
    f = token_shift(f, mu)
    r, k, v, wd, ad, gd = split_cols(f, RW_SPLITS)
    w_log = -jax.nn.softplus(-(w0 + jnp.tanh(wd) @ w_up)) - 0.5
    decay = jnp.exp(-jnp.exp(w_log.astype(F32)))
    a = jax.nn.sigmoid(a0 + ad @ a_up)
    g = jax.nn.sigmoid(gd) @ g_up
    heads = lambda t: t.reshape(B, S, H, N)
    kk = heads(k * k_k).astype(F32)
    kk = kk / jnp.maximum(jnp.sqrt(jnp.sum(kk * kk, axis=-1, keepdims=True)), 1e-12)
    k = k * (1.0 + (a - 1.0) * k_a)
    r_h, k_h, v_h, a_h = heads(r), heads(k), heads(v), heads(a)
    xs = tuple(jnp.swapaxes(t.astype(F32), 0, 1) for t in (r_h, heads(decay), k_h, v_h, kk, a_h))

    def step(state, inp):
        r_t, w_t, k_t, v_t, kk_t, a_t = inp
        sa = jnp.einsum('bhvk,bhk->bhv', state, -kk_t)
        state = (state * w_t[:, :, None, :]
                 + sa[..., None] * (kk_t * a_t)[:, :, None, :]
                 + v_t[..., None] * k_t[:, :, None, :])
        return state, jnp.einsum('bhvk,bhk->bhv', state, r_t)

    state0 = jnp.zeros((B, H, N, N), F32)
    _, y = lax.scan(step, state0, xs)
    y = jnp.swapaxes(y, 0, 1)
    mean = jnp.mean(y, axis=-1, keepdims=True)
    var = jnp.mean(jnp.square(y - mean), axis=-1, keepdims=True)
    y = ((y - mean) * lax.rsqrt(var + RW_GN_EPS)).reshape(B, S, RW_WIDTH) * ln_w + ln_b
    bonus = jnp.sum(r_h * k_h * r_k, axis=-1, keepdims=True) * v_h
    y = y + bonus.reshape(B, S, RW_WIDTH)
    return (y * g).astype(f.dtype)


def compress(x, pe, w1, w2):
    B, S, G, Dh = x.shape
    n_chunks = S // CMP_STRIDE
    ratio = CMP_BLOCK // CMP_STRIDE
    n_cmp = n_chunks - ratio + 1
    chunks = x.reshape(B, n_chunks, CMP_STRIDE, G, Dh)
    blocks = jnp.concatenate([chunks[:, i:i + n_cmp] for i in range(ratio)], axis=2)
    blocks = blocks + pe[None, None, :, None, :]
    blocks = blocks.transpose(0, 3, 1, 2, 4).reshape(B, G, n_cmp, CMP_BLOCK * Dh)
    return jax.nn.gelu(blocks @ w1) @ w2


def nsa_group(f, pos, cmp_pe_k, cmp_w1_k, cmp_w2_k, cmp_pe_v, cmp_w1_v, cmp_w2_v):
    B, S, _ = f.shape
    G, HPG, Dh = NSA_KV_GROUPS, NSA_HPG, NSA_HEAD_DIM
    q, kc, vc, ks, vs, kw, vw, gl = split_cols(f, NSA_SPLITS)
    kv = lambda t: t.reshape(B, S, G, Dh)
    q = q.reshape(B, S, NSA_HEADS, Dh)
    to_bghsd = lambda t: t.reshape(B, S, G, HPG, Dh).transpose(0, 2, 3, 1, 4)
    q_nope = to_bghsd(q)
    q_rope = to_bghsd(partial_rope(q, pos))

    k_cmp = compress(kv(kc), cmp_pe_k, cmp_w1_k, cmp_w2_k)
    v_cmp = compress(kv(vc), cmp_pe_v, cmp_w1_v, cmp_w2_v)
    n_cmp = k_cmp.shape[2]
    cmp_start = jnp.arange(n_cmp) * CMP_STRIDE
    cmp_end = cmp_start + (CMP_BLOCK - 1)
    n_sel = S // SEL_BLOCK
    sel_start = jnp.arange(n_sel) * SEL_BLOCK
    overlap = ((cmp_start[:, None] < sel_start[None, :] + SEL_BLOCK)
               & (cmp_end[:, None] >= sel_start[None, :])).astype(F32)

    to_blocks = lambda t: t.reshape(B, n_sel, SEL_BLOCK, G, Dh).transpose(0, 3, 1, 2, 4)
    k_sel_blocks = to_blocks(partial_rope(kv(ks), pos))
    v_sel_blocks = to_blocks(kv(vs))
    pad_front = lambda t: jnp.pad(t, ((0, 0), (WINDOW, 0), (0, 0), (0, 0))).transpose(0, 2, 1, 3)
    k_win = pad_front(partial_rope(kv(kw), pos))
    v_win = pad_front(kv(vw))
    gates = jax.nn.sigmoid(gl.astype(F32)).reshape(B, S, 3, G, HPG).transpose(0, 2, 3, 4, 1)

    top_n = min(SEL_TOPK, n_sel)
    scale = Dh ** -0.5
    gather = jax.vmap(jax.vmap(lambda blk, ix: blk[ix]))

    def query_block(i):
        q0 = i * Q_BLOCK
        t = q0 + jnp.arange(Q_BLOCK)
        qn = lax.dynamic_slice_in_dim(q_nope, q0, Q_BLOCK, axis=3)
        qr = lax.dynamic_slice_in_dim(q_rope, q0, Q_BLOCK, axis=3)
        s_c = jnp.einsum('bghqd,bgnd->bghqn', qn, k_cmp) * scale
        p_c = masked_softmax(s_c, cmp_end[None, :] <= t[:, None])
        o_c = jnp.einsum('bghqn,bgnd->bghqd', p_c, v_cmp)
        imp = jnp.einsum('bghqn,nj->bgqj', p_c, overlap)
        cur = (t // SEL_BLOCK)[:, None]
        jj = jnp.arange(n_sel)[None, :]
        forced = (jj == 0) | (jj == cur) | (jj == cur - 1)
        score = jnp.where(jj <= cur, imp + FORCE_BONUS * forced, -1.0)
        _, idx = lax.top_k(score, top_n)
        k_s = gather(k_sel_blocks, idx).reshape(B, G, Q_BLOCK, top_n * SEL_BLOCK, Dh)
        v_s = gather(v_sel_blocks, idx).reshape(B, G, Q_BLOCK, top_n * SEL_BLOCK, Dh)
        pos_s = (idx[..., None] * SEL_BLOCK + jnp.arange(SEL_BLOCK)).reshape(B, G, Q_BLOCK, top_n * SEL_BLOCK)
        s_s = jnp.einsum('bghqd,bgqkd->bghqk', qr, k_s) * scale
        p_s = masked_softmax(s_s, (pos_s <= t[:, None])[:, :, None])
        o_s = jnp.einsum('bghqk,bgqkd->bghqd', p_s, v_s)
        k_w = lax.dynamic_slice_in_dim(k_win, q0, Q_BLOCK + WINDOW, axis=2)
        v_w = lax.dynamic_slice_in_dim(v_win, q0, Q_BLOCK + WINDOW, axis=2)
        pos_w = q0 - WINDOW + jnp.arange(Q_BLOCK + WINDOW)
        diff = t[:, None] - pos_w[None, :]
        mask_w = (diff >= 0) & (diff < WINDOW) & (pos_w[None, :] >= 0)
        s_w = jnp.einsum('bghqd,bgkd->bghqk', qr, k_w) * scale
        p_w = masked_softmax(s_w, mask_w)
        o_w = jnp.einsum('bghqk,bgkd->bghqd', p_w, v_w)
        g = lax.dynamic_slice_in_dim(gates, q0, Q_BLOCK, axis=4)[..., None]
        o = g[:, 0] * o_c + g[:, 1] * o_s + g[:, 2] * o_w
        return o.transpose(0, 3, 1, 2, 4).reshape(B, Q_BLOCK, NSA_WIDTH)

    out = lax.map(query_block, jnp.arange(S // Q_BLOCK))
    return out.transpose(1, 0, 2, 3).reshape(B, S, NSA_WIDTH).astype(f.dtype)


def memory_cross_attention(h, mem, w_q, w_kv, w_o):
    B, S, _ = h.shape
    M = mem.shape[1]
    q = (h @ w_q).reshape(B, S, MEM_HEADS, MEM_HEAD_DIM)
    k, v = jnp.split(mem @ w_kv, 2, axis=-1)
    k = k.reshape(B, M, MEM_HEADS, MEM_HEAD_DIM)
    v = v.reshape(B, M, MEM_HEADS, MEM_HEAD_DIM)
    s = jnp.einsum('bshd,bmhd->bhsm', q, k).astype(F32) * (MEM_HEAD_DIM ** -0.5)
    p = jax.nn.softmax(s, axis=-1)
    o = jnp.einsum('bhsm,bmhd->bshd', p, v).astype(h.dtype).reshape(B, S, MEM_WIDTH)
    return o @ w_o


def setup_inputs(seed: int = 0) -> dict:
    key = jax.random.key(seed)
    ks = iter(jax.random.split(key, 48))
    nrm = lambda shape, scale: jax.random.normal(next(ks), shape, F32) * scale
    gain = lambda n: 1.0 + nrm((n,), 0.05)
    uni = lambda shape, lo, hi: jax.random.uniform(next(ks), shape, F32, lo, hi)
    d = D_MODEL
    return {
        'x': nrm((BATCH, SEQ, d), 1.0),
        'mem': nrm((BATCH, MEM_LEN, d), 1.0),
        'ffn1_pre_g': gain(d),
        'ffn1_w_gate': nrm((d, D_FF), d ** -0.5),
        'ffn1_w_up': nrm((d, D_FF), d ** -0.5),
        'ffn1_w_down': nrm((D_FF, d), D_FF ** -0.5),
        'ffn1_post_g': gain(d),
        'mix_pre_g': gain(d),
        'w_in': nrm((d, IN_COLS), d ** -0.5),
        'rw_mu': uni((RW_COLS,), 0.0, 1.0),
        'rw_w0': uni((RW_WIDTH,), -6.0, 1.0),
        'rw_w_up': nrm((RW_DECAY_LORA, RW_WIDTH), 0.5 * RW_DECAY_LORA ** -0.5),
        'rw_a0': nrm((RW_WIDTH,), 0.1),
        'rw_a_up': nrm((RW_ICLR_LORA, RW_WIDTH), 0.5 * RW_ICLR_LORA ** -0.5),
        'rw_g_up': nrm((RW_GATE_LORA, RW_WIDTH), RW_GATE_LORA ** -0.5),
        'rw_k_k': 0.85 + nrm((RW_WIDTH,), 0.05),
        'rw_k_a': 1.0 + nrm((RW_WIDTH,), 0.05),
        'rw_r_k': nrm((RW_HEADS, RW_HEAD_DIM), 0.1),
        'rw_ln_w': gain(RW_WIDTH),
        'rw_ln_b': nrm((RW_WIDTH,), 0.01),
        'cmp_pe_k': nrm((CMP_BLOCK, NSA_HEAD_DIM), 0.02),
        'cmp_w1_k': nrm((CMP_BLOCK * NSA_HEAD_DIM, CMP_HIDDEN), (CMP_BLOCK * NSA_HEAD_DIM) ** -0.5),
        'cmp_w2_k': nrm((CMP_HIDDEN, NSA_HEAD_DIM), CMP_HIDDEN ** -0.5),
        'cmp_pe_v': nrm((CMP_BLOCK, NSA_HEAD_DIM), 0.02),
        'cmp_w1_v': nrm((CMP_BLOCK * NSA_HEAD_DIM, CMP_HIDDEN), (CMP_BLOCK * NSA_HEAD_DIM) ** -0.5),
        'cmp_w2_v': nrm((CMP_HIDDEN, NSA_HEAD_DIM), CMP_HIDDEN ** -0.5),
        'w_out': nrm((MIX_WIDTH, d), MIX_WIDTH ** -0.5),
        'mix_post_g': gain(d),
        'mem_pre_g': gain(d),
        'mem_norm_g': gain(d),
        'mem_w_q': nrm((d, MEM_WIDTH), d ** -0.5),
        'mem_w_kv': nrm((d, 2 * MEM_WIDTH), d ** -0.5),
        'mem_w_o': nrm((MEM_WIDTH, d), MEM_WIDTH ** -0.5),
        'mem_post_g': gain(d),
        'ffn2_pre_g': gain(d),
        'ffn2_w_gate': nrm((d, D_FF), d ** -0.5),
        'ffn2_w_up': nrm((d, D_FF), d ** -0.5),
        'ffn2_w_down': nrm((D_FF, d), D_FF ** -0.5),
        'ffn2_post_g': gain(d),
    }


def reference(x, mem,
              ffn1_pre_g, ffn1_w_gate, ffn1_w_up, ffn1_w_down, ffn1_post_g,
              mix_pre_g, w_in,
              rw_mu, rw_w0, rw_w_up, rw_a0, rw_a_up, rw_g_up, rw_k_k, rw_k_a, rw_r_k, rw_ln_w, rw_ln_b,
              cmp_pe_k, cmp_w1_k, cmp_w2_k, cmp_pe_v, cmp_w1_v, cmp_w2_v,
              w_out, mix_post_g,
              mem_pre_g, mem_norm_g, mem_w_q, mem_w_kv, mem_w_o, mem_post_g,
              ffn2_pre_g, ffn2_w_gate, ffn2_w_up, ffn2_w_down, ffn2_post_g):
    S = x.shape[1]
    pos = jnp.arange(S, dtype=jnp.int32)
    mem_n = rmsnorm(mem, mem_norm_g)
    h = x
    for _ in range(DEPTH):
        h = h + 0.5 * rmsnorm(swiglu(rmsnorm(h, ffn1_pre_g), ffn1_w_gate, ffn1_w_up, ffn1_w_down), ffn1_post_g)
        feats = rmsnorm(h, mix_pre_g) @ w_in
        f_rw, f_nsa = jnp.split(feats, [RW_COLS], axis=-1)
        y_rw = rwkv7_group(f_rw, rw_mu, rw_w0, rw_w_up, rw_a0, rw_a_up, rw_g_up,
                           rw_k_k, rw_k_a, rw_r_k, rw_ln_w, rw_ln_b)
        y_nsa = nsa_group(f_nsa, pos, cmp_pe_k, cmp_w1_k, cmp_w2_k,
                          cmp_pe_v, cmp_w1_v, cmp_w2_v)
        y = jnp.concatenate([y_rw, y_nsa], axis=-1) @ w_out
        h = h + rmsnorm(y, mix_post_g)
        m = memory_cross_attention(rmsnorm(h, mem_pre_g), mem_n, mem_w_q, mem_w_kv, mem_w_o)
        h = h + rmsnorm(m, mem_post_g)
        h = h + 0.5 * rmsnorm(swiglu(rmsnorm(h, ffn2_pre_g), ffn2_w_gate, ffn2_w_up, ffn2_w_down), ffn2_post_g)
    return h
```

```python
import functools

import jax
import jax.numpy as jnp
from jax import lax
from jax.experimental import pallas as pl
from jax.experimental.pallas import tpu as pltpu

F32 = jnp.float32
BF16 = jnp.bfloat16

D_MODEL = 2048
D_FF = 5504
D_FF_PAD = 5632
EPS = 1e-6

RW_HEADS = 16
RW_N = 64
RW_WIDTH = 1024
RW_LORA = 288
RW_LORA_PAD = 384
RW_GN_EPS = 64e-5
RW_CHUNK = 64

NSA_HEADS = 16
NSA_G = 4
NSA_HPG = 4
NSA_DH = 64
NSA_KV = 256
CMP_STRIDE = 16
SEL_BLOCK = 64
SEL_TOPK = 16
WINDOW = 512
FORCE_BONUS = 1000.0
ROPE_THETA = 500000.0
ROPE_DIM = 16

MEM_HEADS = 4
MEM_DH = 128

NEG = -1e30
VMEM_LIMIT = 56 * 1024 * 1024


def _cparams(sem):
    return pltpu.CompilerParams(dimension_semantics=sem, vmem_limit_bytes=VMEM_LIMIT)


def _rms(x, g):
    return x * lax.rsqrt(jnp.mean(x * x, axis=-1, keepdims=True) + EPS) * g


def _dot(a, b):
    return jnp.dot(a, b, preferred_element_type=F32)


def _dot_nt(a, b):
    return lax.dot_general(a, b, (((1,), (1,)), ((), ())), preferred_element_type=F32)


def _split2(x):
    hi = x.astype(BF16)
    lo = (x - hi.astype(F32)).astype(BF16)
    return hi, lo


def _split3(x):
    hi = x.astype(BF16)
    r1 = x - hi.astype(F32)
    mid = r1.astype(BF16)
    lo = (r1 - mid.astype(F32)).astype(BF16)
    return hi, mid, lo


def _ffn_body(h_ref, gpre_ref, wg_ref, wu_ref, wd_ref, gpost_ref, o_ref, xn_ref, *, nj):
    j = pl.program_id(1)

    @pl.when(j == 0)
    def _():
        xn_ref[...] = _rms(h_ref[...], gpre_ref[...]).astype(BF16)
        o_ref[...] = jnp.zeros_like(o_ref)

    xn = xn_ref[...]
    g = _dot(xn, wg_ref[...])
    u = _dot(xn, wu_ref[...])
    a = (g * jax.nn.sigmoid(g) * u).astype(BF16)
    o_ref[...] += _dot(a, wd_ref[...])

    @pl.when(j == nj - 1)
    def _():
        o_ref[...] = h_ref[...] + 0.5 * _rms(o_ref[...], gpost_ref[...])


def _ffn(h, pre_g, wg, wu, wd, post_g, tm=512, tf=512):
    s, d = h.shape
    ffp = wg.shape[1]
    nj = ffp // tf
    return pl.pallas_call(
        functools.partial(_ffn_body, nj=nj),
        grid=(s // tm, nj),
        in_specs=[
            pl.BlockSpec((tm, d), lambda i, j: (i, 0)),
            pl.BlockSpec((1, d), lambda i, j: (0, 0)),
            pl.BlockSpec((d, tf), lambda i, j: (0, j)),
            pl.BlockSpec((d, tf), lambda i, j: (0, j)),
            pl.BlockSpec((tf, d), lambda i, j: (j, 0)),
            pl.BlockSpec((1, d), lambda i, j: (0, 0)),
        ],
        out_specs=pl.BlockSpec((tm, d), lambda i, j: (i, 0)),
        out_shape=jax.ShapeDtypeStruct((s, d), F32),
        scratch_shapes=[pltpu.VMEM((tm, d), BF16)],
        compiler_params=_cparams(("parallel", "arbitrary")),
        name="ffn",
    )(h, pre_g, wg, wu, wd, post_g)


def _inproj_body(h_ref, g_ref, w_ref, o_ref, xn_ref):
    @pl.when(pl.program_id(1) == 0)
    def _():
        xn_ref[...] = _rms(h_ref[...], g_ref[...]).astype(BF16)

    o_ref[...] = _dot(xn_ref[...], w_ref[...])


def _inproj(h, g, w, tm=512, tn=2048):
    s, d = h.shape
    n = w.shape[1]
    return pl.pallas_call(
        _inproj_body,
        grid=(s // tm, n // tn),
        in_specs=[
            pl.BlockSpec((tm, d), lambda i, j: (i, 0)),
            pl.BlockSpec((1, d), lambda i, j: (0, 0)),
            pl.BlockSpec((d, tn), lambda i, j: (0, j)),
        ],
        out_specs=pl.BlockSpec((tm, tn), lambda i, j: (i, j)),
        out_shape=jax.ShapeDtypeStruct((s, n), F32),
        scratch_shapes=[pltpu.VMEM((tm, d), BF16)],
        compiler_params=_cparams(("parallel", "arbitrary")),
        name="inproj",
    )(h, g, w)


def _rwprep_body(f_ref, fp_ref, lo_ref, lop_ref, mua_ref, mul_ref, w0_ref, a0_ref, kk_ref, ka_ref,
                 rk_ref, wup_ref, aup_ref, gup_ref, e_ref, tri_ref,
                 at_ref, rt_ref, bt_ref, kt_ref, v_ref, pt_ref, g_ref, bon_ref, *, tm):
    first = pl.program_id(0) == 0

    def shifted(x, prev_blk):
        prev_last = jnp.where(first, 0.0, prev_blk[7:8, :])
        rolled = pltpu.roll(x, 1, 0)
        row = lax.broadcasted_iota(jnp.int32, x.shape, 0)
        return jnp.where(row == 0, prev_last, rolled)

    f = f_ref[...]
    fs = f + mua_ref[...] * (shifted(f, fp_ref[...]) - f)
    lo = lo_ref[...]
    los = lo + mul_ref[...] * (shifted(lo, lop_ref[...]) - lo)
    r = fs[:, :RW_WIDTH]
    k = fs[:, RW_WIDTH:2 * RW_WIDTH]
    v = fs[:, 2 * RW_WIDTH:]

    wx = _dot(jnp.tanh(los).astype(BF16), wup_ref[...])
    ax = _dot(los.astype(BF16), aup_ref[...])
    g_ref[...] = _dot(jax.nn.sigmoid(los).astype(BF16), gup_ref[...])

    z = -(w0_ref[...] + wx)
    softplus = jnp.maximum(z, 0.0) + jnp.log(1.0 + jnp.exp(-jnp.abs(z)))
    ld = -jnp.exp(-softplus - 0.5)
    a = jax.nn.sigmoid(a0_ref[...] + ax)

    e = e_ref[...]

    def headsum(x):
        parts = []
        for c in range(RW_WIDTH // 256):
            hi, lo_ = _split2(x[:, c * 256:(c + 1) * 256])
            parts.append(_dot(hi, e) + _dot(lo_, e))
        return jnp.concatenate(parts, axis=1)

    kk0 = k * kk_ref[...]
    kk = kk0 / jnp.maximum(jnp.sqrt(headsum(kk0 * kk0)), 1e-12)
    kmod = k * (1.0 + (a - 1.0) * ka_ref[...])
    bon_ref[...] = headsum(r * kmod * rk_ref[...]) * v

    tri = tri_ref[...]
    cums = []
    for c in range(tm // RW_CHUNK):
        hi, mid, lo_ = _split3(ld[c * RW_CHUNK:(c + 1) * RW_CHUNK])
        cums.append(_dot(tri, hi) + _dot(tri, mid) + _dot(tri, lo_))
    cum = jnp.concatenate(cums, axis=0)
    e_in = jnp.exp(cum)
    e_out = jnp.exp(-cum)
    at = -kk * jnp.exp(cum - ld)
    rt = r * e_in
    bt = kk * a * e_out
    kt = kmod * e_out
    for h in range(RW_HEADS):
        sl = slice(h * RW_N, (h + 1) * RW_N)
        at_ref[h] = at[:, sl]
        rt_ref[h] = rt[:, sl]
        bt_ref[h] = bt[:, sl]
        kt_ref[h] = kt[:, sl]
        v_ref[h] = v[:, sl]
    for c in range(tm // RW_CHUNK):
        last = e_in[c * RW_CHUNK + RW_CHUNK - 1:(c + 1) * RW_CHUNK, :]
        for h in range(RW_HEADS):
            pt_ref[h, c * 8:(c + 1) * 8, :] = jnp.broadcast_to(last[:, h * RW_N:(h + 1) * RW_N], (8, RW_N))


def _rwprep(feats, mu_a, mu_l, w0, a0, k_k, k_a, r_k, wup, aup, gup, tm=256):
    s = feats.shape[0]
    nb8 = tm // 8
    lora_blk = (2 * 3072 - RW_LORA_PAD) // RW_LORA_PAD
    e = (lax.broadcasted_iota(jnp.int32, (256, 256), 0) // RW_N
         == lax.broadcasted_iota(jnp.int32, (256, 256), 1) // RW_N).astype(BF16)
    tri = (lax.broadcasted_iota(jnp.int32, (RW_CHUNK, RW_CHUNK), 0)
           >= lax.broadcasted_iota(jnp.int32, (RW_CHUNK, RW_CHUNK), 1)).astype(BF16)
    row = lambda n: pl.BlockSpec((1, n), lambda i: (0, 0))
    full = lambda a, b: pl.BlockSpec((a, b), lambda i: (0, 0))
    head_shape = jax.ShapeDtypeStruct((RW_HEADS, s, RW_N), F32)
    head_spec = pl.BlockSpec((RW_HEADS, tm, RW_N), lambda i: (0, i, 0))
    return pl.pallas_call(
        functools.partial(_rwprep_body, tm=tm),
        grid=(s // tm,),
        in_specs=[
            pl.BlockSpec((tm, 3072), lambda i: (i, 0)),
            pl.BlockSpec((8, 3072), lambda i: (jnp.maximum(i * nb8 - 1, 0), 0)),
            pl.BlockSpec((tm, RW_LORA_PAD), lambda i: (i, lora_blk)),
            pl.BlockSpec((8, RW_LORA_PAD), lambda i: (jnp.maximum(i * nb8 - 1, 0), lora_blk)),
            row(3072), row(RW_LORA_PAD), row(RW_WIDTH), row(RW_WIDTH), row(RW_WIDTH), row(RW_WIDTH),
            row(RW_WIDTH),
            full(RW_LORA_PAD, RW_WIDTH), full(RW_LORA_PAD, RW_WIDTH), full(RW_LORA_PAD, RW_WIDTH),
            full(256, 256), full(RW_CHUNK, RW_CHUNK),
        ],
        out_specs=[head_spec] * 5 + [
            pl.BlockSpec((RW_HEADS, nb8, RW_N), lambda i: (0, i, 0)),
            pl.BlockSpec((tm, RW_WIDTH), lambda i: (i, 0)),
            pl.BlockSpec((tm, RW_WIDTH), lambda i: (i, 0)),
        ],
        out_shape=[head_shape] * 5 + [
            jax.ShapeDtypeStruct((RW_HEADS, s // 8, RW_N), F32),
            jax.ShapeDtypeStruct((s, RW_WIDTH), F32),
            jax.ShapeDtypeStruct((s, RW_WIDTH), F32),
        ],
        compiler_params=_cparams(("parallel",)),
        name="rwprep",
    )(feats, feats, feats, feats, mu_a, mu_l, w0, a0, k_k, k_a, r_k, wup, aup, gup, e, tri)


def _bmm(a, b):
    return jnp.einsum("hij,hjk->hik", a, b, preferred_element_type=F32)


def _bmm_nt(a, b):
    return jnp.einsum("hik,hjk->hij", a, b, preferred_element_type=F32)


def _rwscan_body(at_ref, rt_ref, bt_ref, kt_ref, v_ref, pt_ref, y_ref, s_ref, *, nc):
    @pl.when(pl.program_id(0) == 0)
    def _():
        s_ref[...] = jnp.zeros_like(s_ref)

    t = RW_CHUNK
    row = lax.broadcasted_iota(jnp.int32, (t, t), 0)
    col = lax.broadcasted_iota(jnp.int32, (t, t), 1)
    strict = (row > col)[None]
    incl = (row >= col)[None]
    eye = (row == col).astype(F32)[None]

    def chunk(c, carry):
        r0 = pl.multiple_of(c * t, t)
        a_f = at_ref[:, pl.ds(r0, t), :]
        r_f = rt_ref[:, pl.ds(r0, t), :]
        a_b = a_f.astype(BF16)
        b_b = bt_ref[:, pl.ds(r0, t), :].astype(BF16)
        k_b = kt_ref[:, pl.ds(r0, t), :].astype(BF16)
        v_f = v_ref[:, pl.ds(r0, t), :]
        v_b = v_f.astype(BF16)
        xa = jnp.concatenate([a_b, r_f.astype(BF16)], axis=1)
        gb = _bmm_nt(xa, b_b)
        gk = _bmm_nt(xa, k_b)
        a_ab = jnp.where(strict, gb[:, :t], 0.0)
        a_rb = jnp.where(incl, gb[:, t:], 0.0).astype(BF16)
        a_ak = jnp.where(strict, gk[:, :t], 0.0).astype(BF16)
        a_rk = jnp.where(incl, gk[:, t:], 0.0).astype(BF16)
        inv = eye + a_ab
        pk = a_ab
        for _ in range(5):
            pkb = pk.astype(BF16)
            pk = _bmm(pkb, pkb)
            inv = inv + _bmm(inv.astype(BF16), pk.astype(BF16))
        inv_b = inv.astype(BF16)
        w1 = _bmm(a_ak, v_b)
        atp = _bmm(inv_b, a_b)
        z0 = _bmm(inv_b, w1.astype(BF16))
        rh = r_f + _bmm(a_rb, atp.astype(BF16))
        y0 = _bmm(a_rb, z0.astype(BF16)) + _bmm(a_rk, v_b)
        atp_t = jnp.swapaxes(atp, 1, 2).astype(BF16)
        z0_t = jnp.swapaxes(z0, 1, 2).astype(BF16)
        v_t = jnp.swapaxes(v_f, 1, 2).astype(BF16)
        pt = pt_ref[:, pl.ds(pl.multiple_of(c * 8, 8), 8), :][:, 0:1, :]
        m = (eye + _bmm(atp_t, b_b)) * pt
        cc = (_bmm(z0_t, b_b) + _bmm(v_t, k_b)) * pt
        st = s_ref[...]
        st_b = st.astype(BF16)
        y = _bmm_nt(rh.astype(BF16), st_b) + y0
        s_ref[...] = _bmm(st_b, m.astype(BF16)) + cc
        mean = jnp.mean(y, axis=-1, keepdims=True)
        yc = y - mean
        var = jnp.mean(yc * yc, axis=-1, keepdims=True)
        yn = yc * lax.rsqrt(var + RW_GN_EPS)
        y_ref[pl.ds(r0, t), :] = jnp.concatenate([yn[h] for h in range(RW_HEADS)], axis=-1)
        return carry

    lax.fori_loop(0, nc, chunk, 0)


def _rwscan(at, rt, bt, kt, v, pt, rows=256):
    s = at.shape[1]
    nc = rows // RW_CHUNK
    head_spec = pl.BlockSpec((RW_HEADS, rows, RW_N), lambda i: (0, i, 0))
    return pl.pallas_call(
        functools.partial(_rwscan_body, nc=nc),
        grid=(s // rows,),
        in_specs=[head_spec] * 5 + [pl.BlockSpec((RW_HEADS, rows // 8, RW_N), lambda i: (0, i, 0))],
        out_specs=pl.BlockSpec((rows, RW_WIDTH), lambda i: (i, 0)),
        out_shape=jax.ShapeDtypeStruct((s, RW_WIDTH), F32),
        scratch_shapes=[pltpu.VMEM((RW_HEADS, RW_N, RW_N), F32)],
        compiler_params=_cparams(("arbitrary",)),
        name="rwscan",
    )(at, rt, bt, kt, v, pt)


def _nsaprep_body(f_ref, cos_ref, sin_ref, qn_ref, qr_ref, kc_ref, vc_ref, ks_ref, vst_ref, kw_ref,
                  vwt_ref, gt_ref, *, tq):
    cs = cos_ref[...]
    sn = sin_ref[...]
    lane = lax.broadcasted_iota(jnp.int32, (tq, 128), 1)
    first8 = (lane % NSA_DH) < (ROPE_DIM // 2)

    def rope(x):
        swapped = jnp.where(first8, pltpu.roll(x, 128 - ROPE_DIM // 2, 1), pltpu.roll(x, ROPE_DIM // 2, 1))
        return x * cs + swapped * sn

    scale = NSA_DH ** -0.5
    for p in range(NSA_HEADS // 2):
        x = f_ref[:, p * 128:(p + 1) * 128]
        xn_t = (x * scale).T
        xr_t = (rope(x) * scale).T
        for e in range(2):
            hd = 2 * p + e
            g, h = hd // NSA_HPG, hd % NSA_HPG
            qn_ref[g, 0, :, h * tq:(h + 1) * tq] = xn_t[e * NSA_DH:(e + 1) * NSA_DH].astype(BF16)
            qr_ref[g, 0, :, h * tq:(h + 1) * tq] = xr_t[e * NSA_DH:(e + 1) * NSA_DH].astype(BF16)

    def kv_piece(idx, p):
        base = NSA_HEADS * NSA_DH + idx * NSA_KV + p * 128
        return f_ref[:, base:base + 128]

    for p in range(2):
        kc = kv_piece(0, p)
        vc = kv_piece(1, p)
        ks = rope(kv_piece(2, p))
        vs_t = kv_piece(3, p).T
        kw = rope(kv_piece(4, p))
        vw_t = kv_piece(5, p).T
        for e in range(2):
            g = 2 * p + e
            sl = slice(e * NSA_DH, (e + 1) * NSA_DH)
            kc_ref[g] = kc[:, sl]
            vc_ref[g] = vc[:, sl]
            ks_ref[g] = ks[:, sl].astype(BF16)
            kw_ref[g] = kw[:, sl].astype(BF16)
            vst_ref[g] = vs_t[sl].astype(BF16)
            vwt_ref[g] = vw_t[sl].astype(BF16)
    gl = f_ref[:, 2560:2688]
    gt_ref[...] = jax.nn.sigmoid(gl).T[:3 * NSA_HEADS]


def _nsaprep(feats, cos_t, sin_t, tq):
    s = feats.shape[0]
    ni = s // tq
    kv_f32 = jax.ShapeDtypeStruct((NSA_G, s, NSA_DH), F32)
    kv_b16 = jax.ShapeDtypeStruct((NSA_G, s, NSA_DH), BF16)
    kvt_b16 = jax.ShapeDtypeStruct((NSA_G, NSA_DH, s), BF16)
    q_shape = jax.ShapeDtypeStruct((NSA_G, ni, NSA_DH, NSA_HPG * tq), BF16)
    q_spec = pl.BlockSpec((NSA_G, 1, NSA_DH, NSA_HPG * tq), lambda i: (0, i, 0, 0))
    kv_spec = pl.BlockSpec((NSA_G, tq, NSA_DH), lambda i: (0, i, 0))
    kvt_spec = pl.BlockSpec((NSA_G, NSA_DH, tq), lambda i: (0, 0, i))
    return pl.pallas_call(
        functools.partial(_nsaprep_body, tq=tq),
        grid=(ni,),
        in_specs=[
            pl.BlockSpec((tq, 3072), lambda i: (i, 1)),
            pl.BlockSpec((tq, 128), lambda i: (i, 0)),
            pl.BlockSpec((tq, 128), lambda i: (i, 0)),
        ],
        out_specs=[q_spec, q_spec, kv_spec, kv_spec, kv_spec, kvt_spec, kv_spec, kvt_spec,
                   pl.BlockSpec((3 * NSA_HEADS, tq), lambda i: (0, i))],
        out_shape=[q_shape, q_shape, kv_f32, kv_f32, kv_b16, kvt_b16, kv_b16, kvt_b16,
                   jax.ShapeDtypeStruct((3 * NSA_HEADS, s), F32)],
        compiler_params=_cparams(("parallel",)),
        name="nsaprep",
    )(feats, cos_t, sin_t)


def _gelu_tanh(x):
    return 0.5 * x * (1.0 + jnp.tanh(0.7978845608028654 * (x + 0.044715 * x * x * x)))


def _compress_body(x_ref, pe1_ref, pe2_ref, w1a_ref, w1b_ref, w2_ref, o_ref, *, ncp, transpose_out):
    x = x_ref[0]
    a = _dot((x + pe1_ref[...]).astype(BF16), w1a_ref[...])
    b = _dot((x + pe2_ref[...]).astype(BF16), w1b_ref[...])
    hid = a + pltpu.roll(b, ncp - 1, 0)
    act = _gelu_tanh(hid).astype(BF16)
    if transpose_out:
        o_ref[0] = _dot_nt(w2_ref[...], act).astype(BF16)
    else:
        o_ref[0] = _dot(act, w2_ref[...]).astype(BF16)


def _compress(x, pe, w1, w2, transpose_out):
    g, s, dh = x.shape
    ncp = s // CMP_STRIDE
    half = CMP_STRIDE * dh
    xr = x.reshape(g, ncp, half)
    pe1 = pe[:CMP_STRIDE].reshape(1, half)
    pe2 = pe[CMP_STRIDE:].reshape(1, half)
    w1a = w1[:half].astype(BF16)
    w1b = w1[half:].astype(BF16)
    hid = w1.shape[1]
    w2k = (w2.T if transpose_out else w2).astype(BF16)
    full = lambda a, b: pl.BlockSpec((a, b), lambda i: (0, 0))
    if transpose_out:
        out_shape = jax.ShapeDtypeStruct((g, dh, ncp), BF16)
        out_spec = pl.BlockSpec((1, dh, ncp), lambda i: (i, 0, 0))
    else:
        out_shape = jax.ShapeDtypeStruct((g, ncp, dh), BF16)
        out_spec = pl.BlockSpec((1, ncp, dh), lambda i: (i, 0, 0))
    return pl.pallas_call(
        functools.partial(_compress_body, ncp=ncp, transpose_out=transpose_out),
        grid=(g,),
        in_specs=[
            pl.BlockSpec((1, ncp, half), lambda i: (i, 0, 0)),
            full(1, half), full(1, half), full(half, hid), full(half, hid), full(*w2k.shape),
        ],
        out_specs=out_spec,
        out_shape=out_shape,
        compiler_params=_cparams(("parallel",)),
        name="compress",
    )(xr, pe1, pe2, w1a, w1b, w2k)


def _online_update(s, mask, vt_blk, m_ref, l_ref, acc_ref):
    sm = jnp.where(mask, s, NEG)
    m_old = m_ref[...]
    m_new = jnp.maximum(m_old, jnp.max(sm, axis=0, keepdims=True))
    alpha = jnp.exp(m_old - m_new)
    p = jnp.where(mask, jnp.exp(sm - m_new), 0.0)
    l_ref[...] = alpha * l_ref[...] + jnp.sum(p, axis=0, keepdims=True)
    acc_ref[...] = alpha * acc_ref[...] + _dot(vt_blk, p.astype(BF16))
    m_ref[...] = m_new


def _nsa_body(qn_ref, qr_ref, kc_ref, vct_ref, ov_ref, ks_ref, vst_ref, kw_ref, vwt_ref, gt_ref, o_ref,
              sel_ref, m_ref, l_ref, acc_ref, *, tq, ncp, nsel, topn):
    g = pl.program_id(0)
    i = pl.program_id(1)
    q0 = i * tq
    w4 = NSA_HPG * tq
    bpc = tq // SEL_BLOCK
    lane4 = lax.broadcasted_iota(jnp.int32, (1, w4), 1)
    t4 = q0 + lane4 % tq
    qn = qn_ref[0, 0]
    qr = qr_ref[0, 0]

    sc = _dot(kc_ref[0], qn)
    n_io = lax.broadcasted_iota(jnp.int32, (ncp, 1), 0)
    cmask = (n_io * CMP_STRIDE + (2 * CMP_STRIDE - 1)) <= t4
    scm = jnp.where(cmask, sc, NEG)
    mc = jnp.max(scm, axis=0, keepdims=True)
    pc = jnp.where(cmask, jnp.exp(scm - mc), 0.0)
    lc = jnp.sum(pc, axis=0, keepdims=True)
    pc = pc * jnp.where(lc > 0.0, 1.0 / lc, 0.0)
    o_c = _dot(vct_ref[0], pc.astype(BF16))

    ps = pc[:, :tq]
    for h in range(1, NSA_HPG):
        ps = ps + pc[:, h * tq:(h + 1) * tq]
    hi, lo = _split2(ps)
    imp = _dot(ov_ref[...], hi) + _dot(ov_ref[...], lo)
    jj = lax.broadcasted_iota(jnp.int32, (nsel, tq), 0)
    jf = jj.astype(F32)
    cur = (q0 + lax.broadcasted_iota(jnp.int32, (1, tq), 1)) // SEL_BLOCK
    forced = (jj == 0) | (jj == cur) | (jj == cur - 1)
    score = jnp.where(jj <= cur, imp + jnp.where(forced, FORCE_BONUS, 0.0), -1.0)
    sel = jnp.zeros((nsel, tq), F32)
    for _ in range(topn):
        mx = jnp.max(score, axis=0, keepdims=True)
        first = jnp.min(jnp.where(score == mx, jf, float(nsel)), axis=0, keepdims=True)
        hit = jf == first
        sel = jnp.where(hit, 1.0, sel)
        score = jnp.where(hit, -3e38, score)
    sel_ref[...] = sel

    def reset():
        m_ref[...] = jnp.full_like(m_ref, NEG)
        l_ref[...] = jnp.zeros_like(l_ref)
        acc_ref[...] = jnp.zeros_like(acc_ref)

    def finish():
        l = l_ref[...]
        return acc_ref[...] * jnp.where(l > 0.0, 1.0 / l, 0.0)

    row_io = lax.broadcasted_iota(jnp.int32, (tq, 1), 0)

    def sel_chunk(c, causal):
        c0 = pl.multiple_of(c * tq, tq)
        s = _dot(ks_ref[0, pl.ds(c0, tq), :], qr)
        rows = [jnp.broadcast_to(sel_ref[pl.ds(c * bpc + j, 1), :], (SEL_BLOCK, tq)) for j in range(bpc)]
        mk = jnp.concatenate(rows, axis=0)
        mask = jnp.concatenate([mk] * NSA_HPG, axis=1) > 0.0
        if causal:
            mask = mask & ((c0 + row_io) <= t4)
        _online_update(s, mask, vst_ref[0, :, pl.ds(c0, tq)], m_ref, l_ref, acc_ref)

    reset()

    def sel_loop(c, carry):
        sel_chunk(c, False)
        return carry

    lax.fori_loop(0, i, sel_loop, 0)
    sel_chunk(i, True)
    o_s = finish()

    reset()
    nwin = WINDOW // tq + 1
    start = jnp.maximum(i - (nwin - 1), 0)
    for j in range(nwin):
        c0 = pl.multiple_of((start + j) * tq, tq)
        s = _dot(kw_ref[0, pl.ds(c0, tq), :], qr)
        diff = t4 - (c0 + row_io)
        mask = (diff >= 0) & (diff < WINDOW)
        _online_update(s, mask, vwt_ref[0, :, pl.ds(c0, tq)], m_ref, l_ref, acc_ref)
    o_w = finish()

    def gate(branch):
        rows = [gt_ref[pl.ds(branch * NSA_HEADS + g * NSA_HPG + h, 1), :] for h in range(NSA_HPG)]
        return jnp.concatenate(rows, axis=1)

    out_t = gate(0) * o_c + gate(1) * o_s + gate(2) * o_w
    halves = []
    for p in range(NSA_HPG // 2):
        pair = jnp.concatenate([out_t[:, (2 * p) * tq:(2 * p + 1) * tq],
                                out_t[:, (2 * p + 1) * tq:(2 * p + 2) * tq]], axis=0)
        halves.append(pair.T)
    o_ref[...] = jnp.concatenate(halves, axis=1)


def _nsa(qn, qr, kc, vct, ov, ks, vst, kw, vwt, gt, tq):
    g, ni, dh, w4 = qn.shape
    s = ks.shape[1]
    ncp = kc.shape[1]
    nsel = s // SEL_BLOCK
    topn = min(SEL_TOPK, nsel)
    q_spec = pl.BlockSpec((1, 1, dh, w4), lambda a, b: (a, b, 0, 0))
    return pl.pallas_call(
        functools.partial(_nsa_body, tq=tq, ncp=ncp, nsel=nsel, topn=topn),
        grid=(g, ni),
        in_specs=[
            q_spec, q_spec,
            pl.BlockSpec((1, ncp, dh), lambda a, b: (a, 0, 0)),
            pl.BlockSpec((1, dh, ncp), lambda a, b: (a, 0, 0)),
            pl.BlockSpec((nsel, ncp), lambda a, b: (0, 0)),
            pl.BlockSpec((1, s, dh), lambda a, b: (a, 0, 0)),
            pl.BlockSpec((1, dh, s), lambda a, b: (a, 0, 0)),
            pl.BlockSpec((1, s, dh), lambda a, b: (a, 0, 0)),
            pl.BlockSpec((1, dh, s), lambda a, b: (a, 0, 0)),
            pl.BlockSpec((3 * NSA_HEADS, tq), lambda a, b: (0, b)),
        ],
        out_specs=pl.BlockSpec((tq, NSA_HPG * dh), lambda a, b: (b, a)),
        out_shape=jax.ShapeDtypeStruct((s, NSA_HEADS * dh), F32),
        scratch_shapes=[
            pltpu.VMEM((nsel, tq), F32),
            pltpu.VMEM((1, w4), F32),
            pltpu.VMEM((1, w4), F32),
            pltpu.VMEM((dh, w4), F32),
        ],
        compiler_params=_cparams(("arbitrary", "arbitrary")),
        name="nsa",
    )(qn, qr, kc, vct, ov, ks, vst, kw, vwt, gt)


def _memkv_body(mem_ref, g_ref, w_ref, k_ref, v_ref):
    kv = _dot(_rms(mem_ref[...], g_ref[...]).astype(BF16), w_ref[...])
    width = MEM_HEADS * MEM_DH
    k_ref[...] = kv[:, :width].astype(BF16)
    v_ref[...] = kv[:, width:].astype(BF16)


def _memkv(mem, g, w):
    m, d = mem.shape
    width = MEM_HEADS * MEM_DH
    shp = jax.ShapeDtypeStruct((m, width), BF16)
    return pl.pallas_call(
        _memkv_body,
        out_shape=[shp, shp],
        compiler_params=pltpu.CompilerParams(vmem_limit_bytes=VMEM_LIMIT),
        name="memkv",
    )(mem, g, w)


def _mixout_body(yn_ref, bon_ref, gate_ref, ynsa_ref, h_ref, lnw_ref, lnb_ref, wo_rw_ref, wo_nsa_ref,
                 gpost_ref, mpre_ref, wq_ref, k_ref, v_ref, wom_ref, mpost_ref, o_ref):
    y_rw = ((yn_ref[...] * lnw_ref[...] + lnb_ref[...]) + bon_ref[...]) * gate_ref[...]
    y = _dot(y_rw.astype(BF16), wo_rw_ref[...]) + _dot(ynsa_ref[...].astype(BF16), wo_nsa_ref[...])
    h2 = h_ref[...] + _rms(y, gpost_ref[...])
    q = _dot(_rms(h2, mpre_ref[...]).astype(BF16), wq_ref[...])
    scale = MEM_DH ** -0.5
    outs = []
    for hh in range(MEM_HEADS):
        sl = slice(hh * MEM_DH, (hh + 1) * MEM_DH)
        s = _dot_nt(q[:, sl].astype(BF16), k_ref[:, sl]) * scale
        p = jnp.exp(s - jnp.max(s, axis=-1, keepdims=True))
        p = p / jnp.sum(p, axis=-1, keepdims=True)
        outs.append(_dot(p.astype(BF16), v_ref[:, sl]))
    o = jnp.concatenate(outs, axis=-1).astype(BF16)
    m = _dot(o, wom_ref[...])
    o_ref[...] = h2 + _rms(m, mpost_ref[...])


def _mixout(yn, bon, gate, ynsa, h, lnw, lnb, wo_rw, wo_nsa, gpost, mpre, wq, k, v, wom, mpost, tm=256):
    s, d = h.shape
    rows = lambda n: pl.BlockSpec((tm, n), lambda i: (i, 0))
    full = lambda a: pl.BlockSpec(a.shape, lambda i: (0, 0))
    return pl.pallas_call(
        _mixout_body,
        grid=(s // tm,),
        in_specs=[rows(RW_WIDTH), rows(RW_WIDTH), rows(RW_WIDTH), rows(RW_WIDTH), rows(d),
                  full(lnw), full(lnb), full(wo_rw), full(wo_nsa), full(gpost), full(mpre), full(wq),
                  full(k), full(v), full(wom), full(mpost)],
        out_specs=rows(d),
        out_shape=jax.ShapeDtypeStruct((s, d), F32),
        compiler_params=_cparams(("parallel",)),
        name="mixout",
    )(yn, bon, gate, ynsa, h, lnw, lnb, wo_rw, wo_nsa, gpost, mpre, wq, k, v, wom, mpost)


def _rope_tables(s):
    half = ROPE_DIM // 2
    inv_freq = ROPE_THETA ** (-jnp.arange(half, dtype=F32) * 2.0 / ROPE_DIM)
    ang = jnp.arange(s, dtype=jnp.int32).astype(F32)[:, None] * inv_freq[None, :]
    cos, sin = jnp.cos(ang), jnp.sin(ang)
    ones = jnp.ones((s, NSA_DH - ROPE_DIM), F32)
    zeros = jnp.zeros((s, NSA_DH - ROPE_DIM), F32)
    cos_h = jnp.concatenate([cos, cos, ones], axis=1)
    sin_h = jnp.concatenate([-sin, sin, zeros], axis=1)
    return jnp.tile(cos_h, (1, 2)), jnp.tile(sin_h, (1, 2))


def _overlap_t(s):
    ncp = s // CMP_STRIDE
    nsel = s // SEL_BLOCK
    cmp_start = jnp.arange(ncp)[None, :] * CMP_STRIDE
    sel_start = jnp.arange(nsel)[:, None] * SEL_BLOCK
    ov = (cmp_start < sel_start + SEL_BLOCK) & (cmp_start + 2 * CMP_STRIDE - 1 >= sel_start)
    ov = ov & (jnp.arange(ncp)[None, :] < ncp - 1)
    return ov.astype(BF16)


def kernel(x, mem, ffn1_pre_g, ffn1_w_gate, ffn1_w_up, ffn1_w_down, ffn1_post_g, mix_pre_g, w_in, rw_mu, rw_w0, rw_w_up, rw_a0, rw_a_up, rw_g_up, rw_k_k, rw_k_a, rw_r_k, rw_ln_w, rw_ln_b, cmp_pe_k, cmp_w1_k, cmp_w2_k, cmp_pe_v, cmp_w1_v, cmp_w2_v, w_out, mix_post_g, mem_pre_g, mem_norm_g, mem_w_q, mem_w_kv, mem_w_o, mem_post_g, ffn2_pre_g, ffn2_w_gate, ffn2_w_up, ffn2_w_down, ffn2_post_g):
    b, s, d = x.shape
    assert b == 1 and d == D_MODEL and s % 1024 == 0
    tq = 256
    row = lambda v: v.reshape(1, -1).astype(F32)
    pad_ff = D_FF_PAD - D_FF

    def ffn_weights(wg, wu, wd):
        return (jnp.pad(wg, ((0, 0), (0, pad_ff))).astype(BF16),
                jnp.pad(wu, ((0, 0), (0, pad_ff))).astype(BF16),
                jnp.pad(wd, ((0, pad_ff), (0, 0))).astype(BF16))

    h = x[0]
    h = _ffn(h, row(ffn1_pre_g), *ffn_weights(ffn1_w_gate, ffn1_w_up, ffn1_w_down), row(ffn1_post_g))

    rw_cols = 3 * RW_WIDTH + RW_LORA
    nsa_main = NSA_HEADS * NSA_DH + 6 * NSA_KV
    zc = lambda n: jnp.zeros((d, n), F32)
    w_in_r = jnp.concatenate([
        w_in[:, :3 * RW_WIDTH],
        w_in[:, rw_cols:rw_cols + nsa_main],
        w_in[:, rw_cols + nsa_main:], zc(128 - 3 * NSA_HEADS),
        w_in[:, 3 * RW_WIDTH:rw_cols], zc(RW_LORA_PAD - RW_LORA),
    ], axis=1).astype(BF16)
    feats = _inproj(h, row(mix_pre_g), w_in_r)

    mu_a = row(rw_mu[:3 * RW_WIDTH])
    mu_l = row(jnp.pad(rw_mu[3 * RW_WIDTH:], (0, RW_LORA_PAD - RW_LORA)))
    lora_w = lambda w, off: jnp.pad(w, ((off, RW_LORA_PAD - off - w.shape[0]), (0, 0))).astype(BF16)
    at, rt, bt, kt, v, pt, gate, bonus = _rwprep(
        feats, mu_a, mu_l, row(rw_w0), row(rw_a0), row(rw_k_k), row(rw_k_a), row(rw_r_k),
        lora_w(rw_w_up, 0), lora_w(rw_a_up, 64), lora_w(rw_g_up, 128))
    yn = _rwscan(at, rt, bt, kt, v, pt)

    cos_t, sin_t = _rope_tables(s)
    qn, qr, kc, vc, ks, vst, kw, vwt, gt = _nsaprep(feats, cos_t, sin_t, tq)
    k_cmp = _compress(kc, cmp_pe_k, cmp_w1_k, cmp_w2_k, transpose_out=False)
    v_cmp_t = _compress(vc, cmp_pe_v, cmp_w1_v, cmp_w2_v, transpose_out=True)
    y_nsa = _nsa(qn, qr, k_cmp, v_cmp_t, _overlap_t(s), ks, vst, kw, vwt, gt, tq)

    mem_k, mem_v = _memkv(mem[0], row(mem_norm_g), mem_w_kv.astype(BF16))
    h = _mixout(yn, bonus, gate, y_nsa, h, row(rw_ln_w), row(rw_ln_b),
                w_out[:RW_WIDTH].astype(BF16), w_out[RW_WIDTH:].astype(BF16), row(mix_post_g),
                row(mem_pre_g), mem_w_q.astype(BF16), mem_k, mem_v, mem_w_o.astype(BF16), row(mem_post_g))

    h = _ffn(h, row(ffn2_pre_g), *ffn_weights(ffn2_w_gate, ffn2_w_up, ffn2_w_down), row(ffn2_post_g))
    return h[None]
```

```python
import functools

import jax
import jax.numpy as jnp
from jax import lax
from jax.experimental import pallas as pl
from jax.experimental.pallas import tpu as pltpu

F32 = jnp.float32
BF16 = jnp.bfloat16

D_MODEL = 2048
D_FF = 5504
D_FF_PAD = 5632
EPS = 1e-6

RW_HEADS = 16
RW_N = 64
RW_WIDTH = 1024
RW_LORA = 288
RW_LORA_PAD = 384
RW_GN_EPS = 64e-5
RW_CHUNK = 64

NSA_HEADS = 16
NSA_G = 4
NSA_HPG = 4
NSA_DH = 64
NSA_KV = 256
CMP_STRIDE = 16
SEL_BLOCK = 64
SEL_TOPK = 16
WINDOW = 512
FORCE_BONUS = 1000.0
ROPE_THETA = 500000.0
ROPE_DIM = 16

MEM_HEADS = 4
MEM_DH = 128

LOG2E = 1.4426950408889634
SEL_CHUNK = 256
SEL_UNROLL = 4
BIAS_ROWS = 16
V_ROWS = NSA_DH + 16

NEG = -1e30
VMEM_LIMIT = 56 * 1024 * 1024


def _cparams(sem):
    return pltpu.CompilerParams(dimension_semantics=sem, vmem_limit_bytes=VMEM_LIMIT)


def _rms(x, g):
    return x * lax.rsqrt(jnp.mean(x * x, axis=-1, keepdims=True) + EPS) * g


def _dot(a, b):
    return jnp.dot(a, b, preferred_element_type=F32)


def _dot_nt(a, b):
    return lax.dot_general(a, b, (((1,), (1,)), ((), ())), preferred_element_type=F32)


def _split2(x):
    hi = x.astype(BF16)
    lo = (x - hi.astype(F32)).astype(BF16)
    return hi, lo


def _split3(x):
    hi = x.astype(BF16)
    r1 = x - hi.astype(F32)
    mid = r1.astype(BF16)
    lo = (r1 - mid.astype(F32)).astype(BF16)
    return hi, mid, lo


def _ffn_body(h_ref, gpre_ref, wg_ref, wu_ref, wd_ref, gpost_ref, o_ref, xn_ref, *, nj):
    j = pl.program_id(1)

    @pl.when(j == 0)
    def _():
        xn_ref[...] = _rms(h_ref[...], gpre_ref[...]).astype(BF16)
        o_ref[...] = jnp.zeros_like(o_ref)

    xn = xn_ref[...]
    g = _dot(xn, wg_ref[...])
    u = _dot(xn, wu_ref[...])
    a = (g * jax.nn.sigmoid(g) * u).astype(BF16)
    o_ref[...] += _dot(a, wd_ref[...])

    @pl.when(j == nj - 1)
    def _():
        o_ref[...] = h_ref[...] + 0.5 * _rms(o_ref[...], gpost_ref[...])


def _ffn(h, pre_g, wg, wu, wd, post_g, tm=512, tf=512):
    s, d = h.shape
    ffp = wg.shape[1]
    nj = ffp // tf
    return pl.pallas_call(
        functools.partial(_ffn_body, nj=nj),
        grid=(s // tm, nj),
        in_specs=[
            pl.BlockSpec((tm, d), lambda i, j: (i, 0)),
            pl.BlockSpec((1, d), lambda i, j: (0, 0)),
            pl.BlockSpec((d, tf), lambda i, j: (0, j)),
            pl.BlockSpec((d, tf), lambda i, j: (0, j)),
            pl.BlockSpec((tf, d), lambda i, j: (j, 0)),
            pl.BlockSpec((1, d), lambda i, j: (0, 0)),
        ],
        out_specs=pl.BlockSpec((tm, d), lambda i, j: (i, 0)),
        out_shape=jax.ShapeDtypeStruct((s, d), F32),
        scratch_shapes=[pltpu.VMEM((tm, d), BF16)],
        compiler_params=_cparams(("parallel", "arbitrary")),
        name="ffn",
    )(h, pre_g, wg, wu, wd, post_g)


def _inproj_body(h_ref, g_ref, w_ref, o_ref, xn_ref):
    @pl.when(pl.program_id(1) == 0)
    def _():
        xn_ref[...] = _rms(h_ref[...], g_ref[...]).astype(BF16)

    o_ref[...] = _dot(xn_ref[...], w_ref[...])


def _inproj(h, g, w, tm=512, tn=2048):
    s, d = h.shape
    n = w.shape[1]
    return pl.pallas_call(
        _inproj_body,
        grid=(s // tm, n // tn),
        in_specs=[
            pl.BlockSpec((tm, d), lambda i, j: (i, 0)),
            pl.BlockSpec((1, d), lambda i, j: (0, 0)),
            pl.BlockSpec((d, tn), lambda i, j: (0, j)),
        ],
        out_specs=pl.BlockSpec((tm, tn), lambda i, j: (i, j)),
        out_shape=jax.ShapeDtypeStruct((s, n), F32),
        scratch_shapes=[pltpu.VMEM((tm, d), BF16)],
        compiler_params=_cparams(("parallel", "arbitrary")),
        name="inproj",
    )(h, g, w)


def _rwprep_body(f_ref, fp_ref, lo_ref, lop_ref, mua_ref, mul_ref, w0_ref, a0_ref, kk_ref, ka_ref,
                 rk_ref, wup_ref, aup_ref, gup_ref, e_ref, tri_ref,
                 at_ref, rt_ref, bt_ref, kt_ref, v_ref, pt_ref, g_ref, bon_ref, *, tm):
    first = pl.program_id(0) == 0

    def shifted(x, prev_blk):
        prev_last = jnp.where(first, 0.0, prev_blk[7:8, :])
        rolled = pltpu.roll(x, 1, 0)
        row = lax.broadcasted_iota(jnp.int32, x.shape, 0)
        return jnp.where(row == 0, prev_last, rolled)

    f = f_ref[...]
    fs = f + mua_ref[...] * (shifted(f, fp_ref[...]) - f)
    lo = lo_ref[...]
    los = lo + mul_ref[...] * (shifted(lo, lop_ref[...]) - lo)
    r = fs[:, :RW_WIDTH]
    k = fs[:, RW_WIDTH:2 * RW_WIDTH]
    v = fs[:, 2 * RW_WIDTH:]

    wx = _dot(jnp.tanh(los).astype(BF16), wup_ref[...])
    ax = _dot(los.astype(BF16), aup_ref[...])
    g_ref[...] = _dot(jax.nn.sigmoid(los).astype(BF16), gup_ref[...])

    z = -(w0_ref[...] + wx)
    softplus = jnp.maximum(z, 0.0) + jnp.log(1.0 + jnp.exp(-jnp.abs(z)))
    ld = -jnp.exp(-softplus - 0.5)
    a = jax.nn.sigmoid(a0_ref[...] + ax)

    e = e_ref[...]

    def headsum(x):
        parts = []
        for c in range(RW_WIDTH // 256):
            hi, lo_ = _split2(x[:, c * 256:(c + 1) * 256])
            parts.append(_dot(hi, e) + _dot(lo_, e))
        return jnp.concatenate(parts, axis=1)

    kk0 = k * kk_ref[...]
    kk = kk0 / jnp.maximum(jnp.sqrt(headsum(kk0 * kk0)), 1e-12)
    kmod = k * (1.0 + (a - 1.0) * ka_ref[...])
    bon_ref[...] = headsum(r * kmod * rk_ref[...]) * v

    tri = tri_ref[...]
    cums = []
    for c in range(tm // RW_CHUNK):
        hi, mid, lo_ = _split3(ld[c * RW_CHUNK:(c + 1) * RW_CHUNK])
        cums.append(_dot(tri, hi) + _dot(tri, mid) + _dot(tri, lo_))
    cum = jnp.concatenate(cums, axis=0)
    e_in = jnp.exp(cum)
    e_out = jnp.exp(-cum)
    at = -kk * jnp.exp(cum - ld)
    rt = r * e_in
    bt = kk * a * e_out
    kt = kmod * e_out
    for h in range(RW_HEADS):
        sl = slice(h * RW_N, (h + 1) * RW_N)
        at_ref[h] = at[:, sl]
        rt_ref[h] = rt[:, sl]
        bt_ref[h] = bt[:, sl]
        kt_ref[h] = kt[:, sl]
        v_ref[h] = v[:, sl]
    for c in range(tm // RW_CHUNK):
        last = e_in[c * RW_CHUNK + RW_CHUNK - 1:(c + 1) * RW_CHUNK, :]
        for h in range(RW_HEADS):
            pt_ref[h, c * 8:(c + 1) * 8, :] = jnp.broadcast_to(last[:, h * RW_N:(h + 1) * RW_N], (8, RW_N))


def _rwprep(feats, mu_a, mu_l, w0, a0, k_k, k_a, r_k, wup, aup, gup, tm=256):
    s = feats.shape[0]
    nb8 = tm // 8
    lora_blk = (2 * 3072 - RW_LORA_PAD) // RW_LORA_PAD
    e = (lax.broadcasted_iota(jnp.int32, (256, 256), 0) // RW_N
         == lax.broadcasted_iota(jnp.int32, (256, 256), 1) // RW_N).astype(BF16)
    tri = (lax.broadcasted_iota(jnp.int32, (RW_CHUNK, RW_CHUNK), 0)
           >= lax.broadcasted_iota(jnp.int32, (RW_CHUNK, RW_CHUNK), 1)).astype(BF16)
    row = lambda n: pl.BlockSpec((1, n), lambda i: (0, 0))
    full = lambda a, b: pl.BlockSpec((a, b), lambda i: (0, 0))
    head_shape = jax.ShapeDtypeStruct((RW_HEADS, s, RW_N), F32)
    head_spec = pl.BlockSpec((RW_HEADS, tm, RW_N), lambda i: (0, i, 0))
    return pl.pallas_call(
        functools.partial(_rwprep_body, tm=tm),
        grid=(s // tm,),
        in_specs=[
            pl.BlockSpec((tm, 3072), lambda i: (i, 0)),
            pl.BlockSpec((8, 3072), lambda i: (jnp.maximum(i * nb8 - 1, 0), 0)),
            pl.BlockSpec((tm, RW_LORA_PAD), lambda i: (i, lora_blk)),
            pl.BlockSpec((8, RW_LORA_PAD), lambda i: (jnp.maximum(i * nb8 - 1, 0), lora_blk)),
            row(3072), row(RW_LORA_PAD), row(RW_WIDTH), row(RW_WIDTH), row(RW_WIDTH), row(RW_WIDTH),
            row(RW_WIDTH),
            full(RW_LORA_PAD, RW_WIDTH), full(RW_LORA_PAD, RW_WIDTH), full(RW_LORA_PAD, RW_WIDTH),
            full(256, 256), full(RW_CHUNK, RW_CHUNK),
        ],
        out_specs=[head_spec] * 5 + [
            pl.BlockSpec((RW_HEADS, nb8, RW_N), lambda i: (0, i, 0)),
            pl.BlockSpec((tm, RW_WIDTH), lambda i: (i, 0)),
            pl.BlockSpec((tm, RW_WIDTH), lambda i: (i, 0)),
        ],
        out_shape=[head_shape] * 5 + [
            jax.ShapeDtypeStruct((RW_HEADS, s // 8, RW_N), F32),
            jax.ShapeDtypeStruct((s, RW_WIDTH), F32),
            jax.ShapeDtypeStruct((s, RW_WIDTH), F32),
        ],
        compiler_params=_cparams(("parallel",)),
        name="rwprep",
    )(feats, feats, feats, feats, mu_a, mu_l, w0, a0, k_k, k_a, r_k, wup, aup, gup, e, tri)


def _bmm(a, b):
    return jnp.einsum("hij,hjk->hik", a, b, preferred_element_type=F32)


def _bmm_nt(a, b):
    return jnp.einsum("hik,hjk->hij", a, b, preferred_element_type=F32)


def _rwscan_body(at_ref, rt_ref, bt_ref, kt_ref, v_ref, pt_ref, y_ref, s_ref, *, nc):
    @pl.when(pl.program_id(0) == 0)
    def _():
        s_ref[...] = jnp.zeros_like(s_ref)

    t = RW_CHUNK
    row = lax.broadcasted_iota(jnp.int32, (t, t), 0)
    col = lax.broadcasted_iota(jnp.int32, (t, t), 1)
    strict = (row > col)[None]
    incl = (row >= col)[None]
    eye = (row == col).astype(F32)[None]

    def chunk(c, carry):
        r0 = pl.multiple_of(c * t, t)
        a_f = at_ref[:, pl.ds(r0, t), :]
        r_f = rt_ref[:, pl.ds(r0, t), :]
        a_b = a_f.astype(BF16)
        b_b = bt_ref[:, pl.ds(r0, t), :].astype(BF16)
        k_b = kt_ref[:, pl.ds(r0, t), :].astype(BF16)
        v_f = v_ref[:, pl.ds(r0, t), :]
        v_b = v_f.astype(BF16)
        xa = jnp.concatenate([a_b, r_f.astype(BF16)], axis=1)
        gb = _bmm_nt(xa, b_b)
        gk = _bmm_nt(xa, k_b)
        a_ab = jnp.where(strict, gb[:, :t], 0.0)
        a_rb = jnp.where(incl, gb[:, t:], 0.0).astype(BF16)
        a_ak = jnp.where(strict, gk[:, :t], 0.0).astype(BF16)
        a_rk = jnp.where(incl, gk[:, t:], 0.0).astype(BF16)
        inv = eye + a_ab
        pk = a_ab
        for _ in range(5):
            pkb = pk.astype(BF16)
            pk = _bmm(pkb, pkb)
            inv = inv + _bmm(inv.astype(BF16), pk.astype(BF16))
        inv_b = inv.astype(BF16)
        w1 = _bmm(a_ak, v_b)
        atp = _bmm(inv_b, a_b)
        z0 = _bmm(inv_b, w1.astype(BF16))
        rh = r_f + _bmm(a_rb, atp.astype(BF16))
        y0 = _bmm(a_rb, z0.astype(BF16)) + _bmm(a_rk, v_b)
        atp_t = jnp.swapaxes(atp, 1, 2).astype(BF16)
        z0_t = jnp.swapaxes(z0, 1, 2).astype(BF16)
        v_t = jnp.swapaxes(v_f, 1, 2).astype(BF16)
        pt = pt_ref[:, pl.ds(pl.multiple_of(c * 8, 8), 8), :][:, 0:1, :]
        m = (eye + _bmm(atp_t, b_b)) * pt
        cc = (_bmm(z0_t, b_b) + _bmm(v_t, k_b)) * pt
        st = s_ref[...]
        st_b = st.astype(BF16)
        y = _bmm_nt(rh.astype(BF16), st_b) + y0
        s_ref[...] = _bmm(st_b, m.astype(BF16)) + cc
        mean = jnp.mean(y, axis=-1, keepdims=True)
        yc = y - mean
        var = jnp.mean(yc * yc, axis=-1, keepdims=True)
        yn = yc * lax.rsqrt(var + RW_GN_EPS)
        y_ref[pl.ds(r0, t), :] = jnp.concatenate([yn[h] for h in range(RW_HEADS)], axis=-1)
        return carry

    lax.fori_loop(0, nc, chunk, 0)


def _rwscan(at, rt, bt, kt, v, pt, rows=256):
    s = at.shape[1]
    nc = rows // RW_CHUNK
    head_spec = pl.BlockSpec((RW_HEADS, rows, RW_N), lambda i: (0, i, 0))
    return pl.pallas_call(
        functools.partial(_rwscan_body, nc=nc),
        grid=(s // rows,),
        in_specs=[head_spec] * 5 + [pl.BlockSpec((RW_HEADS, rows // 8, RW_N), lambda i: (0, i, 0))],
        out_specs=pl.BlockSpec((rows, RW_WIDTH), lambda i: (i, 0)),
        out_shape=jax.ShapeDtypeStruct((s, RW_WIDTH), F32),
        scratch_shapes=[pltpu.VMEM((RW_HEADS, RW_N, RW_N), F32)],
        compiler_params=_cparams(("arbitrary",)),
        name="rwscan",
    )(at, rt, bt, kt, v, pt)


def _nsaprep_body(f_ref, cos_ref, sin_ref, qn_ref, qr_ref, kc_ref, vc_ref, ks_ref, vst_ref, kw_ref,
                  vwt_ref, gt_ref, *, tq):
    cs = cos_ref[...]
    sn = sin_ref[...]
    lane = lax.broadcasted_iota(jnp.int32, (tq, 128), 1)
    first8 = (lane % NSA_DH) < (ROPE_DIM // 2)

    def rope(x):
        swapped = jnp.where(first8, pltpu.roll(x, 128 - ROPE_DIM // 2, 1), pltpu.roll(x, ROPE_DIM // 2, 1))
        return x * cs + swapped * sn

    scale = NSA_DH ** -0.5 * LOG2E
    lane_hi = lane - NSA_DH
    blk_in_chunk = (lax.broadcasted_iota(jnp.int32, (tq, 128), 0) % SEL_CHUNK) // SEL_BLOCK
    onehot = ((lane_hi >= 0) & (lane_hi % BIAS_ROWS == 0) & (lane_hi // BIAS_ROWS == blk_in_chunk)).astype(F32)
    ones_rows = jnp.ones((V_ROWS - NSA_DH, tq), BF16)
    for p in range(NSA_HEADS // 2):
        x = f_ref[:, p * 128:(p + 1) * 128]
        xn_t = (x * scale).T
        xr_t = (rope(x) * scale).T
        for e in range(2):
            hd = 2 * p + e
            g, h = hd // NSA_HPG, hd % NSA_HPG
            qn_ref[g, 0, :, h * tq:(h + 1) * tq] = xn_t[e * NSA_DH:(e + 1) * NSA_DH].astype(BF16)
            qr_ref[g, 0, :, h * tq:(h + 1) * tq] = xr_t[e * NSA_DH:(e + 1) * NSA_DH].astype(BF16)

    def kv_piece(idx, p):
        base = NSA_HEADS * NSA_DH + idx * NSA_KV + p * 128
        return f_ref[:, base:base + 128]

    for p in range(2):
        kc = kv_piece(0, p)
        vc = kv_piece(1, p)
        ks = rope(kv_piece(2, p))
        vs_t = kv_piece(3, p).T
        kw = rope(kv_piece(4, p))
        vw_t = kv_piece(5, p).T
        for e in range(2):
            g = 2 * p + e
            sl = slice(e * NSA_DH, (e + 1) * NSA_DH)
            kc_ref[g] = kc[:, sl]
            vc_ref[g] = vc[:, sl]
            ks_low = ks if e == 0 else pltpu.roll(ks, NSA_DH, 1)
            ks_ref[g] = jnp.where(lane < NSA_DH, ks_low, onehot).astype(BF16)
            kw_low = kw if e == 0 else pltpu.roll(kw, NSA_DH, 1)
            kw_ref[g] = jnp.where(lane < NSA_DH, kw_low, jnp.where(lane == NSA_DH, 1.0, 0.0)).astype(BF16)
            vst_ref[g, :NSA_DH] = vs_t[sl].astype(BF16)
            vst_ref[g, NSA_DH:] = ones_rows
            vwt_ref[g, :NSA_DH] = vw_t[sl].astype(BF16)
            vwt_ref[g, NSA_DH:] = ones_rows
    gl = f_ref[:, 2560:2688]
    gt_ref[...] = jax.nn.sigmoid(gl).T[:3 * NSA_HEADS]


def _nsaprep(feats, cos_t, sin_t, tq):
    s = feats.shape[0]
    ni = s // tq
    kv_f32 = jax.ShapeDtypeStruct((NSA_G, s, NSA_DH), F32)
    kv_b16 = jax.ShapeDtypeStruct((NSA_G, s, NSA_DH), BF16)
    ksa_b16 = jax.ShapeDtypeStruct((NSA_G, s, 2 * NSA_DH), BF16)
    kvt_b16 = jax.ShapeDtypeStruct((NSA_G, V_ROWS, s), BF16)
    q_shape = jax.ShapeDtypeStruct((NSA_G, ni, NSA_DH, NSA_HPG * tq), BF16)
    q_spec = pl.BlockSpec((NSA_G, 1, NSA_DH, NSA_HPG * tq), lambda i: (0, i, 0, 0))
    kv_spec = pl.BlockSpec((NSA_G, tq, NSA_DH), lambda i: (0, i, 0))
    ksa_spec = pl.BlockSpec((NSA_G, tq, 2 * NSA_DH), lambda i: (0, i, 0))
    kvt_spec = pl.BlockSpec((NSA_G, V_ROWS, tq), lambda i: (0, 0, i))
    return pl.pallas_call(
        functools.partial(_nsaprep_body, tq=tq),
        grid=(ni,),
        in_specs=[
            pl.BlockSpec((tq, 3072), lambda i: (i, 1)),
            pl.BlockSpec((tq, 128), lambda i: (i, 0)),
            pl.BlockSpec((tq, 128), lambda i: (i, 0)),
        ],
        out_specs=[q_spec, q_spec, kv_spec, kv_spec, ksa_spec, kvt_spec, ksa_spec, kvt_spec,
                   pl.BlockSpec((3 * NSA_HEADS, tq), lambda i: (0, i))],
        out_shape=[q_shape, q_shape, kv_f32, kv_f32, ksa_b16, kvt_b16, ksa_b16, kvt_b16,
                   jax.ShapeDtypeStruct((3 * NSA_HEADS, s), F32)],
        compiler_params=_cparams(("parallel",)),
        name="nsaprep",
    )(feats, cos_t, sin_t)


def _gelu_tanh(x):
    return 0.5 * x * (1.0 + jnp.tanh(0.7978845608028654 * (x + 0.044715 * x * x * x)))


def _compress_body(x_ref, pe1_ref, pe2_ref, w1a_ref, w1b_ref, w2_ref, o_ref, *, ncp, transpose_out):
    x = x_ref[0]
    a = _dot((x + pe1_ref[...]).astype(BF16), w1a_ref[...])
    b = _dot((x + pe2_ref[...]).astype(BF16), w1b_ref[...])
    hid = a + pltpu.roll(b, ncp - 1, 0)
    act = _gelu_tanh(hid).astype(BF16)
    if transpose_out:
        o_ref[0, :NSA_DH] = _dot_nt(w2_ref[...], act).astype(BF16)
        o_ref[0, NSA_DH:] = jnp.ones((V_ROWS - NSA_DH, ncp), BF16)
    else:
        o_ref[0] = _dot(act, w2_ref[...]).astype(BF16)


def _compress(x, pe, w1, w2, transpose_out):
    g, s, dh = x.shape
    ncp = s // CMP_STRIDE
    half = CMP_STRIDE * dh
    xr = x.reshape(g, ncp, half)
    pe1 = pe[:CMP_STRIDE].reshape(1, half)
    pe2 = pe[CMP_STRIDE:].reshape(1, half)
    w1a = w1[:half].astype(BF16)
    w1b = w1[half:].astype(BF16)
    hid = w1.shape[1]
    w2k = (w2.T if transpose_out else w2).astype(BF16)
    full = lambda a, b: pl.BlockSpec((a, b), lambda i: (0, 0))
    if transpose_out:
        out_shape = jax.ShapeDtypeStruct((g, V_ROWS, ncp), BF16)
        out_spec = pl.BlockSpec((1, V_ROWS, ncp), lambda i: (i, 0, 0))
    else:
        out_shape = jax.ShapeDtypeStruct((g, ncp, dh), BF16)
        out_spec = pl.BlockSpec((1, ncp, dh), lambda i: (i, 0, 0))
    return pl.pallas_call(
        functools.partial(_compress_body, ncp=ncp, transpose_out=transpose_out),
        grid=(g,),
        in_specs=[
            pl.BlockSpec((1, ncp, half), lambda i: (i, 0, 0)),
            full(1, half), full(1, half), full(half, hid), full(half, hid), full(*w2k.shape),
        ],
        out_specs=out_spec,
        out_shape=out_shape,
        compiler_params=_cparams(("parallel",)),
        name="compress",
    )(xr, pe1, pe2, w1a, w1b, w2k)


def _rows_bf16(row, n):
    r16 = jnp.broadcast_to(row, (16, row.shape[1])).astype(BF16)
    return jnp.concatenate([r16] * (n // 16), axis=0)


def _flash_biased(sb, vt_blk, m_ref, acc_ref, h):
    m_old = m_ref[h]
    m_new = jnp.maximum(m_old, jnp.max(sb, axis=0, keepdims=True).astype(F32))
    alpha = jnp.exp2(m_old - m_new)
    p = jnp.exp2(sb - _rows_bf16(m_new, sb.shape[0]))
    acc_ref[h] = alpha * acc_ref[h] + _dot(vt_blk, p)
    m_ref[h] = m_new


def _nsa_body(qn_ref, qr_ref, kc_ref, vct_ref, ov_ref, ks_ref, vst_ref, kw_ref, vwt_ref, gt_ref, o_ref,
              bias_ref, qa_ref, qw_ref, st0_ref, st1_ref, st2_ref, st3_ref, m_ref, acc_ref, *, tq, ncp, nsel, topn):
    st_refs = (st0_ref, st1_ref, st2_ref, st3_ref)
    g = pl.program_id(0)
    i = pl.program_id(1)
    q0 = i * tq
    bpc = tq // SEL_BLOCK
    tpos = q0 + lax.broadcasted_iota(jnp.int32, (1, tq), 1)
    heads = range(NSA_HPG)
    hcols = lambda h: slice(h * tq, (h + 1) * tq)

    row_io = lax.broadcasted_iota(jnp.int32, (tq, 1), 0)
    col_io = lax.broadcasted_iota(jnp.int32, (1, tq), 1)
    as_bias = lambda keep: jnp.where(keep, 0.0, NEG).astype(BF16)
    causal_bias = as_bias(row_io <= col_io)
    anti_bias = as_bias(row_io > col_io)

    ncc = ncp // tq
    cmp_last = 2 * CMP_STRIDE - 1
    cbias = [as_bias((cc * tq + row_io) * CMP_STRIDE + cmp_last <= tpos) for cc in range(ncc)]
    has_cmp = jnp.where(tpos >= cmp_last, 1.0, 0.0)
    imp = jnp.zeros((nsel, tq), F32)
    o_c = []
    for h in heads:
        qn_h = qn_ref[0, 0, :, hcols(h)]
        sbs = [_dot(kc_ref[0, cc * tq:(cc + 1) * tq, :], qn_h).astype(BF16) + cbias[cc] for cc in range(ncc)]
        mc = jnp.max(sbs[0], axis=0, keepdims=True)
        for sb in sbs[1:]:
            mc = jnp.maximum(mc, jnp.max(sb, axis=0, keepdims=True))
        mrows = _rows_bf16(mc.astype(F32), tq)
        pc = jnp.concatenate([jnp.exp2(sb - mrows) for sb in sbs], axis=0)
        rv = _dot(vct_ref[0], pc)
        inv = has_cmp / rv[NSA_DH:NSA_DH + 1]
        o_c.append(rv[:NSA_DH] * inv)
        imp = imp + _dot(ov_ref[...], pc) * inv

    jj = lax.broadcasted_iota(jnp.int32, (nsel, tq), 0)
    jf = jj.astype(F32)
    cur = (q0 + lax.broadcasted_iota(jnp.int32, (1, tq), 1)) // SEL_BLOCK
    forced = (jj == 0) | (jj == cur) | (jj == cur - 1)
    score = jnp.where(forced, -3e38, jnp.where(jj <= cur, imp, -1.0))
    sel = jnp.where(forced, 1.0, 0.0)
    for _ in range(topn - 3):
        mx = jnp.max(score, axis=0, keepdims=True)
        first = jnp.min(jnp.where(score == mx, jf, float(nsel)), axis=0, keepdims=True)
        hit = jf == first
        sel = jnp.where(hit, 1.0, sel)
        score = jnp.where(hit, -3e38, score)
    bias_ref[...] = jnp.where(sel > 0.0, 0.0, NEG)

    def reset():
        m_ref[...] = jnp.full_like(m_ref, NEG)
        acc_ref[...] = jnp.zeros_like(acc_ref)

    def finish(h):
        acc = acc_ref[h]
        l = acc[NSA_DH:NSA_DH + 1]
        return acc[:NSA_DH] * jnp.where(l > 0.0, 1.0 / l, 0.0)

    row16 = lax.broadcasted_iota(jnp.int32, (BIAS_ROWS, NSA_HPG * tq), 0)

    qa_ref[:NSA_DH] = qr_ref[0, 0]

    def set_bias_rows(c):
        for b in range(bpc):
            brow = bias_ref[pl.ds(c * bpc + b, 1), :]
            brow4 = jnp.concatenate([brow] * NSA_HPG, axis=1)
            lo_row = NSA_DH + BIAS_ROWS * b
            qa_ref[lo_row:lo_row + BIAS_ROWS] = jnp.where(row16 == 0, brow4, 0.0).astype(BF16)

    def sel_scores(c, h):
        kblk = ks_ref[0, pl.ds(pl.multiple_of(c * tq, tq), tq), :]
        return _dot(kblk, qa_ref[:, hcols(h)]).astype(BF16)

    reset()
    set_bias_rows(0)
    for h in heads:
        st_refs[h][...] = sel_scores(0, h)

    def sel_chunk(c):
        set_bias_rows(c + 1)
        vblk = vst_ref[0, :, pl.ds(pl.multiple_of(c * tq, tq), tq)]
        nxt = sel_scores(c + 1, 0)
        for h in heads:
            cur_scores = st_refs[h][...]
            after = sel_scores(c + 1, h + 1) if h + 1 < NSA_HPG else None
            _flash_biased(cur_scores, vblk, m_ref, acc_ref, h)
            st_refs[h][...] = nxt
            nxt = after

    def sel_group(cg, carry):
        for u in range(SEL_UNROLL):
            sel_chunk(SEL_UNROLL * cg + u)
        return carry

    lax.fori_loop(0, i // SEL_UNROLL, sel_group, 0)
    done = (i // SEL_UNROLL) * SEL_UNROLL
    part = SEL_UNROLL // 2
    while part >= 1:
        @pl.when((i & part) != 0)
        def _(base=done, part=part):
            for u in range(part):
                sel_chunk(base + u)
        done = done + (i & part)
        part //= 2

    vblk = vst_ref[0, :, pl.ds(pl.multiple_of(q0, tq), tq)]
    for h in heads:
        _flash_biased(st_refs[h][...] + causal_bias, vblk, m_ref, acc_ref, h)
    o_s = [finish(h) for h in heads]

    reset()
    nback = WINDOW // tq
    win_bias = [causal_bias] + [None] * (nback - 1) + [anti_bias]
    qw_ref[:NSA_DH] = qr_ref[0, 0]
    qw_ref[NSA_DH + BIAS_ROWS:] = jnp.zeros((NSA_DH - BIAS_ROWS, NSA_HPG * tq), BF16)

    def stash_window(j, h):
        exists = jnp.where(i >= j, 0.0, NEG)
        if h == 0:
            qw_ref[NSA_DH:NSA_DH + BIAS_ROWS] = jnp.where(row16 == 0, exists, 0.0).astype(BF16)
        kblk = kw_ref[0, pl.ds(pl.multiple_of(jnp.maximum(i - j, 0) * tq, tq), tq), :]
        sb = _dot(kblk, qw_ref[:, hcols(h)]).astype(BF16)
        st_refs[h][...] = sb if win_bias[j] is None else sb + win_bias[j]

    for h in heads:
        stash_window(0, h)
    for j in range(nback + 1):
        vblk = vwt_ref[0, :, pl.ds(pl.multiple_of(jnp.maximum(i - j, 0) * tq, tq), tq)]
        for h in heads:
            _flash_biased(st_refs[h][...], vblk, m_ref, acc_ref, h)
            if j < nback:
                stash_window(j + 1, h)
    o_w = [finish(h) for h in heads]

    def gate(branch, h):
        return gt_ref[pl.ds(branch * NSA_HEADS + g * NSA_HPG + h, 1), :]

    out_t = [gate(0, h) * o_c[h] + gate(1, h) * o_s[h] + gate(2, h) * o_w[h] for h in heads]
    halves = [jnp.concatenate(out_t[2 * p:2 * p + 2], axis=0).T for p in range(NSA_HPG // 2)]
    o_ref[...] = jnp.concatenate(halves, axis=1)


def _nsa(qn, qr, kc, vct, ov, ks, vst, kw, vwt, gt, tq):
    g, ni, dh, w4 = qn.shape
    s = ks.shape[1]
    ncp = kc.shape[1]
    nsel = s // SEL_BLOCK
    topn = min(SEL_TOPK, nsel)
    assert tq == SEL_CHUNK and (tq // SEL_BLOCK) * BIAS_ROWS == dh and w4 == NSA_HPG * tq
    q_spec = pl.BlockSpec((1, 1, dh, w4), lambda a, b: (a, b, 0, 0))
    return pl.pallas_call(
        functools.partial(_nsa_body, tq=tq, ncp=ncp, nsel=nsel, topn=topn),
        grid=(g, ni),
        in_specs=[
            q_spec, q_spec,
            pl.BlockSpec((1, ncp, dh), lambda a, b: (a, 0, 0)),
            pl.BlockSpec((1, V_ROWS, ncp), lambda a, b: (a, 0, 0)),
            pl.BlockSpec((nsel, ncp), lambda a, b: (0, 0)),
            pl.BlockSpec((1, s, 2 * dh), lambda a, b: (a, 0, 0)),
            pl.BlockSpec((1, V_ROWS, s), lambda a, b: (a, 0, 0)),
            pl.BlockSpec((1, s, 2 * dh), lambda a, b: (a, 0, 0)),
            pl.BlockSpec((1, V_ROWS, s), lambda a, b: (a, 0, 0)),
            pl.BlockSpec((3 * NSA_HEADS, tq), lambda a, b: (0, b)),
        ],
        out_specs=pl.BlockSpec((tq, NSA_HPG * dh), lambda a, b: (b, a)),
        out_shape=jax.ShapeDtypeStruct((s, NSA_HEADS * dh), F32),
        scratch_shapes=[
            pltpu.VMEM((nsel, tq), F32),
            pltpu.VMEM((2 * dh, w4), BF16),
            pltpu.VMEM((2 * dh, w4), BF16),
            pltpu.VMEM((tq, tq), BF16), pltpu.VMEM((tq, tq), BF16),
            pltpu.VMEM((tq, tq), BF16), pltpu.VMEM((tq, tq), BF16),
            pltpu.VMEM((NSA_HPG, 1, tq), F32),
            pltpu.VMEM((NSA_HPG, V_ROWS, tq), F32),
        ],
        compiler_params=_cparams(("arbitrary", "arbitrary")),
        name="nsa",
    )(qn, qr, kc, vct, ov, ks, vst, kw, vwt, gt)


def _memkv_body(mem_ref, g_ref, w_ref, k_ref, v_ref):
    kv = _dot(_rms(mem_ref[...], g_ref[...]).astype(BF16), w_ref[...])
    width = MEM_HEADS * MEM_DH
    k_ref[...] = kv[:, :width].astype(BF16)
    v_ref[...] = kv[:, width:].astype(BF16)


def _memkv(mem, g, w):
    m, d = mem.shape
    width = MEM_HEADS * MEM_DH
    shp = jax.ShapeDtypeStruct((m, width), BF16)
    return pl.pallas_call(
        _memkv_body,
        out_shape=[shp, shp],
        compiler_params=pltpu.CompilerParams(vmem_limit_bytes=VMEM_LIMIT),
        name="memkv",
    )(mem, g, w)


def _mixout_body(yn_ref, bon_ref, gate_ref, ynsa_ref, h_ref, lnw_ref, lnb_ref, wo_rw_ref, wo_nsa_ref,
                 gpost_ref, mpre_ref, wq_ref, k_ref, v_ref, wom_ref, mpost_ref, o_ref):
    y_rw = ((yn_ref[...] * lnw_ref[...] + lnb_ref[...]) + bon_ref[...]) * gate_ref[...]
    y = _dot(y_rw.astype(BF16), wo_rw_ref[...]) + _dot(ynsa_ref[...].astype(BF16), wo_nsa_ref[...])
    h2 = h_ref[...] + _rms(y, gpost_ref[...])
    q = _dot(_rms(h2, mpre_ref[...]).astype(BF16), wq_ref[...])
    scale = MEM_DH ** -0.5
    outs = []
    for hh in range(MEM_HEADS):
        sl = slice(hh * MEM_DH, (hh + 1) * MEM_DH)
        s = _dot_nt(q[:, sl].astype(BF16), k_ref[:, sl]) * scale
        p = jnp.exp(s - jnp.max(s, axis=-1, keepdims=True))
        p = p / jnp.sum(p, axis=-1, keepdims=True)
        outs.append(_dot(p.astype(BF16), v_ref[:, sl]))
    o = jnp.concatenate(outs, axis=-1).astype(BF16)
    m = _dot(o, wom_ref[...])
    o_ref[...] = h2 + _rms(m, mpost_ref[...])


def _mixout(yn, bon, gate, ynsa, h, lnw, lnb, wo_rw, wo_nsa, gpost, mpre, wq, k, v, wom, mpost, tm=256):
    s, d = h.shape
    rows = lambda n: pl.BlockSpec((tm, n), lambda i: (i, 0))
    full = lambda a: pl.BlockSpec(a.shape, lambda i: (0, 0))
    return pl.pallas_call(
        _mixout_body,
        grid=(s // tm,),
        in_specs=[rows(RW_WIDTH), rows(RW_WIDTH), rows(RW_WIDTH), rows(RW_WIDTH), rows(d),
                  full(lnw), full(lnb), full(wo_rw), full(wo_nsa), full(gpost), full(mpre), full(wq),
                  full(k), full(v), full(wom), full(mpost)],
        out_specs=rows(d),
        out_shape=jax.ShapeDtypeStruct((s, d), F32),
        compiler_params=_cparams(("parallel",)),
        name="mixout",
    )(yn, bon, gate, ynsa, h, lnw, lnb, wo_rw, wo_nsa, gpost, mpre, wq, k, v, wom, mpost)


def _rope_tables(s):
    half = ROPE_DIM // 2
    inv_freq = ROPE_THETA ** (-jnp.arange(half, dtype=F32) * 2.0 / ROPE_DIM)
    ang = jnp.arange(s, dtype=jnp.int32).astype(F32)[:, None] * inv_freq[None, :]
    cos, sin = jnp.cos(ang), jnp.sin(ang)
    ones = jnp.ones((s, NSA_DH - ROPE_DIM), F32)
    zeros = jnp.zeros((s, NSA_DH - ROPE_DIM), F32)
    cos_h = jnp.concatenate([cos, cos, ones], axis=1)
    sin_h = jnp.concatenate([-sin, sin, zeros], axis=1)
    return jnp.tile(cos_h, (1, 2)), jnp.tile(sin_h, (1, 2))


def _overlap_t(s):
    ncp = s // CMP_STRIDE
    nsel = s // SEL_BLOCK
    cmp_start = jnp.arange(ncp)[None, :] * CMP_STRIDE
    sel_start = jnp.arange(nsel)[:, None] * SEL_BLOCK
    ov = (cmp_start < sel_start + SEL_BLOCK) & (cmp_start + 2 * CMP_STRIDE - 1 >= sel_start)
    ov = ov & (jnp.arange(ncp)[None, :] < ncp - 1)
    return ov.astype(BF16)


def kernel(x, mem, ffn1_pre_g, ffn1_w_gate, ffn1_w_up, ffn1_w_down, ffn1_post_g, mix_pre_g, w_in, rw_mu, rw_w0, rw_w_up, rw_a0, rw_a_up, rw_g_up, rw_k_k, rw_k_a, rw_r_k, rw_ln_w, rw_ln_b, cmp_pe_k, cmp_w1_k, cmp_w2_k, cmp_pe_v, cmp_w1_v, cmp_w2_v, w_out, mix_post_g, mem_pre_g, mem_norm_g, mem_w_q, mem_w_kv, mem_w_o, mem_post_g, ffn2_pre_g, ffn2_w_gate, ffn2_w_up, ffn2_w_down, ffn2_post_g):
    b, s, d = x.shape
    tq = SEL_CHUNK
    assert b == 1 and d == D_MODEL and s % (CMP_STRIDE * tq) == 0
    row = lambda v: v.reshape(1, -1).astype(F32)
    pad_ff = D_FF_PAD - D_FF

    def ffn_weights(wg, wu, wd):
        return (jnp.pad(wg, ((0, 0), (0, pad_ff))).astype(BF16),
                jnp.pad(wu, ((0, 0), (0, pad_ff))).astype(BF16),
                jnp.pad(wd, ((0, pad_ff), (0, 0))).astype(BF16))

    h = x[0]
    h = _ffn(h, row(ffn1_pre_g), *ffn_weights(ffn1_w_gate, ffn1_w_up, ffn1_w_down), row(ffn1_post_g))

    rw_cols = 3 * RW_WIDTH + RW_LORA
    nsa_main = NSA_HEADS * NSA_DH + 6 * NSA_KV
    zc = lambda n: jnp.zeros((d, n), F32)
    w_in_r = jnp.concatenate([
        w_in[:, :3 * RW_WIDTH],
        w_in[:, rw_cols:rw_cols + nsa_main],
        w_in[:, rw_cols + nsa_main:], zc(128 - 3 * NSA_HEADS),
        w_in[:, 3 * RW_WIDTH:rw_cols], zc(RW_LORA_PAD - RW_LORA),
    ], axis=1).astype(BF16)
    feats = _inproj(h, row(mix_pre_g), w_in_r)

    mu_a = row(rw_mu[:3 * RW_WIDTH])
    mu_l = row(jnp.pad(rw_mu[3 * RW_WIDTH:], (0, RW_LORA_PAD - RW_LORA)))
    lora_w = lambda w, off: jnp.pad(w, ((off, RW_LORA_PAD - off - w.shape[0]), (0, 0))).astype(BF16)
    at, rt, bt, kt, v, pt, gate, bonus = _rwprep(
        feats, mu_a, mu_l, row(rw_w0), row(rw_a0), row(rw_k_k), row(rw_k_a), row(rw_r_k),
        lora_w(rw_w_up, 0), lora_w(rw_a_up, 64), lora_w(rw_g_up, 128))
    yn = _rwscan(at, rt, bt, kt, v, pt)

    cos_t, sin_t = _rope_tables(s)
    qn, qr, kc, vc, ks, vst, kw, vwt, gt = _nsaprep(feats, cos_t, sin_t, tq)
    k_cmp = _compress(kc, cmp_pe_k, cmp_w1_k, cmp_w2_k, transpose_out=False)
    v_cmp_t = _compress(vc, cmp_pe_v, cmp_w1_v, cmp_w2_v, transpose_out=True)
    y_nsa = _nsa(qn, qr, k_cmp, v_cmp_t, _overlap_t(s), ks, vst, kw, vwt, gt, tq)

    mem_k, mem_v = _memkv(mem[0], row(mem_norm_g), mem_w_kv.astype(BF16))
    h = _mixout(yn, bonus, gate, y_nsa, h, row(rw_ln_w), row(rw_ln_b),
                w_out[:RW_WIDTH].astype(BF16), w_out[RW_WIDTH:].astype(BF16), row(mix_post_g),
                row(mem_pre_g), mem_w_q.astype(BF16), mem_k, mem_v, mem_w_o.astype(BF16), row(mem_post_g))

    h = _ffn(h, row(ffn2_pre_g), *ffn_weights(ffn2_w_gate, ffn2_w_up, ffn2_w_down), row(ffn2_post_g))
    return h[None]
```

```python
import functools

import jax
import jax.numpy as jnp
from jax import lax
from jax.experimental import pallas as pl
from jax.experimental.pallas import tpu as pltpu

F32 = jnp.float32
BF16 = jnp.bfloat16

D_MODEL = 2048
D_FF = 5504
D_FF_PAD = 5632
EPS = 1e-6

RW_HEADS = 16
RW_N = 64
RW_WIDTH = 1024
RW_LORA = 288
RW_LORA_PAD = 384
RW_GN_EPS = 64e-5
RW_CHUNK = 64

NSA_HEADS = 16
NSA_G = 4
NSA_HPG = 4
NSA_DH = 64
NSA_KV = 256
CMP_STRIDE = 16
SEL_BLOCK = 64
SEL_TOPK = 16
WINDOW = 512
FORCE_BONUS = 1000.0
ROPE_THETA = 500000.0
ROPE_DIM = 16

MEM_HEADS = 4
MEM_DH = 128

LOG2E = 1.4426950408889634
SEL_CHUNK = 256
SEL_UNROLL = 4
BIAS_ROWS = 16
V_ROWS = NSA_DH + 16

NEG = -1e30
VMEM_LIMIT = 56 * 1024 * 1024


def _cparams(sem):
    return pltpu.CompilerParams(dimension_semantics=sem, vmem_limit_bytes=VMEM_LIMIT)


def _rms(x, g):
    return x * lax.rsqrt(jnp.mean(x * x, axis=-1, keepdims=True) + EPS) * g


def _dot(a, b):
    return jnp.dot(a, b, preferred_element_type=F32)


def _dot_nt(a, b):
    return lax.dot_general(a, b, (((1,), (1,)), ((), ())), preferred_element_type=F32)


def _split2(x):
    hi = x.astype(BF16)
    lo = (x - hi.astype(F32)).astype(BF16)
    return hi, lo


def _split3(x):
    hi = x.astype(BF16)
    r1 = x - hi.astype(F32)
    mid = r1.astype(BF16)
    lo = (r1 - mid.astype(F32)).astype(BF16)
    return hi, mid, lo


def _ffn_body(h_ref, gpre_ref, wg_ref, wu_ref, wd_ref, gpost_ref, o_ref, xn_ref, *, nj):
    j = pl.program_id(1)

    @pl.when(j == 0)
    def _():
        xn_ref[...] = _rms(h_ref[...], gpre_ref[...]).astype(BF16)
        o_ref[...] = jnp.zeros_like(o_ref)

    xn = xn_ref[...]
    g = _dot(xn, wg_ref[...])
    u = _dot(xn, wu_ref[...])
    a = (g * jax.nn.sigmoid(g) * u).astype(BF16)
    o_ref[...] += _dot(a, wd_ref[...])

    @pl.when(j == nj - 1)
    def _():
        o_ref[...] = h_ref[...] + 0.5 * _rms(o_ref[...], gpost_ref[...])


def _ffn(h, pre_g, wg, wu, wd, post_g, tm=512, tf=512):
    s, d = h.shape
    ffp = wg.shape[1]
    nj = ffp // tf
    return pl.pallas_call(
        functools.partial(_ffn_body, nj=nj),
        grid=(s // tm, nj),
        in_specs=[
            pl.BlockSpec((tm, d), lambda i, j: (i, 0)),
            pl.BlockSpec((1, d), lambda i, j: (0, 0)),
            pl.BlockSpec((d, tf), lambda i, j: (0, j)),
            pl.BlockSpec((d, tf), lambda i, j: (0, j)),
            pl.BlockSpec((tf, d), lambda i, j: (j, 0)),
            pl.BlockSpec((1, d), lambda i, j: (0, 0)),
        ],
        out_specs=pl.BlockSpec((tm, d), lambda i, j: (i, 0)),
        out_shape=jax.ShapeDtypeStruct((s, d), F32),
        scratch_shapes=[pltpu.VMEM((tm, d), BF16)],
        compiler_params=_cparams(("parallel", "arbitrary")),
        name="ffn",
    )(h, pre_g, wg, wu, wd, post_g)


def _inproj_body(h_ref, g_ref, w_ref, o_ref, xn_ref):
    @pl.when(pl.program_id(1) == 0)
    def _():
        xn_ref[...] = _rms(h_ref[...], g_ref[...]).astype(BF16)

    o_ref[...] = _dot(xn_ref[...], w_ref[...])


def _inproj(h, g, w, tm=512, tn=2048):
    s, d = h.shape
    n = w.shape[1]
    return pl.pallas_call(
        _inproj_body,
        grid=(s // tm, n // tn),
        in_specs=[
            pl.BlockSpec((tm, d), lambda i, j: (i, 0)),
            pl.BlockSpec((1, d), lambda i, j: (0, 0)),
            pl.BlockSpec((d, tn), lambda i, j: (0, j)),
        ],
        out_specs=pl.BlockSpec((tm, tn), lambda i, j: (i, j)),
        out_shape=jax.ShapeDtypeStruct((s, n), F32),
        scratch_shapes=[pltpu.VMEM((tm, d), BF16)],
        compiler_params=_cparams(("parallel", "arbitrary")),
        name="inproj",
    )(h, g, w)


def _rwprep_body(f_ref, fp_ref, lo_ref, lop_ref, mua_ref, mul_ref, w0_ref, a0_ref, kk_ref, ka_ref,
                 rk_ref, wup_ref, aup_ref, gup_ref, e_ref, tri_ref,
                 at_ref, rt_ref, bt_ref, kt_ref, v_ref, pt_ref, g_ref, bon_ref, *, tm):
    first = pl.program_id(0) == 0

    def shifted(x, prev_blk):
        prev_last = jnp.where(first, 0.0, prev_blk[7:8, :])
        rolled = pltpu.roll(x, 1, 0)
        row = lax.broadcasted_iota(jnp.int32, x.shape, 0)
        return jnp.where(row == 0, prev_last, rolled)

    f = f_ref[...]
    fs = f + mua_ref[...] * (shifted(f, fp_ref[...]) - f)
    lo = lo_ref[...]
    los = lo + mul_ref[...] * (shifted(lo, lop_ref[...]) - lo)
    r = fs[:, :RW_WIDTH]
    k = fs[:, RW_WIDTH:2 * RW_WIDTH]
    v = fs[:, 2 * RW_WIDTH:]

    wx = _dot(jnp.tanh(los).astype(BF16), wup_ref[...])
    ax = _dot(los.astype(BF16), aup_ref[...])
    g_ref[...] = _dot(jax.nn.sigmoid(los).astype(BF16), gup_ref[...])

    z = -(w0_ref[...] + wx)
    softplus = jnp.maximum(z, 0.0) + jnp.log(1.0 + jnp.exp(-jnp.abs(z)))
    ld = -jnp.exp(-softplus - 0.5)
    a = jax.nn.sigmoid(a0_ref[...] + ax)

    e = e_ref[...]

    def headsum(x):
        parts = []
        for c in range(RW_WIDTH // 256):
            hi, lo_ = _split2(x[:, c * 256:(c + 1) * 256])
            parts.append(_dot(hi, e) + _dot(lo_, e))
        return jnp.concatenate(parts, axis=1)

    kk0 = k * kk_ref[...]
    kk = kk0 / jnp.maximum(jnp.sqrt(headsum(kk0 * kk0)), 1e-12)
    kmod = k * (1.0 + (a - 1.0) * ka_ref[...])
    bon_ref[...] = headsum(r * kmod * rk_ref[...]) * v

    tri = tri_ref[...]
    cums = []
    for c in range(tm // RW_CHUNK):
        hi, mid, lo_ = _split3(ld[c * RW_CHUNK:(c + 1) * RW_CHUNK])
        cums.append(_dot(tri, hi) + _dot(tri, mid) + _dot(tri, lo_))
    cum = jnp.concatenate(cums, axis=0)
    e_in = jnp.exp(cum)
    e_out = jnp.exp(-cum)
    at_ref[...] = -kk * jnp.exp(cum - ld)
    rt_ref[...] = r * e_in
    bt_ref[...] = kk * a * e_out
    kt_ref[...] = kmod * e_out
    v_ref[...] = v
    for c in range(tm // RW_CHUNK):
        pt_ref[c] = e_in[c * RW_CHUNK + RW_CHUNK - 1:(c + 1) * RW_CHUNK, :]


def _rwprep(feats, mu_a, mu_l, w0, a0, k_k, k_a, r_k, wup, aup, gup, tm=256):
    s = feats.shape[0]
    nb8 = tm // 8
    lora_blk = (2 * 3072 - RW_LORA_PAD) // RW_LORA_PAD
    e = (lax.broadcasted_iota(jnp.int32, (256, 256), 0) // RW_N
         == lax.broadcasted_iota(jnp.int32, (256, 256), 1) // RW_N).astype(BF16)
    tri = (lax.broadcasted_iota(jnp.int32, (RW_CHUNK, RW_CHUNK), 0)
           >= lax.broadcasted_iota(jnp.int32, (RW_CHUNK, RW_CHUNK), 1)).astype(BF16)
    row = lambda n: pl.BlockSpec((1, n), lambda i: (0, 0))
    full = lambda a, b: pl.BlockSpec((a, b), lambda i: (0, 0))
    dense_shape = jax.ShapeDtypeStruct((s, RW_WIDTH), F32)
    dense_spec = pl.BlockSpec((tm, RW_WIDTH), lambda i: (i, 0))
    ncb = tm // RW_CHUNK
    return pl.pallas_call(
        functools.partial(_rwprep_body, tm=tm),
        grid=(s // tm,),
        in_specs=[
            pl.BlockSpec((tm, 3072), lambda i: (i, 0)),
            pl.BlockSpec((8, 3072), lambda i: (jnp.maximum(i * nb8 - 1, 0), 0)),
            pl.BlockSpec((tm, RW_LORA_PAD), lambda i: (i, lora_blk)),
            pl.BlockSpec((8, RW_LORA_PAD), lambda i: (jnp.maximum(i * nb8 - 1, 0), lora_blk)),
            row(3072), row(RW_LORA_PAD), row(RW_WIDTH), row(RW_WIDTH), row(RW_WIDTH), row(RW_WIDTH),
            row(RW_WIDTH),
            full(RW_LORA_PAD, RW_WIDTH), full(RW_LORA_PAD, RW_WIDTH), full(RW_LORA_PAD, RW_WIDTH),
            full(256, 256), full(RW_CHUNK, RW_CHUNK),
        ],
        out_specs=[dense_spec] * 5 + [
            pl.BlockSpec((ncb, 1, RW_WIDTH), lambda i: (i, 0, 0)),
            dense_spec,
            dense_spec,
        ],
        out_shape=[dense_shape] * 5 + [
            jax.ShapeDtypeStruct((s // RW_CHUNK, 1, RW_WIDTH), F32),
            dense_shape,
            dense_shape,
        ],
        compiler_params=_cparams(("parallel",)),
        name="rwprep",
    )(feats, feats, feats, feats, mu_a, mu_l, w0, a0, k_k, k_a, r_k, wup, aup, gup, e, tri)


def _dot_tn(a, b):
    return lax.dot_general(a, b, (((0,), (0,)), ((), ())), preferred_element_type=F32)


def _rwscan_pairs_body(at_ref, rt_ref, bt_ref, kt_ref, v_ref, pt_ref, y_ref, s_ref, *, nc):
    @pl.when(pl.program_id(0) == 0)
    def _():
        s_ref[...] = jnp.zeros_like(s_ref)

    t = RW_CHUNK
    w = 2 * RW_N
    row = lax.broadcasted_iota(jnp.int32, (t, w), 0)
    col = lax.broadcasted_iota(jnp.int32, (t, w), 1)
    colh = col % RW_N
    strict = row > colh
    incl = row >= colh
    eye_pair = (row == colh).astype(F32)
    left = col < RW_N
    r2 = lax.broadcasted_iota(jnp.int32, (w, w), 0)
    c2 = lax.broadcasted_iota(jnp.int32, (w, w), 1)
    blk_f = ((r2 // RW_N) == (c2 // RW_N)).astype(F32)
    blk_b = blk_f.astype(BF16)
    eye_w = (r2 == c2).astype(F32)

    def bd(x):
        return jnp.concatenate([x, x], axis=0) * blk_b

    def run_step():
        npair = RW_HEADS // 2
        items = [(c, p) for c in range(nc) for p in range(npair)]
        pairs = range(len(items))
        sls = [(slice(c * t, (c + 1) * t), slice(p * w, (p + 1) * w)) for c, p in items]
        pts = [pt_ref[c][:, p * w:(p + 1) * w] for c, p in items]
        bf = lambda xs: [x.astype(BF16) for x in xs]
        a_b = bf([at_ref[sl] for sl in sls])
        r_f = [rt_ref[sl] for sl in sls]
        b_b = bf([bt_ref[sl] for sl in sls])
        k_b = bf([kt_ref[sl] for sl in sls])
        v_b = bf([v_ref[sl] for sl in sls])
        gg = [_dot_nt(jnp.concatenate([a_b[p], r_f[p].astype(BF16)], axis=0),
                      jnp.concatenate([bd(b_b[p]), bd(k_b[p])], axis=0)) for p in pairs]
        a_ab = [jnp.where(strict, g[:t, :w], 0.0) for g in gg]
        a_ak = bf([jnp.where(strict, g[:t, w:], 0.0) for g in gg])
        a_rb = bf([jnp.where(incl, g[t:, :w], 0.0) for g in gg])
        a_rk = bf([jnp.where(incl, g[t:, w:], 0.0) for g in gg])
        inv = [eye_pair + x for x in a_ab]
        pk = a_ab
        for _ in range(5):
            pkb = bf(pk)
            pk = [_dot(pkb[p], bd(pkb[p])) for p in pairs]
            pkb = bf(pk)
            inv = [inv[p] + _dot(inv[p].astype(BF16), bd(pkb[p])) for p in pairs]
        inv_b = bf(inv)
        w1 = bf([_dot(a_ak[p], bd(v_b[p])) for p in pairs])
        az = [_dot(inv_b[p], jnp.concatenate([bd(a_b[p]), bd(w1[p])], axis=1)) for p in pairs]
        atp = bf([x[:, :w] for x in az])
        z0 = bf([x[:, w:] for x in az])
        ry = [_dot(a_rb[p], jnp.concatenate([bd(atp[p]), bd(z0[p])], axis=1)) for p in pairs]
        rh = bf([r_f[p] + ry[p][:, :w] for p in pairs])
        y0 = [ry[p][:, w:] + _dot(a_rk[p], bd(v_b[p])) for p in pairs]
        m_bd = bf([(eye_w + _dot_tn(atp[p], b_b[p])) * blk_f * pts[p] for p in pairs])
        c_full = [(_dot_tn(z0[p], b_b[p]) + _dot_tn(v_b[p], k_b[p])) * blk_f * pts[p] for p in pairs]
        c_pair = [x[:RW_N] + x[RW_N:] for x in c_full]

        def headmean(x):
            s1 = jnp.sum(jnp.where(left, x, 0.0), axis=-1, keepdims=True)
            s2 = jnp.sum(jnp.where(left, 0.0, x), axis=-1, keepdims=True)
            return jnp.where(left, s1, s2) * (1.0 / RW_N)

        state = [s_ref[p] for p in range(npair)]
        for q in pairs:
            p = items[q][1]
            st_b = state[p].astype(BF16)
            y = _dot_nt(rh[q], bd(st_b)) + y0[q]
            state[p] = _dot(st_b, m_bd[q]) + c_pair[q]
            yc = y - headmean(y)
            y_ref[sls[q]] = yc * lax.rsqrt(headmean(yc * yc) + RW_GN_EPS)
        for p in range(npair):
            s_ref[p] = state[p]

    run_step()


def _rwscan(at, rt, bt, kt, v, pt, rows=512):
    s = at.shape[0]
    nc = rows // RW_CHUNK
    dense_spec = pl.BlockSpec((rows, RW_WIDTH), lambda i: (i, 0))
    return pl.pallas_call(
        functools.partial(_rwscan_pairs_body, nc=nc),
        grid=(s // rows,),
        in_specs=[dense_spec] * 5 + [pl.BlockSpec((nc, 1, RW_WIDTH), lambda i: (i, 0, 0))],
        out_specs=dense_spec,
        out_shape=jax.ShapeDtypeStruct((s, RW_WIDTH), F32),
        scratch_shapes=[pltpu.VMEM((RW_HEADS // 2, RW_N, 2 * RW_N), F32)],
        compiler_params=_cparams(("arbitrary",)),
        name="rwscan",
    )(at, rt, bt, kt, v, pt)


def _nsaprep_body(f_ref, cos_ref, sin_ref, qn_ref, qr_ref, kc_ref, vc_ref, ks_ref, vst_ref, kw_ref,
                  vwt_ref, gt_ref, *, tq):
    cs = cos_ref[...]
    sn = sin_ref[...]
    lane = lax.broadcasted_iota(jnp.int32, (tq, 128), 1)
    first8 = (lane % NSA_DH) < (ROPE_DIM // 2)

    def rope(x):
        swapped = jnp.where(first8, pltpu.roll(x, 128 - ROPE_DIM // 2, 1), pltpu.roll(x, ROPE_DIM // 2, 1))
        return x * cs + swapped * sn

    scale = NSA_DH ** -0.5 * LOG2E
    lane_hi = lane - NSA_DH
    blk_in_chunk = (lax.broadcasted_iota(jnp.int32, (tq, 128), 0) % SEL_CHUNK) // SEL_BLOCK
    onehot = ((lane_hi >= 0) & (lane_hi % BIAS_ROWS == 0) & (lane_hi // BIAS_ROWS == blk_in_chunk)).astype(F32)
    ones_rows = jnp.ones((V_ROWS - NSA_DH, tq), BF16)
    for p in range(NSA_HEADS // 2):
        x = f_ref[:, p * 128:(p + 1) * 128]
        xn_t = (x * scale).T
        xr_t = (rope(x) * scale).T
        for e in range(2):
            hd = 2 * p + e
            g, h = hd // NSA_HPG, hd % NSA_HPG
            qn_ref[g, 0, :, h * tq:(h + 1) * tq] = xn_t[e * NSA_DH:(e + 1) * NSA_DH].astype(BF16)
            qr_ref[g, 0, :, h * tq:(h + 1) * tq] = xr_t[e * NSA_DH:(e + 1) * NSA_DH].astype(BF16)

    def kv_piece(idx, p):
        base = NSA_HEADS * NSA_DH + idx * NSA_KV + p * 128
        return f_ref[:, base:base + 128]

    for p in range(2):
        kc = kv_piece(0, p)
        vc = kv_piece(1, p)
        ks = rope(kv_piece(2, p))
        vs_t = kv_piece(3, p).T
        kw = rope(kv_piece(4, p))
        vw_t = kv_piece(5, p).T
        for e in range(2):
            g = 2 * p + e
            sl = slice(e * NSA_DH, (e + 1) * NSA_DH)
            kc_ref[g] = kc[:, sl]
            vc_ref[g] = vc[:, sl]
            ks_low = ks if e == 0 else pltpu.roll(ks, NSA_DH, 1)
            ks_ref[g] = jnp.where(lane < NSA_DH, ks_low, onehot).astype(BF16)
            kw_low = kw if e == 0 else pltpu.roll(kw, NSA_DH, 1)
            kw_ref[g] = jnp.where(lane < NSA_DH, kw_low, jnp.where(lane == NSA_DH, 1.0, 0.0)).astype(BF16)
            vst_ref[g, :NSA_DH] = vs_t[sl].astype(BF16)
            vst_ref[g, NSA_DH:] = ones_rows
            vwt_ref[g, :NSA_DH] = vw_t[sl].astype(BF16)
            vwt_ref[g, NSA_DH:] = ones_rows
    gl = f_ref[:, 2560:2688]
    gt_ref[...] = jax.nn.sigmoid(gl).T[:3 * NSA_HEADS]


def _nsaprep(feats, cos_t, sin_t, tq):
    s = feats.shape[0]
    ni = s // tq
    kv_f32 = jax.ShapeDtypeStruct((NSA_G, s, NSA_DH), F32)
    kv_b16 = jax.ShapeDtypeStruct((NSA_G, s, NSA_DH), BF16)
    ksa_b16 = jax.ShapeDtypeStruct((NSA_G, s, 2 * NSA_DH), BF16)
    kvt_b16 = jax.ShapeDtypeStruct((NSA_G, V_ROWS, s), BF16)
    q_shape = jax.ShapeDtypeStruct((NSA_G, ni, NSA_DH, NSA_HPG * tq), BF16)
    q_spec = pl.BlockSpec((NSA_G, 1, NSA_DH, NSA_HPG * tq), lambda i: (0, i, 0, 0))
    kv_spec = pl.BlockSpec((NSA_G, tq, NSA_DH), lambda i: (0, i, 0))
    ksa_spec = pl.BlockSpec((NSA_G, tq, 2 * NSA_DH), lambda i: (0, i, 0))
    kvt_spec = pl.BlockSpec((NSA_G, V_ROWS, tq), lambda i: (0, 0, i))
    return pl.pallas_call(
        functools.partial(_nsaprep_body, tq=tq),
        grid=(ni,),
        in_specs=[
            pl.BlockSpec((tq, 3072), lambda i: (i, 1)),
            pl.BlockSpec((tq, 128), lambda i: (i, 0)),
            pl.BlockSpec((tq, 128), lambda i: (i, 0)),
        ],
        out_specs=[q_spec, q_spec, kv_spec, kv_spec, ksa_spec, kvt_spec, ksa_spec, kvt_spec,
                   pl.BlockSpec((3 * NSA_HEADS, tq), lambda i: (0, i))],
        out_shape=[q_shape, q_shape, kv_f32, kv_f32, ksa_b16, kvt_b16, ksa_b16, kvt_b16,
                   jax.ShapeDtypeStruct((3 * NSA_HEADS, s), F32)],
        compiler_params=_cparams(("parallel",)),
        name="nsaprep",
    )(feats, cos_t, sin_t)


def _gelu_tanh(x):
    return 0.5 * x * (1.0 + jnp.tanh(0.7978845608028654 * (x + 0.044715 * x * x * x)))


def _compress_body(x_ref, pe1_ref, pe2_ref, w1a_ref, w1b_ref, w2_ref, o_ref, *, ncp, transpose_out):
    x = x_ref[0]
    a = _dot((x + pe1_ref[...]).astype(BF16), w1a_ref[...])
    b = _dot((x + pe2_ref[...]).astype(BF16), w1b_ref[...])
    hid = a + pltpu.roll(b, ncp - 1, 0)
    act = _gelu_tanh(hid).astype(BF16)
    if transpose_out:
        o_ref[0, :NSA_DH] = _dot_nt(w2_ref[...], act).astype(BF16)
        o_ref[0, NSA_DH:] = jnp.ones((V_ROWS - NSA_DH, ncp), BF16)
    else:
        o_ref[0] = _dot(act, w2_ref[...]).astype(BF16)


def _compress(x, pe, w1, w2, transpose_out):
    g, s, dh = x.shape
    ncp = s // CMP_STRIDE
    half = CMP_STRIDE * dh
    xr = x.reshape(g, ncp, half)
    pe1 = pe[:CMP_STRIDE].reshape(1, half)
    pe2 = pe[CMP_STRIDE:].reshape(1, half)
    w1a = w1[:half].astype(BF16)
    w1b = w1[half:].astype(BF16)
    hid = w1.shape[1]
    w2k = (w2.T if transpose_out else w2).astype(BF16)
    full = lambda a, b: pl.BlockSpec((a, b), lambda i: (0, 0))
    if transpose_out:
        out_shape = jax.ShapeDtypeStruct((g, V_ROWS, ncp), BF16)
        out_spec = pl.BlockSpec((1, V_ROWS, ncp), lambda i: (i, 0, 0))
    else:
        out_shape = jax.ShapeDtypeStruct((g, ncp, dh), BF16)
        out_spec = pl.BlockSpec((1, ncp, dh), lambda i: (i, 0, 0))
    return pl.pallas_call(
        functools.partial(_compress_body, ncp=ncp, transpose_out=transpose_out),
        grid=(g,),
        in_specs=[
            pl.BlockSpec((1, ncp, half), lambda i: (i, 0, 0)),
            full(1, half), full(1, half), full(half, hid), full(half, hid), full(*w2k.shape),
        ],
        out_specs=out_spec,
        out_shape=out_shape,
        compiler_params=_cparams(("parallel",)),
        name="compress",
    )(xr, pe1, pe2, w1a, w1b, w2k)


def _rows_bf16(row, n):
    r16 = jnp.broadcast_to(row, (16, row.shape[1])).astype(BF16)
    return jnp.concatenate([r16] * (n // 16), axis=0)


def _flash_biased(sb, vt_blk, m_ref, acc_ref, h):
    m_old = m_ref[h]
    m_new = jnp.maximum(m_old, jnp.max(sb, axis=0, keepdims=True).astype(F32))
    alpha = jnp.exp2(m_old - m_new)
    p = jnp.exp2(sb - _rows_bf16(m_new, sb.shape[0]))
    acc_ref[h] = alpha * acc_ref[h] + _dot(vt_blk, p)
    m_ref[h] = m_new


def _nsa_body(qn_ref, qr_ref, kc_ref, vct_ref, ov_ref, ks_ref, vst_ref, kw_ref, vwt_ref, gt_ref, o_ref,
              bias_ref, qa_ref, qw_ref, st0_ref, st1_ref, st2_ref, st3_ref, m_ref, acc_ref, *, tq, ncp, nsel, topn):
    st_refs = (st0_ref, st1_ref, st2_ref, st3_ref)
    g = pl.program_id(0)
    i = pl.program_id(1)
    q0 = i * tq
    bpc = tq // SEL_BLOCK
    tpos = q0 + lax.broadcasted_iota(jnp.int32, (1, tq), 1)
    heads = range(NSA_HPG)
    hcols = lambda h: slice(h * tq, (h + 1) * tq)

    row_io = lax.broadcasted_iota(jnp.int32, (tq, 1), 0)
    col_io = lax.broadcasted_iota(jnp.int32, (1, tq), 1)
    as_bias = lambda keep: jnp.where(keep, 0.0, NEG).astype(BF16)
    causal_bias = as_bias(row_io <= col_io)
    anti_bias = as_bias(row_io > col_io)

    ncc = ncp // tq
    cmp_last = 2 * CMP_STRIDE - 1
    cbias = [as_bias((cc * tq + row_io) * CMP_STRIDE + cmp_last <= tpos) for cc in range(ncc)]
    has_cmp = jnp.where(tpos >= cmp_last, 1.0, 0.0)
    psum = None
    o_c = []
    def cmp_scores(h):
        qn_h = qn_ref[0, 0, :, hcols(h)]
        return [_dot(kc_ref[0, cc * tq:(cc + 1) * tq, :], qn_h).astype(BF16) + cbias[cc] for cc in range(ncc)]

    ahead = cmp_scores(0)
    for h in heads:
        sbs = ahead
        if h + 1 < NSA_HPG:
            ahead = cmp_scores(h + 1)
        mc = jnp.max(sbs[0], axis=0, keepdims=True)
        for sb in sbs[1:]:
            mc = jnp.maximum(mc, jnp.max(sb, axis=0, keepdims=True))
        mrows = _rows_bf16(mc.astype(F32), tq)
        pc = jnp.concatenate([jnp.exp2(sb - mrows) for sb in sbs], axis=0)
        rv = _dot(vct_ref[0], pc)
        inv = has_cmp / rv[NSA_DH:NSA_DH + 1]
        o_c.append(rv[:NSA_DH] * inv)
        pn = pc * _rows_bf16(inv, ncp)
        psum = pn if psum is None else psum + pn
    bias_ref[...] = _dot(ov_ref[...], psum)

    def select_blocks(rows):
        jj = lax.broadcasted_iota(jnp.int32, (rows, tq), 0)
        jf = jj.astype(F32)
        cur = tpos // SEL_BLOCK
        forced = (jj == 0) | (jj == cur) | (jj == cur - 1)
        score = jnp.where(forced, -3e38, jnp.where(jj <= cur, bias_ref[:rows], -1.0))
        sel = jnp.where(forced, 1.0, 0.0)
        for _ in range(topn - 3):
            mx = jnp.max(score, axis=0, keepdims=True)
            first = jnp.min(jnp.where(score == mx, jf, float(rows)), axis=0, keepdims=True)
            hit = jf == first
            sel = jnp.where(hit, 1.0, sel)
            score = jnp.where(hit, -3e38, score)
        bias_ref[:rows] = jnp.where(sel > 0.0, 0.0, NEG)

    nvar = 4 if nsel % 4 == 0 and nsel // 4 >= 2 * topn else 1
    live_blocks = (i + 1) * bpc
    for v in range(nvar):
        rows = nsel * (v + 1) // nvar
        lo = nsel * v // nvar

        @pl.when((live_blocks > lo) & (live_blocks <= rows))
        def _(rows=rows):
            select_blocks(rows)

    def reset():
        m_ref[...] = jnp.full_like(m_ref, NEG)
        acc_ref[...] = jnp.zeros_like(acc_ref)

    def finish(h):
        acc = acc_ref[h]
        l = acc[NSA_DH:NSA_DH + 1]
        return acc[:NSA_DH] * jnp.where(l > 0.0, 1.0 / l, 0.0)

    row16 = lax.broadcasted_iota(jnp.int32, (BIAS_ROWS, NSA_HPG * tq), 0)

    qa_ref[:NSA_DH] = qr_ref[0, 0]

    def set_bias_rows(c):
        for b in range(bpc):
            brow = bias_ref[pl.ds(c * bpc + b, 1), :]
            brow4 = jnp.concatenate([brow] * NSA_HPG, axis=1)
            lo_row = NSA_DH + BIAS_ROWS * b
            qa_ref[lo_row:lo_row + BIAS_ROWS] = jnp.where(row16 == 0, brow4, 0.0).astype(BF16)

    def sel_scores(c, h):
        kblk = ks_ref[0, pl.ds(pl.multiple_of(c * tq, tq), tq), :]
        return _dot(kblk, qa_ref[:, hcols(h)]).astype(BF16)

    reset()
    set_bias_rows(0)
    for h in heads:
        st_refs[h][...] = sel_scores(0, h)

    def sel_chunk(c):
        set_bias_rows(c + 1)
        vblk = vst_ref[0, :, pl.ds(pl.multiple_of(c * tq, tq), tq)]
        nxt = sel_scores(c + 1, 0)
        for h in heads:
            cur_scores = st_refs[h][...]
            after = sel_scores(c + 1, h + 1) if h + 1 < NSA_HPG else None
            _flash_biased(cur_scores, vblk, m_ref, acc_ref, h)
            st_refs[h][...] = nxt
            nxt = after

    def sel_group(cg, carry):
        for u in range(SEL_UNROLL):
            sel_chunk(SEL_UNROLL * cg + u)
        return carry

    lax.fori_loop(0, i // SEL_UNROLL, sel_group, 0)
    done = (i // SEL_UNROLL) * SEL_UNROLL
    part = SEL_UNROLL // 2
    while part >= 1:
        @pl.when((i & part) != 0)
        def _(base=done, part=part):
            for u in range(part):
                sel_chunk(base + u)
        done = done + (i & part)
        part //= 2

    vblk = vst_ref[0, :, pl.ds(pl.multiple_of(q0, tq), tq)]
    for h in heads:
        _flash_biased(st_refs[h][...] + causal_bias, vblk, m_ref, acc_ref, h)
    o_s = [finish(h) for h in heads]

    reset()
    nback = WINDOW // tq
    win_bias = [causal_bias] + [None] * (nback - 1) + [anti_bias]
    qw_ref[:NSA_DH] = qr_ref[0, 0]
    qw_ref[NSA_DH + BIAS_ROWS:] = jnp.zeros((NSA_DH - BIAS_ROWS, NSA_HPG * tq), BF16)

    def stash_window(j, h):
        exists = jnp.where(i >= j, 0.0, NEG)
        if h == 0:
            qw_ref[NSA_DH:NSA_DH + BIAS_ROWS] = jnp.where(row16 == 0, exists, 0.0).astype(BF16)
        kblk = kw_ref[0, pl.ds(pl.multiple_of(jnp.maximum(i - j, 0) * tq, tq), tq), :]
        sb = _dot(kblk, qw_ref[:, hcols(h)]).astype(BF16)
        st_refs[h][...] = sb if win_bias[j] is None else sb + win_bias[j]

    for h in heads:
        stash_window(0, h)
    for j in range(nback + 1):
        vblk = vwt_ref[0, :, pl.ds(pl.multiple_of(jnp.maximum(i - j, 0) * tq, tq), tq)]
        for h in heads:
            _flash_biased(st_refs[h][...], vblk, m_ref, acc_ref, h)
            if j < nback:
                stash_window(j + 1, h)
    o_w = [finish(h) for h in heads]

    def gate(branch, h):
        return gt_ref[pl.ds(branch * NSA_HEADS + g * NSA_HPG + h, 1), :]

    out_t = [gate(0, h) * o_c[h] + gate(1, h) * o_s[h] + gate(2, h) * o_w[h] for h in heads]
    halves = [jnp.concatenate(out_t[2 * p:2 * p + 2], axis=0).T for p in range(NSA_HPG // 2)]
    o_ref[...] = jnp.concatenate(halves, axis=1)


def _nsa(qn, qr, kc, vct, ov, ks, vst, kw, vwt, gt, tq):
    g, ni, dh, w4 = qn.shape
    s = ks.shape[1]
    ncp = kc.shape[1]
    nsel = s // SEL_BLOCK
    topn = min(SEL_TOPK, nsel)
    assert tq == SEL_CHUNK and (tq // SEL_BLOCK) * BIAS_ROWS == dh and w4 == NSA_HPG * tq
    q_spec = pl.BlockSpec((1, 1, dh, w4), lambda a, b: (a, b, 0, 0))
    return pl.pallas_call(
        functools.partial(_nsa_body, tq=tq, ncp=ncp, nsel=nsel, topn=topn),
        grid=(g, ni),
        in_specs=[
            q_spec, q_spec,
            pl.BlockSpec((1, ncp, dh), lambda a, b: (a, 0, 0)),
            pl.BlockSpec((1, V_ROWS, ncp), lambda a, b: (a, 0, 0)),
            pl.BlockSpec((nsel, ncp), lambda a, b: (0, 0)),
            pl.BlockSpec((1, s, 2 * dh), lambda a, b: (a, 0, 0)),
            pl.BlockSpec((1, V_ROWS, s), lambda a, b: (a, 0, 0)),
            pl.BlockSpec((1, s, 2 * dh), lambda a, b: (a, 0, 0)),
            pl.BlockSpec((1, V_ROWS, s), lambda a, b: (a, 0, 0)),
            pl.BlockSpec((3 * NSA_HEADS, tq), lambda a, b: (0, b)),
        ],
        out_specs=pl.BlockSpec((tq, NSA_HPG * dh), lambda a, b: (b, a)),
        out_shape=jax.ShapeDtypeStruct((s, NSA_HEADS * dh), F32),
        scratch_shapes=[
            pltpu.VMEM((nsel, tq), F32),
            pltpu.VMEM((2 * dh, w4), BF16),
            pltpu.VMEM((2 * dh, w4), BF16),
            pltpu.VMEM((tq, tq), BF16), pltpu.VMEM((tq, tq), BF16),
            pltpu.VMEM((tq, tq), BF16), pltpu.VMEM((tq, tq), BF16),
            pltpu.VMEM((NSA_HPG, 1, tq), F32),
            pltpu.VMEM((NSA_HPG, V_ROWS, tq), F32),
        ],
        compiler_params=_cparams(("arbitrary", "arbitrary")),
        name="nsa",
    )(qn, qr, kc, vct, ov, ks, vst, kw, vwt, gt)


def _memkv_body(mem_ref, g_ref, w_ref, k_ref, v_ref):
    kv = _dot(_rms(mem_ref[...], g_ref[...]).astype(BF16), w_ref[...])
    width = MEM_HEADS * MEM_DH
    k_ref[...] = kv[:, :width].astype(BF16)
    v_ref[...] = kv[:, width:].astype(BF16)


def _memkv(mem, g, w):
    m, d = mem.shape
    width = MEM_HEADS * MEM_DH
    shp = jax.ShapeDtypeStruct((m, width), BF16)
    return pl.pallas_call(
        _memkv_body,
        out_shape=[shp, shp],
        compiler_params=pltpu.CompilerParams(vmem_limit_bytes=VMEM_LIMIT),
        name="memkv",
    )(mem, g, w)


def _mixout_body(yn_ref, bon_ref, gate_ref, ynsa_ref, h_ref, lnw_ref, lnb_ref, wo_rw_ref, wo_nsa_ref,
                 gpost_ref, mpre_ref, wq_ref, k_ref, v_ref, wom_ref, mpost_ref, o_ref):
    y_rw = ((yn_ref[...] * lnw_ref[...] + lnb_ref[...]) + bon_ref[...]) * gate_ref[...]
    y = _dot(y_rw.astype(BF16), wo_rw_ref[...]) + _dot(ynsa_ref[...].astype(BF16), wo_nsa_ref[...])
    h2 = h_ref[...] + _rms(y, gpost_ref[...])
    q = _dot(_rms(h2, mpre_ref[...]).astype(BF16), wq_ref[...])
    scale = MEM_DH ** -0.5
    outs = []
    for hh in range(MEM_HEADS):
        sl = slice(hh * MEM_DH, (hh + 1) * MEM_DH)
        s = _dot_nt(q[:, sl].astype(BF16), k_ref[:, sl]) * scale
        p = jnp.exp(s - jnp.max(s, axis=-1, keepdims=True))
        p = p / jnp.sum(p, axis=-1, keepdims=True)
        outs.append(_dot(p.astype(BF16), v_ref[:, sl]))
    o = jnp.concatenate(outs, axis=-1).astype(BF16)
    m = _dot(o, wom_ref[...])
    o_ref[...] = h2 + _rms(m, mpost_ref[...])


def _mixout(yn, bon, gate, ynsa, h, lnw, lnb, wo_rw, wo_nsa, gpost, mpre, wq, k, v, wom, mpost, tm=256):
    s, d = h.shape
    rows = lambda n: pl.BlockSpec((tm, n), lambda i: (i, 0))
    full = lambda a: pl.BlockSpec(a.shape, lambda i: (0, 0))
    return pl.pallas_call(
        _mixout_body,
        grid=(s // tm,),
        in_specs=[rows(RW_WIDTH), rows(RW_WIDTH), rows(RW_WIDTH), rows(RW_WIDTH), rows(d),
                  full(lnw), full(lnb), full(wo_rw), full(wo_nsa), full(gpost), full(mpre), full(wq),
                  full(k), full(v), full(wom), full(mpost)],
        out_specs=rows(d),
        out_shape=jax.ShapeDtypeStruct((s, d), F32),
        compiler_params=_cparams(("parallel",)),
        name="mixout",
    )(yn, bon, gate, ynsa, h, lnw, lnb, wo_rw, wo_nsa, gpost, mpre, wq, k, v, wom, mpost)


def _rope_tables(s):
    half = ROPE_DIM // 2
    inv_freq = ROPE_THETA ** (-jnp.arange(half, dtype=F32) * 2.0 / ROPE_DIM)
    ang = jnp.arange(s, dtype=jnp.int32).astype(F32)[:, None] * inv_freq[None, :]
    cos, sin = jnp.cos(ang), jnp.sin(ang)
    ones = jnp.ones((s, NSA_DH - ROPE_DIM), F32)
    zeros = jnp.zeros((s, NSA_DH - ROPE_DIM), F32)
    cos_h = jnp.concatenate([cos, cos, ones], axis=1)
    sin_h = jnp.concatenate([-sin, sin, zeros], axis=1)
    return jnp.tile(cos_h, (1, 2)), jnp.tile(sin_h, (1, 2))


def _overlap_t(s):
    ncp = s // CMP_STRIDE
    nsel = s // SEL_BLOCK
    cmp_start = jnp.arange(ncp)[None, :] * CMP_STRIDE
    sel_start = jnp.arange(nsel)[:, None] * SEL_BLOCK
    ov = (cmp_start < sel_start + SEL_BLOCK) & (cmp_start + 2 * CMP_STRIDE - 1 >= sel_start)
    ov = ov & (jnp.arange(ncp)[None, :] < ncp - 1)
    return ov.astype(BF16)


def kernel(x, mem, ffn1_pre_g, ffn1_w_gate, ffn1_w_up, ffn1_w_down, ffn1_post_g, mix_pre_g, w_in, rw_mu, rw_w0, rw_w_up, rw_a0, rw_a_up, rw_g_up, rw_k_k, rw_k_a, rw_r_k, rw_ln_w, rw_ln_b, cmp_pe_k, cmp_w1_k, cmp_w2_k, cmp_pe_v, cmp_w1_v, cmp_w2_v, w_out, mix_post_g, mem_pre_g, mem_norm_g, mem_w_q, mem_w_kv, mem_w_o, mem_post_g, ffn2_pre_g, ffn2_w_gate, ffn2_w_up, ffn2_w_down, ffn2_post_g):
    b, s, d = x.shape
    tq = SEL_CHUNK
    assert b == 1 and d == D_MODEL and s % (CMP_STRIDE * tq) == 0
    row = lambda v: v.reshape(1, -1).astype(F32)
    pad_ff = D_FF_PAD - D_FF

    def ffn_weights(wg, wu, wd):
        return (jnp.pad(wg.astype(BF16), ((0, 0), (0, pad_ff))),
                jnp.pad(wu.astype(BF16), ((0, 0), (0, pad_ff))),
                jnp.pad(wd.astype(BF16), ((0, pad_ff), (0, 0))))

    h = x[0]
    h = _ffn(h, row(ffn1_pre_g), *ffn_weights(ffn1_w_gate, ffn1_w_up, ffn1_w_down), row(ffn1_post_g))

    rw_cols = 3 * RW_WIDTH + RW_LORA
    nsa_main = NSA_HEADS * NSA_DH + 6 * NSA_KV
    zc = lambda n: jnp.zeros((d, n), BF16)
    w_in_b = w_in.astype(BF16)
    w_in_r = jnp.concatenate([
        w_in_b[:, :3 * RW_WIDTH],
        w_in_b[:, rw_cols:rw_cols + nsa_main],
        w_in_b[:, rw_cols + nsa_main:], zc(128 - 3 * NSA_HEADS),
        w_in_b[:, 3 * RW_WIDTH:rw_cols], zc(RW_LORA_PAD - RW_LORA),
    ], axis=1)
    feats = _inproj(h, row(mix_pre_g), w_in_r)

    mu_a = row(rw_mu[:3 * RW_WIDTH])
    mu_l = row(jnp.pad(rw_mu[3 * RW_WIDTH:], (0, RW_LORA_PAD - RW_LORA)))
    lora_w = lambda w, off: jnp.pad(w, ((off, RW_LORA_PAD - off - w.shape[0]), (0, 0))).astype(BF16)
    at, rt, bt, kt, v, pt, gate, bonus = _rwprep(
        feats, mu_a, mu_l, row(rw_w0), row(rw_a0), row(rw_k_k), row(rw_k_a), row(rw_r_k),
        lora_w(rw_w_up, 0), lora_w(rw_a_up, 64), lora_w(rw_g_up, 128))
    yn = _rwscan(at, rt, bt, kt, v, pt)

    cos_t, sin_t = _rope_tables(s)
    qn, qr, kc, vc, ks, vst, kw, vwt, gt = _nsaprep(feats, cos_t, sin_t, tq)
    k_cmp = _compress(kc, cmp_pe_k, cmp_w1_k, cmp_w2_k, transpose_out=False)
    v_cmp_t = _compress(vc, cmp_pe_v, cmp_w1_v, cmp_w2_v, transpose_out=True)
    y_nsa = _nsa(qn, qr, k_cmp, v_cmp_t, _overlap_t(s), ks, vst, kw, vwt, gt, tq)

    mem_k, mem_v = _memkv(mem[0], row(mem_norm_g), mem_w_kv.astype(BF16))
    h = _mixout(yn, bonus, gate, y_nsa, h, row(rw_ln_w), row(rw_ln_b),
                w_out[:RW_WIDTH].astype(BF16), w_out[RW_WIDTH:].astype(BF16), row(mix_post_g),
                row(mem_pre_g), mem_w_q.astype(BF16), mem_k, mem_v, mem_w_o.astype(BF16), row(mem_post_g))

    h = _ffn(h, row(ffn2_pre_g), *ffn_weights(ffn2_w_gate, ffn2_w_up, ffn2_w_down), row(ffn2_post_g))
    return h[None]
```

```python
import functools

import jax
import jax.numpy as jnp
from jax import lax
from jax.experimental import pallas as pl
from jax.experimental.pallas import tpu as pltpu

F32 = jnp.float32
BF16 = jnp.bfloat16

D_MODEL = 2048
EPS = 1e-6

RW_HEADS = 16
RW_N = 64
RW_WIDTH = 1024
RW_LORA = 288
RW_LORA_PAD = 384
RW_GN_EPS = 64e-5
RW_CHUNK = 64

NSA_HEADS = 16
NSA_G = 4
NSA_HPG = 4
NSA_DH = 64
NSA_KV = 256
CMP_STRIDE = 16
SEL_BLOCK = 64
SEL_TOPK = 16
WINDOW = 512
FORCE_BONUS = 1000.0
ROPE_THETA = 500000.0
ROPE_DIM = 16

MEM_HEADS = 4
MEM_DH = 128

LOG2E = 1.4426950408889634
SEL_CHUNK = 256
SEL_UNROLL = 4
BIAS_ROWS = 16
V_ROWS = NSA_DH + 16

NEG = -1e30
VMEM_LIMIT = 56 * 1024 * 1024


def _cparams(sem):
    return pltpu.CompilerParams(dimension_semantics=sem, vmem_limit_bytes=VMEM_LIMIT)


def _rms(x, g):
    return x * lax.rsqrt(jnp.mean(x * x, axis=-1, keepdims=True) + EPS) * g


def _dot(a, b):
    return jnp.dot(a, b, preferred_element_type=F32)


def _dot_nt(a, b):
    return lax.dot_general(a, b, (((1,), (1,)), ((), ())), preferred_element_type=F32)


def _split2(x):
    hi = x.astype(BF16)
    lo = (x - hi.astype(F32)).astype(BF16)
    return hi, lo


def _split3(x):
    hi = x.astype(BF16)
    r1 = x - hi.astype(F32)
    mid = r1.astype(BF16)
    lo = (r1 - mid.astype(F32)).astype(BF16)
    return hi, mid, lo


def _ffn_body(h_ref, gpre_ref, wg_ref, wu_ref, wd_ref, wgt_ref, wut_ref, wdt_ref, gpost_ref, o_ref, xn_ref,
              *, nj):
    j = pl.program_id(1)

    @pl.when(j == 0)
    def _():
        xn_ref[...] = _rms(h_ref[...], gpre_ref[...]).astype(BF16)
        o_ref[...] = jnp.zeros_like(o_ref)

    xn = xn_ref[...]

    def swiglu_part(wg, wu, wd):
        g = _dot(xn, wg)
        u = _dot(xn, wu)
        return _dot((g * jax.nn.sigmoid(g) * u).astype(BF16), wd)

    o_ref[...] += swiglu_part(wg_ref[...], wu_ref[...], wd_ref[...])

    @pl.when(j == nj - 1)
    def _():
        y = o_ref[...] + swiglu_part(wgt_ref[...], wut_ref[...], wdt_ref[...])
        o_ref[...] = h_ref[...] + 0.5 * _rms(y, gpost_ref[...])


def _ffn(h, pre_g, wg, wu, wd, post_g, tm=512, tf=512):
    s, d = h.shape
    ff = wg.shape[1]
    nj = ff // tf
    tail = ff - nj * tf
    assert 0 < tail and tail % 128 == 0
    wgt, wut, wdt = wg[:, nj * tf:], wu[:, nj * tf:], wd[nj * tf:]
    const = lambda a: pl.BlockSpec(a.shape, lambda i, j: (0, 0))
    return pl.pallas_call(
        functools.partial(_ffn_body, nj=nj),
        grid=(s // tm, nj),
        in_specs=[
            pl.BlockSpec((tm, d), lambda i, j: (i, 0)),
            const(pre_g),
            pl.BlockSpec((d, tf), lambda i, j: (0, j)),
            pl.BlockSpec((d, tf), lambda i, j: (0, j)),
            pl.BlockSpec((tf, d), lambda i, j: (j, 0)),
            const(wgt), const(wut), const(wdt),
            const(post_g),
        ],
        out_specs=pl.BlockSpec((tm, d), lambda i, j: (i, 0)),
        out_shape=jax.ShapeDtypeStruct((s, d), F32),
        scratch_shapes=[pltpu.VMEM((tm, d), BF16)],
        compiler_params=_cparams(("parallel", "arbitrary")),
        name="ffn",
    )(h, pre_g, wg, wu, wd, wgt, wut, wdt, post_g)


def _inproj_body(h_ref, g_ref, w_ref, o_ref, xn_ref):
    @pl.when(pl.program_id(1) == 0)
    def _():
        xn_ref[...] = _rms(h_ref[...], g_ref[...]).astype(BF16)

    o_ref[...] = _dot(xn_ref[...], w_ref[...])


def _inproj(h, g, w, tm=512, tn=2048):
    s, d = h.shape
    n = w.shape[1]
    return pl.pallas_call(
        _inproj_body,
        grid=(s // tm, n // tn),
        in_specs=[
            pl.BlockSpec((tm, d), lambda i, j: (i, 0)),
            pl.BlockSpec((1, d), lambda i, j: (0, 0)),
            pl.BlockSpec((d, tn), lambda i, j: (0, j)),
        ],
        out_specs=pl.BlockSpec((tm, tn), lambda i, j: (i, j)),
        out_shape=jax.ShapeDtypeStruct((s, n), F32),
        scratch_shapes=[pltpu.VMEM((tm, d), BF16)],
        compiler_params=_cparams(("parallel", "arbitrary")),
        name="inproj",
    )(h, g, w)


def _rwprep_body(f_ref, fp_ref, lo_ref, lop_ref, mua_ref, mul_ref, w0_ref, a0_ref, kk_ref, ka_ref,
                 rk_ref, wup_ref, aup_ref, gup_ref, e_ref, tri_ref,
                 at_ref, rt_ref, bt_ref, kt_ref, v_ref, pt_ref, g_ref, bon_ref, *, tm):
    first = pl.program_id(0) == 0

    def shifted(x, prev_blk):
        prev_last = jnp.where(first, 0.0, prev_blk[7:8, :])
        rolled = pltpu.roll(x, 1, 0)
        row = lax.broadcasted_iota(jnp.int32, x.shape, 0)
        return jnp.where(row == 0, prev_last, rolled)

    f = f_ref[...]
    fs = f + mua_ref[...] * (shifted(f, fp_ref[...]) - f)
    lo = lo_ref[...]
    los = lo + mul_ref[...] * (shifted(lo, lop_ref[...]) - lo)
    r = fs[:, :RW_WIDTH]
    k = fs[:, RW_WIDTH:2 * RW_WIDTH]
    v = fs[:, 2 * RW_WIDTH:]

    wx = _dot(jnp.tanh(los).astype(BF16), wup_ref[...])
    ax = _dot(los.astype(BF16), aup_ref[...])
    g_ref[...] = _dot(jax.nn.sigmoid(los).astype(BF16), gup_ref[...])

    z = -(w0_ref[...] + wx)
    softplus = jnp.maximum(z, 0.0) + jnp.log(1.0 + jnp.exp(-jnp.abs(z)))
    ld = -jnp.exp(-softplus - 0.5)
    a = jax.nn.sigmoid(a0_ref[...] + ax)

    e = e_ref[...]

    def headsum(x):
        parts = []
        for c in range(RW_WIDTH // 256):
            hi, lo_ = _split2(x[:, c * 256:(c + 1) * 256])
            parts.append(_dot(hi, e) + _dot(lo_, e))
        return jnp.concatenate(parts, axis=1)

    kk0 = k * kk_ref[...]
    kk = kk0 / jnp.maximum(jnp.sqrt(headsum(kk0 * kk0)), 1e-12)
    kmod = k * (1.0 + (a - 1.0) * ka_ref[...])
    bon_ref[...] = headsum(r * kmod * rk_ref[...]) * v

    tri = tri_ref[...]
    cums = []
    for c in range(tm // RW_CHUNK):
        hi, mid, lo_ = _split3(ld[c * RW_CHUNK:(c + 1) * RW_CHUNK])
        cums.append(_dot(tri, hi) + _dot(tri, mid) + _dot(tri, lo_))
    cum = jnp.concatenate(cums, axis=0)
    e_in = jnp.exp(cum)
    e_out = jnp.exp(-cum)
    at_ref[...] = (-kk * jnp.exp(cum - ld)).astype(BF16)
    rt_ref[...] = r * e_in
    bt_ref[...] = (kk * a * e_out).astype(BF16)
    kt_ref[...] = (kmod * e_out).astype(BF16)
    v_ref[...] = v.astype(BF16)
    for c in range(tm // RW_CHUNK):
        pt_ref[c] = e_in[c * RW_CHUNK + RW_CHUNK - 1:(c + 1) * RW_CHUNK, :]


def _rwprep(feats, mu_a, mu_l, w0, a0, k_k, k_a, r_k, wup, aup, gup, tm=256):
    s = feats.shape[0]
    nb8 = tm // 8
    lora_blk = (2 * 3072 - RW_LORA_PAD) // RW_LORA_PAD
    e = (lax.broadcasted_iota(jnp.int32, (256, 256), 0) // RW_N
         == lax.broadcasted_iota(jnp.int32, (256, 256), 1) // RW_N).astype(BF16)
    tri = (lax.broadcasted_iota(jnp.int32, (RW_CHUNK, RW_CHUNK), 0)
           >= lax.broadcasted_iota(jnp.int32, (RW_CHUNK, RW_CHUNK), 1)).astype(BF16)
    row = lambda n: pl.BlockSpec((1, n), lambda i: (0, 0))
    full = lambda a, b: pl.BlockSpec((a, b), lambda i: (0, 0))
    dense_shape = jax.ShapeDtypeStruct((s, RW_WIDTH), F32)
    dense_b16 = jax.ShapeDtypeStruct((s, RW_WIDTH), BF16)
    dense_spec = pl.BlockSpec((tm, RW_WIDTH), lambda i: (i, 0))
    ncb = tm // RW_CHUNK
    return pl.pallas_call(
        functools.partial(_rwprep_body, tm=tm),
        grid=(s // tm,),
        in_specs=[
            pl.BlockSpec((tm, 3072), lambda i: (i, 0)),
            pl.BlockSpec((8, 3072), lambda i: (jnp.maximum(i * nb8 - 1, 0), 0)),
            pl.BlockSpec((tm, RW_LORA_PAD), lambda i: (i, lora_blk)),
            pl.BlockSpec((8, RW_LORA_PAD), lambda i: (jnp.maximum(i * nb8 - 1, 0), lora_blk)),
            row(3072), row(RW_LORA_PAD), row(RW_WIDTH), row(RW_WIDTH), row(RW_WIDTH), row(RW_WIDTH),
            row(RW_WIDTH),
            full(RW_LORA_PAD, RW_WIDTH), full(RW_LORA_PAD, RW_WIDTH), full(RW_LORA_PAD, RW_WIDTH),
            full(256, 256), full(RW_CHUNK, RW_CHUNK),
        ],
        out_specs=[dense_spec] * 5 + [
            pl.BlockSpec((ncb, 1, RW_WIDTH), lambda i: (i, 0, 0)),
            dense_spec,
            dense_spec,
        ],
        out_shape=[dense_b16, dense_shape, dense_b16, dense_b16, dense_b16] + [
            jax.ShapeDtypeStruct((s // RW_CHUNK, 1, RW_WIDTH), F32),
            dense_shape,
            dense_shape,
        ],
        compiler_params=_cparams(("parallel",)),
        name="rwprep",
    )(feats, feats, feats, feats, mu_a, mu_l, w0, a0, k_k, k_a, r_k, wup, aup, gup, e, tri)


def _dot_tn(a, b):
    return lax.dot_general(a, b, (((0,), (0,)), ((), ())), preferred_element_type=F32)


def _rwscan_pairs_body(at_ref, rt_ref, bt_ref, kt_ref, v_ref, pt_ref, y_ref, s_ref, *, nc):
    @pl.when(pl.program_id(0) == 0)
    def _():
        s_ref[...] = jnp.zeros_like(s_ref)

    t = RW_CHUNK
    w = 2 * RW_N
    row = lax.broadcasted_iota(jnp.int32, (t, w), 0)
    col = lax.broadcasted_iota(jnp.int32, (t, w), 1)
    colh = col % RW_N
    strict = row > colh
    incl = row >= colh
    eye_pair = (row == colh).astype(F32)
    left = col < RW_N
    r2 = lax.broadcasted_iota(jnp.int32, (w, w), 0)
    c2 = lax.broadcasted_iota(jnp.int32, (w, w), 1)
    blk_f = ((r2 // RW_N) == (c2 // RW_N)).astype(F32)
    blk_b = blk_f.astype(BF16)
    eye_w = (r2 == c2).astype(F32)

    def bd(x):
        return jnp.concatenate([x, x], axis=0) * blk_b

    def run_step():
        npair = RW_HEADS // 2
        items = [(c, p) for c in range(nc) for p in range(npair)]
        pairs = range(len(items))
        sls = [(slice(c * t, (c + 1) * t), slice(p * w, (p + 1) * w)) for c, p in items]
        pts = [pt_ref[c][:, p * w:(p + 1) * w] for c, p in items]
        bf = lambda xs: [x.astype(BF16) for x in xs]
        a_b = [at_ref[sl] for sl in sls]
        r_f = [rt_ref[sl] for sl in sls]
        b_b = [bt_ref[sl] for sl in sls]
        k_b = [kt_ref[sl] for sl in sls]
        v_b = [v_ref[sl] for sl in sls]
        gg = [_dot_nt(jnp.concatenate([a_b[p], r_f[p].astype(BF16)], axis=0),
                      jnp.concatenate([bd(b_b[p]), bd(k_b[p])], axis=0)) for p in pairs]
        a_ab = [jnp.where(strict, g[:t, :w], 0.0) for g in gg]
        a_ak = bf([jnp.where(strict, g[:t, w:], 0.0) for g in gg])
        a_rb = bf([jnp.where(incl, g[t:, :w], 0.0) for g in gg])
        a_rk = bf([jnp.where(incl, g[t:, w:], 0.0) for g in gg])
        inv = [eye_pair + x for x in a_ab]
        pk = a_ab
        for _ in range(5):
            pkb = bf(pk)
            pk = [_dot(pkb[p], bd(pkb[p])) for p in pairs]
            pkb = bf(pk)
            inv = [inv[p] + _dot(inv[p].astype(BF16), bd(pkb[p])) for p in pairs]
        inv_b = bf(inv)
        w1 = bf([_dot(a_ak[p], bd(v_b[p])) for p in pairs])
        az = [_dot(inv_b[p], jnp.concatenate([bd(a_b[p]), bd(w1[p])], axis=1)) for p in pairs]
        atp = bf([x[:, :w] for x in az])
        z0 = bf([x[:, w:] for x in az])
        ry = [_dot(a_rb[p], jnp.concatenate([bd(atp[p]), bd(z0[p])], axis=1)) for p in pairs]
        rh = bf([r_f[p] + ry[p][:, :w] for p in pairs])
        y0 = [ry[p][:, w:] + _dot(a_rk[p], bd(v_b[p])) for p in pairs]
        m_bd = bf([(eye_w + _dot_tn(atp[p], b_b[p])) * blk_f * pts[p] for p in pairs])
        c_full = [(_dot_tn(z0[p], b_b[p]) + _dot_tn(v_b[p], k_b[p])) * blk_f * pts[p] for p in pairs]
        c_pair = [x[:RW_N] + x[RW_N:] for x in c_full]

        def headmean(x):
            s1 = jnp.sum(jnp.where(left, x, 0.0), axis=-1, keepdims=True)
            s2 = jnp.sum(jnp.where(left, 0.0, x), axis=-1, keepdims=True)
            return jnp.where(left, s1, s2) * (1.0 / RW_N)

        state = [s_ref[p] for p in range(npair)]
        for q in pairs:
            p = items[q][1]
            st_b = state[p].astype(BF16)
            y = _dot_nt(rh[q], bd(st_b)) + y0[q]
            state[p] = _dot(st_b, m_bd[q]) + c_pair[q]
            yc = y - headmean(y)
            y_ref[sls[q]] = yc * lax.rsqrt(headmean(yc * yc) + RW_GN_EPS)
        for p in range(npair):
            s_ref[p] = state[p]

    run_step()


def _rwscan(at, rt, bt, kt, v, pt, rows=512):
    s = at.shape[0]
    nc = rows // RW_CHUNK
    dense_spec = pl.BlockSpec((rows, RW_WIDTH), lambda i: (i, 0))
    return pl.pallas_call(
        functools.partial(_rwscan_pairs_body, nc=nc),
        grid=(s // rows,),
        in_specs=[dense_spec] * 5 + [pl.BlockSpec((nc, 1, RW_WIDTH), lambda i: (i, 0, 0))],
        out_specs=dense_spec,
        out_shape=jax.ShapeDtypeStruct((s, RW_WIDTH), F32),
        scratch_shapes=[pltpu.VMEM((RW_HEADS // 2, RW_N, 2 * RW_N), F32)],
        compiler_params=_cparams(("arbitrary",)),
        name="rwscan",
    )(at, rt, bt, kt, v, pt)


def _nsaprep_body(f_ref, cos_ref, sin_ref, qn_ref, qr_ref, kc_ref, vc_ref, ks_ref, vst_ref, kw_ref,
                  vwt_ref, gt_ref, *, tq):
    cs = cos_ref[...]
    sn = sin_ref[...]
    lane = lax.broadcasted_iota(jnp.int32, (tq, 128), 1)
    first8 = (lane % NSA_DH) < (ROPE_DIM // 2)

    def rope(x):
        swapped = jnp.where(first8, pltpu.roll(x, 128 - ROPE_DIM // 2, 1), pltpu.roll(x, ROPE_DIM // 2, 1))
        return x * cs + swapped * sn

    scale = NSA_DH ** -0.5 * LOG2E
    lane_hi = lane - NSA_DH
    blk_in_chunk = (lax.broadcasted_iota(jnp.int32, (tq, 128), 0) % SEL_CHUNK) // SEL_BLOCK
    onehot = ((lane_hi >= 0) & (lane_hi % BIAS_ROWS == 0) & (lane_hi // BIAS_ROWS == blk_in_chunk)).astype(F32)
    ones_rows = jnp.ones((V_ROWS - NSA_DH, tq), BF16)
    for p in range(NSA_HEADS // 2):
        x = f_ref[:, p * 128:(p + 1) * 128]
        xn_t = (x * scale).T
        xr_t = (rope(x) * scale).T
        for e in range(2):
            hd = 2 * p + e
            g, h = hd // NSA_HPG, hd % NSA_HPG
            qn_ref[g, 0, :, h * tq:(h + 1) * tq] = xn_t[e * NSA_DH:(e + 1) * NSA_DH].astype(BF16)
            qr_ref[g, 0, :, h * tq:(h + 1) * tq] = xr_t[e * NSA_DH:(e + 1) * NSA_DH].astype(BF16)

    def kv_piece(idx, p):
        base = NSA_HEADS * NSA_DH + idx * NSA_KV + p * 128
        return f_ref[:, base:base + 128]

    for p in range(2):
        kc = kv_piece(0, p)
        vc = kv_piece(1, p)
        ks = rope(kv_piece(2, p))
        vs_t = kv_piece(3, p).T
        kw = rope(kv_piece(4, p))
        vw_t = kv_piece(5, p).T
        for e in range(2):
            g = 2 * p + e
            sl = slice(e * NSA_DH, (e + 1) * NSA_DH)
            kc_ref[g] = kc[:, sl]
            vc_ref[g] = vc[:, sl]
            ks_low = ks if e == 0 else pltpu.roll(ks, NSA_DH, 1)
            ks_ref[g] = jnp.where(lane < NSA_DH, ks_low, onehot).astype(BF16)
            kw_low = kw if e == 0 else pltpu.roll(kw, NSA_DH, 1)
            kw_ref[g] = jnp.where(lane < NSA_DH, kw_low, jnp.where(lane == NSA_DH, 1.0, 0.0)).astype(BF16)
            vst_ref[g, :NSA_DH] = vs_t[sl].astype(BF16)
            vst_ref[g, NSA_DH:] = ones_rows
            vwt_ref[g, :NSA_DH] = vw_t[sl].astype(BF16)
            vwt_ref[g, NSA_DH:] = ones_rows
    gl = f_ref[:, 2560:2688]
    gt_ref[...] = jax.nn.sigmoid(gl).T[:3 * NSA_HEADS]


def _nsaprep(feats, cos_t, sin_t, tq):
    s = feats.shape[0]
    ni = s // tq
    kv_f32 = jax.ShapeDtypeStruct((NSA_G, s, NSA_DH), F32)
    kv_b16 = jax.ShapeDtypeStruct((NSA_G, s, NSA_DH), BF16)
    ksa_b16 = jax.ShapeDtypeStruct((NSA_G, s, 2 * NSA_DH), BF16)
    kvt_b16 = jax.ShapeDtypeStruct((NSA_G, V_ROWS, s), BF16)
    q_shape = jax.ShapeDtypeStruct((NSA_G, ni, NSA_DH, NSA_HPG * tq), BF16)
    q_spec = pl.BlockSpec((NSA_G, 1, NSA_DH, NSA_HPG * tq), lambda i: (0, i, 0, 0))
    kv_spec = pl.BlockSpec((NSA_G, tq, NSA_DH), lambda i: (0, i, 0))
    ksa_spec = pl.BlockSpec((NSA_G, tq, 2 * NSA_DH), lambda i: (0, i, 0))
    kvt_spec = pl.BlockSpec((NSA_G, V_ROWS, tq), lambda i: (0, 0, i))
    return pl.pallas_call(
        functools.partial(_nsaprep_body, tq=tq),
        grid=(ni,),
        in_specs=[
            pl.BlockSpec((tq, 3072), lambda i: (i, 1)),
            pl.BlockSpec((tq, 128), lambda i: (i, 0)),
            pl.BlockSpec((tq, 128), lambda i: (i, 0)),
        ],
        out_specs=[q_spec, q_spec, kv_spec, kv_spec, ksa_spec, kvt_spec, ksa_spec, kvt_spec,
                   pl.BlockSpec((3 * NSA_HEADS, tq), lambda i: (0, i))],
        out_shape=[q_shape, q_shape, kv_f32, kv_f32, ksa_b16, kvt_b16, ksa_b16, kvt_b16,
                   jax.ShapeDtypeStruct((3 * NSA_HEADS, s), F32)],
        compiler_params=_cparams(("parallel",)),
        name="nsaprep",
    )(feats, cos_t, sin_t)


def _gelu_tanh(x):
    return 0.5 * x * (1.0 + jnp.tanh(0.7978845608028654 * (x + 0.044715 * x * x * x)))


def _compress_body(x_ref, pe1_ref, pe2_ref, w1a_ref, w1b_ref, w2_ref, o_ref, *, ncp, transpose_out):
    x = x_ref[0]
    a = _dot((x + pe1_ref[...]).astype(BF16), w1a_ref[...])
    b = _dot((x + pe2_ref[...]).astype(BF16), w1b_ref[...])
    hid = a + pltpu.roll(b, ncp - 1, 0)
    act = _gelu_tanh(hid).astype(BF16)
    if transpose_out:
        o_ref[0, :NSA_DH] = _dot_nt(w2_ref[...], act).astype(BF16)
        o_ref[0, NSA_DH:] = jnp.ones((V_ROWS - NSA_DH, ncp), BF16)
    else:
        o_ref[0] = _dot(act, w2_ref[...]).astype(BF16)


def _compress(x, pe, w1, w2, transpose_out):
    g, s, dh = x.shape
    ncp = s // CMP_STRIDE
    half = CMP_STRIDE * dh
    xr = x.reshape(g, ncp, half)
    pe1 = pe[:CMP_STRIDE].reshape(1, half)
    pe2 = pe[CMP_STRIDE:].reshape(1, half)
    w1a = w1[:half].astype(BF16)
    w1b = w1[half:].astype(BF16)
    hid = w1.shape[1]
    w2k = (w2.T if transpose_out else w2).astype(BF16)
    full = lambda a, b: pl.BlockSpec((a, b), lambda i: (0, 0))
    if transpose_out:
        out_shape = jax.ShapeDtypeStruct((g, V_ROWS, ncp), BF16)
        out_spec = pl.BlockSpec((1, V_ROWS, ncp), lambda i: (i, 0, 0))
    else:
        out_shape = jax.ShapeDtypeStruct((g, ncp, dh), BF16)
        out_spec = pl.BlockSpec((1, ncp, dh), lambda i: (i, 0, 0))
    return pl.pallas_call(
        functools.partial(_compress_body, ncp=ncp, transpose_out=transpose_out),
        grid=(g,),
        in_specs=[
            pl.BlockSpec((1, ncp, half), lambda i: (i, 0, 0)),
            full(1, half), full(1, half), full(half, hid), full(half, hid), full(*w2k.shape),
        ],
        out_specs=out_spec,
        out_shape=out_shape,
        compiler_params=_cparams(("parallel",)),
        name="compress",
    )(xr, pe1, pe2, w1a, w1b, w2k)


def _rows_bf16(row, n):
    r16 = jnp.broadcast_to(row, (16, row.shape[1])).astype(BF16)
    return jnp.concatenate([r16] * (n // 16), axis=0)


def _flash_biased(sb, vt_blk, m_ref, acc_ref, h):
    m_old = m_ref[h]
    m_new = jnp.maximum(m_old, jnp.max(sb, axis=0, keepdims=True).astype(F32))
    alpha = jnp.exp2(m_old - m_new)
    p = jnp.exp2(sb - _rows_bf16(m_new, sb.shape[0]))
    acc_ref[h] = alpha * acc_ref[h] + _dot(vt_blk, p)
    m_ref[h] = m_new


def _nsa_body(qn_ref, qr_ref, kc_ref, vct_ref, ov_ref, ks_ref, vst_ref, kw_ref, vwt_ref, gt_ref, o_ref,
              bias_ref, oc_ref, qa_ref, qw_ref, st0_ref, st1_ref, st2_ref, st3_ref, m_ref, acc_ref,
              *, tq, ncp, nsel, topn):
    st_refs = (st0_ref, st1_ref, st2_ref, st3_ref)
    g = pl.program_id(0)
    i = pl.program_id(1)
    q0 = i * tq
    bpc = tq // SEL_BLOCK
    tpos = q0 + lax.broadcasted_iota(jnp.int32, (1, tq), 1)
    heads = range(NSA_HPG)
    hcols = lambda h: slice(h * tq, (h + 1) * tq)

    row_io = lax.broadcasted_iota(jnp.int32, (tq, 1), 0)
    col_io = lax.broadcasted_iota(jnp.int32, (1, tq), 1)
    as_bias = lambda keep: jnp.where(keep, 0.0, NEG).astype(BF16)
    causal_bias = as_bias(row_io <= col_io)
    anti_bias = as_bias(row_io > col_io)

    cmp_last = 2 * CMP_STRIDE - 1
    has_cmp = jnp.where(tpos >= cmp_last, 1.0, 0.0)

    def compressed_and_select(ncc, rows):
        nck = ncc * tq
        cbias = [as_bias((cc * tq + row_io) * CMP_STRIDE + cmp_last <= tpos) for cc in range(ncc)]

        def cmp_scores(h):
            qn_h = qn_ref[0, 0, :, hcols(h)]
            return [_dot(kc_ref[0, cc * tq:(cc + 1) * tq, :], qn_h).astype(BF16) + cbias[cc]
                    for cc in range(ncc)]

        psum = None
        ahead = cmp_scores(0)
        for h in heads:
            sbs = ahead
            if h + 1 < NSA_HPG:
                ahead = cmp_scores(h + 1)
            mc = jnp.max(sbs[0], axis=0, keepdims=True)
            for sb in sbs[1:]:
                mc = jnp.maximum(mc, jnp.max(sb, axis=0, keepdims=True))
            mrows = _rows_bf16(mc.astype(F32), tq)
            pc = jnp.concatenate([jnp.exp2(sb - mrows) for sb in sbs], axis=0)
            rv = _dot(vct_ref[0, :, :nck], pc)
            inv = has_cmp / rv[NSA_DH:NSA_DH + 1]
            oc_ref[h] = rv[:NSA_DH] * inv
            pn = pc * _rows_bf16(inv, nck)
            psum = pn if psum is None else psum + pn
        imp = _dot(ov_ref[:rows, :nck], psum)

        jj = lax.broadcasted_iota(jnp.int32, (rows, tq), 0)
        jf = jj.astype(F32)
        cur = tpos // SEL_BLOCK
        forced = (jj == 0) | (jj == cur) | (jj == cur - 1)
        score = jnp.where(forced, -3e38, jnp.where(jj <= cur, imp, -1.0))
        sel = jnp.where(forced, 1.0, 0.0)
        for _ in range(topn - 3):
            mx = jnp.max(score, axis=0, keepdims=True)
            first = jnp.min(jnp.where(score == mx, jf, float(rows)), axis=0, keepdims=True)
            hit = jf == first
            sel = jnp.where(hit, 1.0, sel)
            score = jnp.where(hit, -3e38, score)
        bias_ref[:rows] = jnp.where(sel > 0.0, 0.0, NEG)

    ncc_all = ncp // tq
    nvar = 4 if ncc_all % 4 == 0 and nsel % 4 == 0 and nsel // 4 >= 2 * topn else 1
    live_blocks = (i + 1) * bpc
    for v in range(nvar):
        rows = nsel * (v + 1) // nvar
        lo = nsel * v // nvar

        @pl.when((live_blocks > lo) & (live_blocks <= rows))
        def _(v=v, rows=rows):
            compressed_and_select(ncc_all * (v + 1) // nvar, rows)

    def reset():
        m_ref[...] = jnp.full_like(m_ref, NEG)
        acc_ref[...] = jnp.zeros_like(acc_ref)

    def finish(h):
        acc = acc_ref[h]
        l = acc[NSA_DH:NSA_DH + 1]
        return acc[:NSA_DH] * jnp.where(l > 0.0, 1.0 / l, 0.0)

    row16 = lax.broadcasted_iota(jnp.int32, (BIAS_ROWS, NSA_HPG * tq), 0)

    qa_ref[:NSA_DH] = qr_ref[0, 0]

    def set_bias_rows(c):
        for b in range(bpc):
            brow = bias_ref[pl.ds(c * bpc + b, 1), :]
            brow4 = jnp.concatenate([brow] * NSA_HPG, axis=1)
            lo_row = NSA_DH + BIAS_ROWS * b
            qa_ref[lo_row:lo_row + BIAS_ROWS] = jnp.where(row16 == 0, brow4, 0.0).astype(BF16)

    def sel_scores(c, h):
        kblk = ks_ref[0, pl.ds(pl.multiple_of(c * tq, tq), tq), :]
        return _dot(kblk, qa_ref[:, hcols(h)]).astype(BF16)

    reset()
    set_bias_rows(0)
    for h in heads:
        st_refs[h][...] = sel_scores(0, h)

    def sel_chunk(c):
        set_bias_rows(c + 1)
        vblk = vst_ref[0, :, pl.ds(pl.multiple_of(c * tq, tq), tq)]
        nxt = sel_scores(c + 1, 0)
        for h in heads:
            cur_scores = st_refs[h][...]
            after = sel_scores(c + 1, h + 1) if h + 1 < NSA_HPG else None
            _flash_biased(cur_scores, vblk, m_ref, acc_ref, h)
            st_refs[h][...] = nxt
            nxt = after

    def sel_group(cg, carry):
        for u in range(SEL_UNROLL):
            sel_chunk(SEL_UNROLL * cg + u)
        return carry

    lax.fori_loop(0, i // SEL_UNROLL, sel_group, 0)
    done = (i // SEL_UNROLL) * SEL_UNROLL
    part = SEL_UNROLL // 2
    while part >= 1:
        @pl.when((i & part) != 0)
        def _(base=done, part=part):
            for u in range(part):
                sel_chunk(base + u)
        done = done + (i & part)
        part //= 2

    vblk = vst_ref[0, :, pl.ds(pl.multiple_of(q0, tq), tq)]
    for h in heads:
        _flash_biased(st_refs[h][...] + causal_bias, vblk, m_ref, acc_ref, h)
    o_s = [finish(h) for h in heads]

    reset()
    nback = WINDOW // tq
    win_bias = [causal_bias] + [None] * (nback - 1) + [anti_bias]
    qw_ref[:NSA_DH] = qr_ref[0, 0]
    qw_ref[NSA_DH + BIAS_ROWS:] = jnp.zeros((NSA_DH - BIAS_ROWS, NSA_HPG * tq), BF16)

    def stash_window(j, h):
        exists = jnp.where(i >= j, 0.0, NEG)
        if h == 0:
            qw_ref[NSA_DH:NSA_DH + BIAS_ROWS] = jnp.where(row16 == 0, exists, 0.0).astype(BF16)
        kblk = kw_ref[0, pl.ds(pl.multiple_of(jnp.maximum(i - j, 0) * tq, tq), tq), :]
        sb = _dot(kblk, qw_ref[:, hcols(h)]).astype(BF16)
        st_refs[h][...] = sb if win_bias[j] is None else sb + win_bias[j]

    for h in heads:
        stash_window(0, h)
    for j in range(nback + 1):
        vblk = vwt_ref[0, :, pl.ds(pl.multiple_of(jnp.maximum(i - j, 0) * tq, tq), tq)]
        for h in heads:
            _flash_biased(st_refs[h][...], vblk, m_ref, acc_ref, h)
            if j < nback:
                stash_window(j + 1, h)
    o_w = [finish(h) for h in heads]

    def gate(branch, h):
        return gt_ref[pl.ds(branch * NSA_HEADS + g * NSA_HPG + h, 1), :]

    out_t = [gate(0, h) * oc_ref[h] + gate(1, h) * o_s[h] + gate(2, h) * o_w[h] for h in heads]
    halves = [jnp.concatenate(out_t[2 * p:2 * p + 2], axis=0).T for p in range(NSA_HPG // 2)]
    o_ref[...] = jnp.concatenate(halves, axis=1)


def _nsa(qn, qr, kc, vct, ov, ks, vst, kw, vwt, gt, tq):
    g, ni, dh, w4 = qn.shape
    s = ks.shape[1]
    ncp = kc.shape[1]
    nsel = s // SEL_BLOCK
    topn = min(SEL_TOPK, nsel)
    assert tq == SEL_CHUNK and (tq // SEL_BLOCK) * BIAS_ROWS == dh and w4 == NSA_HPG * tq
    q_spec = pl.BlockSpec((1, 1, dh, w4), lambda a, b: (a, b, 0, 0))
    return pl.pallas_call(
        functools.partial(_nsa_body, tq=tq, ncp=ncp, nsel=nsel, topn=topn),
        grid=(g, ni),
        in_specs=[
            q_spec, q_spec,
            pl.BlockSpec((1, ncp, dh), lambda a, b: (a, 0, 0)),
            pl.BlockSpec((1, V_ROWS, ncp), lambda a, b: (a, 0, 0)),
            pl.BlockSpec((nsel, ncp), lambda a, b: (0, 0)),
            pl.BlockSpec((1, s, 2 * dh), lambda a, b: (a, 0, 0)),
            pl.BlockSpec((1, V_ROWS, s), lambda a, b: (a, 0, 0)),
            pl.BlockSpec((1, s, 2 * dh), lambda a, b: (a, 0, 0)),
            pl.BlockSpec((1, V_ROWS, s), lambda a, b: (a, 0, 0)),
            pl.BlockSpec((3 * NSA_HEADS, tq), lambda a, b: (0, b)),
        ],
        out_specs=pl.BlockSpec((tq, NSA_HPG * dh), lambda a, b: (b, a)),
        out_shape=jax.ShapeDtypeStruct((s, NSA_HEADS * dh), F32),
        scratch_shapes=[
            pltpu.VMEM((nsel, tq), F32),
            pltpu.VMEM((NSA_HPG, dh, tq), F32),
            pltpu.VMEM((2 * dh, w4), BF16),
            pltpu.VMEM((2 * dh, w4), BF16),
            pltpu.VMEM((tq, tq), BF16), pltpu.VMEM((tq, tq), BF16),
            pltpu.VMEM((tq, tq), BF16), pltpu.VMEM((tq, tq), BF16),
            pltpu.VMEM((NSA_HPG, 1, tq), F32),
            pltpu.VMEM((NSA_HPG, V_ROWS, tq), F32),
        ],
        compiler_params=_cparams(("arbitrary", "arbitrary")),
        name="nsa",
    )(qn, qr, kc, vct, ov, ks, vst, kw, vwt, gt)


def _memkv_body(mem_ref, g_ref, w_ref, k_ref, v_ref):
    kv = _dot(_rms(mem_ref[...], g_ref[...]).astype(BF16), w_ref[...])
    width = MEM_HEADS * MEM_DH
    k_ref[...] = kv[:, :width].astype(BF16)
    v_ref[...] = kv[:, width:].astype(BF16)


def _memkv(mem, g, w):
    m, d = mem.shape
    width = MEM_HEADS * MEM_DH
    shp = jax.ShapeDtypeStruct((m, width), BF16)
    return pl.pallas_call(
        _memkv_body,
        out_shape=[shp, shp],
        compiler_params=pltpu.CompilerParams(vmem_limit_bytes=VMEM_LIMIT),
        name="memkv",
    )(mem, g, w)


def _mixout_body(yn_ref, bon_ref, gate_ref, ynsa_ref, h_ref, lnw_ref, lnb_ref, wo_rw_ref, wo_nsa_ref,
                 gpost_ref, mpre_ref, wq_ref, k_ref, v_ref, wom_ref, mpost_ref, o_ref):
    y_rw = ((yn_ref[...] * lnw_ref[...] + lnb_ref[...]) + bon_ref[...]) * gate_ref[...]
    y = _dot(y_rw.astype(BF16), wo_rw_ref[...]) + _dot(ynsa_ref[...].astype(BF16), wo_nsa_ref[...])
    h2 = h_ref[...] + _rms(y, gpost_ref[...])
    q = _dot(_rms(h2, mpre_ref[...]).astype(BF16), wq_ref[...])
    scale = MEM_DH ** -0.5
    outs = []
    for hh in range(MEM_HEADS):
        sl = slice(hh * MEM_DH, (hh + 1) * MEM_DH)
        s = _dot_nt(q[:, sl].astype(BF16), k_ref[:, sl]) * scale
        p = jnp.exp(s - jnp.max(s, axis=-1, keepdims=True))
        p = p / jnp.sum(p, axis=-1, keepdims=True)
        outs.append(_dot(p.astype(BF16), v_ref[:, sl]))
    o = jnp.concatenate(outs, axis=-1).astype(BF16)
    m = _dot(o, wom_ref[...])
    o_ref[...] = h2 + _rms(m, mpost_ref[...])


def _mixout(yn, bon, gate, ynsa, h, lnw, lnb, wo_rw, wo_nsa, gpost, mpre, wq, k, v, wom, mpost, tm=256):
    s, d = h.shape
    rows = lambda n: pl.BlockSpec((tm, n), lambda i: (i, 0))
    full = lambda a: pl.BlockSpec(a.shape, lambda i: (0, 0))
    return pl.pallas_call(
        _mixout_body,
        grid=(s // tm,),
        in_specs=[rows(RW_WIDTH), rows(RW_WIDTH), rows(RW_WIDTH), rows(RW_WIDTH), rows(d),
                  full(lnw), full(lnb), full(wo_rw), full(wo_nsa), full(gpost), full(mpre), full(wq),
                  full(k), full(v), full(wom), full(mpost)],
        out_specs=rows(d),
        out_shape=jax.ShapeDtypeStruct((s, d), F32),
        compiler_params=_cparams(("parallel",)),
        name="mixout",
    )(yn, bon, gate, ynsa, h, lnw, lnb, wo_rw, wo_nsa, gpost, mpre, wq, k, v, wom, mpost)


def _rope_tables(s):
    half = ROPE_DIM // 2
    inv_freq = ROPE_THETA ** (-jnp.arange(half, dtype=F32) * 2.0 / ROPE_DIM)
    ang = jnp.arange(s, dtype=jnp.int32).astype(F32)[:, None] * inv_freq[None, :]
    cos, sin = jnp.cos(ang), jnp.sin(ang)
    ones = jnp.ones((s, NSA_DH - ROPE_DIM), F32)
    zeros = jnp.zeros((s, NSA_DH - ROPE_DIM), F32)
    cos_h = jnp.concatenate([cos, cos, ones], axis=1)
    sin_h = jnp.concatenate([-sin, sin, zeros], axis=1)
    return jnp.tile(cos_h, (1, 2)), jnp.tile(sin_h, (1, 2))


def _overlap_t(s):
    ncp = s // CMP_STRIDE
    nsel = s // SEL_BLOCK
    cmp_start = jnp.arange(ncp)[None, :] * CMP_STRIDE
    sel_start = jnp.arange(nsel)[:, None] * SEL_BLOCK
    ov = (cmp_start < sel_start + SEL_BLOCK) & (cmp_start + 2 * CMP_STRIDE - 1 >= sel_start)
    ov = ov & (jnp.arange(ncp)[None, :] < ncp - 1)
    return ov.astype(BF16)


def kernel(x, mem, ffn1_pre_g, ffn1_w_gate, ffn1_w_up, ffn1_w_down, ffn1_post_g, mix_pre_g, w_in, rw_mu, rw_w0, rw_w_up, rw_a0, rw_a_up, rw_g_up, rw_k_k, rw_k_a, rw_r_k, rw_ln_w, rw_ln_b, cmp_pe_k, cmp_w1_k, cmp_w2_k, cmp_pe_v, cmp_w1_v, cmp_w2_v, w_out, mix_post_g, mem_pre_g, mem_norm_g, mem_w_q, mem_w_kv, mem_w_o, mem_post_g, ffn2_pre_g, ffn2_w_gate, ffn2_w_up, ffn2_w_down, ffn2_post_g):
    b, s, d = x.shape
    tq = SEL_CHUNK
    assert b == 1 and d == D_MODEL and s % (CMP_STRIDE * tq) == 0
    row = lambda v: v.reshape(1, -1).astype(F32)

    def ffn_weights(wg, wu, wd):
        return wg.astype(BF16), wu.astype(BF16), wd.astype(BF16)

    h = x[0]
    h = _ffn(h, row(ffn1_pre_g), *ffn_weights(ffn1_w_gate, ffn1_w_up, ffn1_w_down), row(ffn1_post_g))

    rw_cols = 3 * RW_WIDTH + RW_LORA
    nsa_main = NSA_HEADS * NSA_DH + 6 * NSA_KV
    zc = lambda n: jnp.zeros((d, n), BF16)
    w_in_b = w_in.astype(BF16)
    w_in_r = jnp.concatenate([
        w_in_b[:, :3 * RW_WIDTH],
        w_in_b[:, rw_cols:rw_cols + nsa_main],
        w_in_b[:, rw_cols + nsa_main:], zc(128 - 3 * NSA_HEADS),
        w_in_b[:, 3 * RW_WIDTH:rw_cols], zc(RW_LORA_PAD - RW_LORA),
    ], axis=1)
    feats = _inproj(h, row(mix_pre_g), w_in_r)

    mu_a = row(rw_mu[:3 * RW_WIDTH])
    mu_l = row(jnp.pad(rw_mu[3 * RW_WIDTH:], (0, RW_LORA_PAD - RW_LORA)))
    lora_w = lambda w, off: jnp.pad(w, ((off, RW_LORA_PAD - off - w.shape[0]), (0, 0))).astype(BF16)
    at, rt, bt, kt, v, pt, gate, bonus = _rwprep(
        feats, mu_a, mu_l, row(rw_w0), row(rw_a0), row(rw_k_k), row(rw_k_a), row(rw_r_k),
        lora_w(rw_w_up, 0), lora_w(rw_a_up, 64), lora_w(rw_g_up, 128))
    yn = _rwscan(at, rt, bt, kt, v, pt)

    cos_t, sin_t = _rope_tables(s)
    qn, qr, kc, vc, ks, vst, kw, vwt, gt = _nsaprep(feats, cos_t, sin_t, tq)
    k_cmp = _compress(kc, cmp_pe_k, cmp_w1_k, cmp_w2_k, transpose_out=False)
    v_cmp_t = _compress(vc, cmp_pe_v, cmp_w1_v, cmp_w2_v, transpose_out=True)
    y_nsa = _nsa(qn, qr, k_cmp, v_cmp_t, _overlap_t(s), ks, vst, kw, vwt, gt, tq)

    mem_k, mem_v = _memkv(mem[0], row(mem_norm_g), mem_w_kv.astype(BF16))
    h = _mixout(yn, bonus, gate, y_nsa, h, row(rw_ln_w), row(rw_ln_b),
                w_out[:RW_WIDTH].astype(BF16), w_out[RW_WIDTH:].astype(BF16), row(mix_post_g),
                row(mem_pre_g), mem_w_q.astype(BF16), mem_k, mem_v, mem_w_o.astype(BF16), row(mem_post_g))

    h = _ffn(h, row(ffn2_pre_g), *ffn_weights(ffn2_w_gate, ffn2_w_up, ffn2_w_down), row(ffn2_post_g))
    return h[None]
```

```python
import functools

import jax
import jax.numpy as jnp
from jax import lax
from jax.experimental import pallas as pl
from jax.experimental.pallas import tpu as pltpu

F32 = jnp.float32
BF16 = jnp.bfloat16

D_MODEL = 2048
EPS = 1e-6

RW_HEADS = 16
RW_N = 64
RW_WIDTH = 1024
RW_LORA = 288
RW_LORA_PAD = 384
RW_GN_EPS = 64e-5
RW_CHUNK = 64
RW_GROUP = 2

NSA_HEADS = 16
NSA_G = 4
NSA_HPG = 4
NSA_DH = 64
NSA_KV = 256
CMP_STRIDE = 16
SEL_BLOCK = 64
SEL_TOPK = 16
WINDOW = 512
FORCE_BONUS = 1000.0
ROPE_THETA = 500000.0
ROPE_DIM = 16

MEM_HEADS = 4
MEM_DH = 128

LOG2E = 1.4426950408889634
SEL_CHUNK = 256
SEL_UNROLL = 8
BIAS_ROWS = 16
V_ROWS = NSA_DH + 16

NEG = -1e30
VMEM_LIMIT = 56 * 1024 * 1024


def _cparams(sem):
    return pltpu.CompilerParams(dimension_semantics=sem, vmem_limit_bytes=VMEM_LIMIT)


def _rms(x, g):
    return x * lax.rsqrt(jnp.mean(x * x, axis=-1, keepdims=True) + EPS) * g


def _dot(a, b):
    return jnp.dot(a, b, preferred_element_type=F32)


def _dot_nt(a, b):
    return lax.dot_general(a, b, (((1,), (1,)), ((), ())), preferred_element_type=F32)


def _split2(x):
    hi = x.astype(BF16)
    lo = (x - hi.astype(F32)).astype(BF16)
    return hi, lo


def _split3(x):
    hi = x.astype(BF16)
    r1 = x - hi.astype(F32)
    mid = r1.astype(BF16)
    lo = (r1 - mid.astype(F32)).astype(BF16)
    return hi, mid, lo


def _ffn_body(h_ref, gpre_ref, wg_ref, wu_ref, wd_ref, wgt_ref, wut_ref, wdt_ref, gpost_ref, o_ref, xn_ref,
              *, nj):
    j = pl.program_id(1)

    @pl.when(j == 0)
    def _():
        xn_ref[...] = _rms(h_ref[...], gpre_ref[...]).astype(BF16)
        o_ref[...] = jnp.zeros_like(o_ref)

    xn = xn_ref[...]

    def swiglu_part(wg, wu, wd):
        g = _dot(xn, wg)
        u = _dot(xn, wu)
        return _dot((g * jax.nn.sigmoid(g) * u).astype(BF16), wd)

    o_ref[...] += swiglu_part(wg_ref[...], wu_ref[...], wd_ref[...])

    @pl.when(j == nj - 1)
    def _():
        y = o_ref[...] + swiglu_part(wgt_ref[...], wut_ref[...], wdt_ref[...])
        o_ref[...] = h_ref[...] + 0.5 * _rms(y, gpost_ref[...])


def _ffn(h, pre_g, wg, wu, wd, post_g, tm=512, tf=512):
    s, d = h.shape
    ff = wg.shape[1]
    nj = ff // tf
    tail = ff - nj * tf
    assert 0 < tail and tail % 128 == 0
    wgt, wut, wdt = wg[:, nj * tf:], wu[:, nj * tf:], wd[nj * tf:]
    const = lambda a: pl.BlockSpec(a.shape, lambda i, j: (0, 0))
    return pl.pallas_call(
        functools.partial(_ffn_body, nj=nj),
        grid=(s // tm, nj),
        in_specs=[
            pl.BlockSpec((tm, d), lambda i, j: (i, 0)),
            const(pre_g),
            pl.BlockSpec((d, tf), lambda i, j: (0, j)),
            pl.BlockSpec((d, tf), lambda i, j: (0, j)),
            pl.BlockSpec((tf, d), lambda i, j: (j, 0)),
            const(wgt), const(wut), const(wdt),
            const(post_g),
        ],
        out_specs=pl.BlockSpec((tm, d), lambda i, j: (i, 0)),
        out_shape=jax.ShapeDtypeStruct((s, d), F32),
        scratch_shapes=[pltpu.VMEM((tm, d), BF16)],
        compiler_params=_cparams(("parallel", "arbitrary")),
        name="ffn",
    )(h, pre_g, wg, wu, wd, wgt, wut, wdt, post_g)


def _inproj_body(h_ref, g_ref, w_ref, o_ref, xn_ref):
    @pl.when(pl.program_id(1) == 0)
    def _():
        xn_ref[...] = _rms(h_ref[...], g_ref[...]).astype(BF16)

    o_ref[...] = _dot(xn_ref[...], w_ref[...])


def _inproj(h, g, w, tm=1024, tn=2048):
    s, d = h.shape
    n = w.shape[1]
    return pl.pallas_call(
        _inproj_body,
        grid=(s // tm, n // tn),
        in_specs=[
            pl.BlockSpec((tm, d), lambda i, j: (i, 0)),
            pl.BlockSpec((1, d), lambda i, j: (0, 0)),
            pl.BlockSpec((d, tn), lambda i, j: (0, j)),
        ],
        out_specs=pl.BlockSpec((tm, tn), lambda i, j: (i, j)),
        out_shape=jax.ShapeDtypeStruct((s, n), F32),
        scratch_shapes=[pltpu.VMEM((tm, d), BF16)],
        compiler_params=_cparams(("parallel", "arbitrary")),
        name="inproj",
    )(h, g, w)


def _rwprep_body(f_ref, fp_ref, lo_ref, lop_ref, mua_ref, mul_ref, w0_ref, a0_ref, kk_ref, ka_ref,
                 rk_ref, wup_ref, aup_ref, gup_ref, e_ref, tri_ref,
                 at_ref, rt_ref, bt_ref, kt_ref, v_ref, pt_ref, g_ref, bon_ref, *, tm):
    first = pl.program_id(0) == 0

    def shifted(x, prev_blk):
        prev_last = jnp.where(first, 0.0, prev_blk[7:8, :])
        rolled = pltpu.roll(x, 1, 0)
        row = lax.broadcasted_iota(jnp.int32, x.shape, 0)
        return jnp.where(row == 0, prev_last, rolled)

    f = f_ref[...]
    fs = f + mua_ref[...] * (shifted(f, fp_ref[...]) - f)
    lo = lo_ref[...]
    los = lo + mul_ref[...] * (shifted(lo, lop_ref[...]) - lo)
    r = fs[:, :RW_WIDTH]
    k = fs[:, RW_WIDTH:2 * RW_WIDTH]
    v = fs[:, 2 * RW_WIDTH:]

    wx = _dot(jnp.tanh(los).astype(BF16), wup_ref[...])
    ax = _dot(los.astype(BF16), aup_ref[...])
    g_ref[...] = _dot(jax.nn.sigmoid(los).astype(BF16), gup_ref[...])

    z = -(w0_ref[...] + wx)
    softplus = jnp.maximum(z, 0.0) + jnp.log(1.0 + jnp.exp(-jnp.abs(z)))
    ld = -jnp.exp(-softplus - 0.5)
    a = jax.nn.sigmoid(a0_ref[...] + ax)

    e = e_ref[...]

    def headsum(x):
        parts = []
        for c in range(RW_WIDTH // 256):
            hi, lo_ = _split2(x[:, c * 256:(c + 1) * 256])
            parts.append(_dot(hi, e) + _dot(lo_, e))
        return jnp.concatenate(parts, axis=1)

    kk0 = k * kk_ref[...]
    kk = kk0 / jnp.maximum(jnp.sqrt(headsum(kk0 * kk0)), 1e-12)
    kmod = k * (1.0 + (a - 1.0) * ka_ref[...])
    bon_ref[...] = headsum(r * kmod * rk_ref[...]) * v

    tri = tri_ref[...]
    cums = []
    for c in range(tm // RW_CHUNK):
        hi, mid, lo_ = _split3(ld[c * RW_CHUNK:(c + 1) * RW_CHUNK])
        cums.append(_dot(tri, hi) + _dot(tri, mid) + _dot(tri, lo_))
    cum = jnp.concatenate(cums, axis=0)
    e_in = jnp.exp(cum)
    e_out = jnp.exp(-cum)
    at_ref[...] = (-kk * jnp.exp(cum - ld)).astype(BF16)
    rt_ref[...] = r * e_in
    bt_ref[...] = (kk * a * e_out).astype(BF16)
    kt_ref[...] = (kmod * e_out).astype(BF16)
    v_ref[...] = v.astype(BF16)
    for c in range(tm // RW_CHUNK):
        pt_ref[c] = e_in[c * RW_CHUNK + RW_CHUNK - 1:(c + 1) * RW_CHUNK, :]


def _rwprep(feats, mu_a, mu_l, w0, a0, k_k, k_a, r_k, wup, aup, gup, tm=256):
    s = feats.shape[0]
    nb8 = tm // 8
    lora_blk = (2 * 3072 - RW_LORA_PAD) // RW_LORA_PAD
    e = (lax.broadcasted_iota(jnp.int32, (256, 256), 0) // RW_N
         == lax.broadcasted_iota(jnp.int32, (256, 256), 1) // RW_N).astype(BF16)
    tri = (lax.broadcasted_iota(jnp.int32, (RW_CHUNK, RW_CHUNK), 0)
           >= lax.broadcasted_iota(jnp.int32, (RW_CHUNK, RW_CHUNK), 1)).astype(BF16)
    row = lambda n: pl.BlockSpec((1, n), lambda i: (0, 0))
    full = lambda a, b: pl.BlockSpec((a, b), lambda i: (0, 0))
    dense_shape = jax.ShapeDtypeStruct((s, RW_WIDTH), F32)
    dense_b16 = jax.ShapeDtypeStruct((s, RW_WIDTH), BF16)
    dense_spec = pl.BlockSpec((tm, RW_WIDTH), lambda i: (i, 0))
    ncb = tm // RW_CHUNK
    return pl.pallas_call(
        functools.partial(_rwprep_body, tm=tm),
        grid=(s // tm,),
        in_specs=[
            pl.BlockSpec((tm, 3072), lambda i: (i, 0)),
            pl.BlockSpec((8, 3072), lambda i: (jnp.maximum(i * nb8 - 1, 0), 0)),
            pl.BlockSpec((tm, RW_LORA_PAD), lambda i: (i, lora_blk)),
            pl.BlockSpec((8, RW_LORA_PAD), lambda i: (jnp.maximum(i * nb8 - 1, 0), lora_blk)),
            row(3072), row(RW_LORA_PAD), row(RW_WIDTH), row(RW_WIDTH), row(RW_WIDTH), row(RW_WIDTH),
            row(RW_WIDTH),
            full(RW_LORA_PAD, RW_WIDTH), full(RW_LORA_PAD, RW_WIDTH), full(RW_LORA_PAD, RW_WIDTH),
            full(256, 256), full(RW_CHUNK, RW_CHUNK),
        ],
        out_specs=[dense_spec] * 5 + [
            pl.BlockSpec((ncb, 1, RW_WIDTH), lambda i: (i, 0, 0)),
            dense_spec,
            dense_spec,
        ],
        out_shape=[dense_b16, dense_shape, dense_b16, dense_b16, dense_b16] + [
            jax.ShapeDtypeStruct((s // RW_CHUNK, 1, RW_WIDTH), F32),
            dense_shape,
            dense_shape,
        ],
        compiler_params=_cparams(("parallel",)),
        name="rwprep",
    )(feats, feats, feats, feats, mu_a, mu_l, w0, a0, k_k, k_a, r_k, wup, aup, gup, e, tri)


def _dot_tn(a, b):
    return lax.dot_general(a, b, (((0,), (0,)), ((), ())), preferred_element_type=F32)


def _rwscan_pairs_body(at_ref, rt_ref, bt_ref, kt_ref, v_ref, pt_ref, y_ref, s_ref, *, nc):
    @pl.when(pl.program_id(0) == 0)
    def _():
        s_ref[...] = jnp.zeros_like(s_ref)

    t = RW_CHUNK
    w = RW_GROUP * RW_N
    row = lax.broadcasted_iota(jnp.int32, (t, w), 0)
    col = lax.broadcasted_iota(jnp.int32, (t, w), 1)
    colh = col % RW_N
    strict = row > colh
    incl = row >= colh
    eye_pair = (row == colh).astype(F32)
    head_of_lane = col // RW_N
    r2 = lax.broadcasted_iota(jnp.int32, (w, w), 0)
    c2 = lax.broadcasted_iota(jnp.int32, (w, w), 1)
    blk_f = ((r2 // RW_N) == (c2 // RW_N)).astype(F32)
    blk_b = blk_f.astype(BF16)
    eye_w = (r2 == c2).astype(F32)

    def bd(x):
        return jnp.concatenate([x] * RW_GROUP, axis=0) * blk_b

    def run_step():
        npair = RW_HEADS // RW_GROUP
        items = [(c, p) for c in range(nc) for p in range(npair)]
        pairs = range(len(items))
        sls = [(slice(c * t, (c + 1) * t), slice(p * w, (p + 1) * w)) for c, p in items]
        pts = [pt_ref[c][:, p * w:(p + 1) * w] for c, p in items]
        bf = lambda xs: [x.astype(BF16) for x in xs]
        a_b = [at_ref[sl] for sl in sls]
        r_f = [rt_ref[sl] for sl in sls]
        b_b = [bt_ref[sl] for sl in sls]
        k_b = [kt_ref[sl] for sl in sls]
        v_b = [v_ref[sl] for sl in sls]
        gg = [_dot_nt(jnp.concatenate([a_b[p], r_f[p].astype(BF16)], axis=0),
                      jnp.concatenate([bd(b_b[p]), bd(k_b[p])], axis=0)) for p in pairs]
        a_ab = [jnp.where(strict, g[:t, :w], 0.0) for g in gg]
        a_ak = bf([jnp.where(strict, g[:t, w:], 0.0) for g in gg])
        a_rb = bf([jnp.where(incl, g[t:, :w], 0.0) for g in gg])
        a_rk = bf([jnp.where(incl, g[t:, w:], 0.0) for g in gg])
        inv = [eye_pair + x for x in a_ab]
        pk = a_ab
        for _ in range(5):
            pkb = bf(pk)
            pk = [_dot(pkb[p], bd(pkb[p])) for p in pairs]
            pkb = bf(pk)
            inv = [inv[p] + _dot(inv[p].astype(BF16), bd(pkb[p])) for p in pairs]
        inv_b = bf(inv)
        w1 = bf([_dot(a_ak[p], bd(v_b[p])) for p in pairs])
        az = [_dot(inv_b[p], jnp.concatenate([bd(a_b[p]), bd(w1[p])], axis=1)) for p in pairs]
        atp = bf([x[:, :w] for x in az])
        z0 = bf([x[:, w:] for x in az])
        ry = [_dot(a_rb[p], jnp.concatenate([bd(atp[p]), bd(z0[p])], axis=1)) for p in pairs]
        rh = bf([r_f[p] + ry[p][:, :w] for p in pairs])
        y0 = [ry[p][:, w:] + _dot(a_rk[p], bd(v_b[p])) for p in pairs]
        m_bd = bf([(eye_w + _dot_tn(atp[p], b_b[p])) * blk_f * pts[p] for p in pairs])
        c_full = [(_dot_tn(z0[p], b_b[p]) + _dot_tn(v_b[p], k_b[p])) * blk_f * pts[p] for p in pairs]
        c_pair = [sum(x[hh * RW_N:(hh + 1) * RW_N] for hh in range(RW_GROUP)) for x in c_full]

        def headmean(x):
            out = jnp.zeros_like(x)
            for hh in range(RW_GROUP):
                mine = head_of_lane == hh
                out = jnp.where(mine, jnp.sum(jnp.where(mine, x, 0.0), axis=-1, keepdims=True), out)
            return out * (1.0 / RW_N)

        state = [s_ref[p] for p in range(npair)]
        for q in pairs:
            p = items[q][1]
            st_b = state[p].astype(BF16)
            y = _dot_nt(rh[q], bd(st_b)) + y0[q]
            state[p] = _dot(st_b, m_bd[q]) + c_pair[q]
            yc = y - headmean(y)
            y_ref[sls[q]] = yc * lax.rsqrt(headmean(yc * yc) + RW_GN_EPS)
        for p in range(npair):
            s_ref[p] = state[p]

    run_step()


def _rwscan(at, rt, bt, kt, v, pt, rows=512):
    s = at.shape[0]
    nc = rows // RW_CHUNK
    dense_spec = pl.BlockSpec((rows, RW_WIDTH), lambda i: (i, 0))
    return pl.pallas_call(
        functools.partial(_rwscan_pairs_body, nc=nc),
        grid=(s // rows,),
        in_specs=[dense_spec] * 5 + [pl.BlockSpec((nc, 1, RW_WIDTH), lambda i: (i, 0, 0))],
        out_specs=dense_spec,
        out_shape=jax.ShapeDtypeStruct((s, RW_WIDTH), F32),
        scratch_shapes=[pltpu.VMEM((RW_HEADS // RW_GROUP, RW_N, RW_GROUP * RW_N), F32)],
        compiler_params=_cparams(("arbitrary",)),
        name="rwscan",
    )(at, rt, bt, kt, v, pt)


def _nsaprep_body(f_ref, cos_ref, sin_ref, qn_ref, qr_ref, kc_ref, vc_ref, ks_ref, vst_ref, kw_ref,
                  vwt_ref, gt_ref, *, tq):
    cs = cos_ref[...]
    sn = sin_ref[...]
    lane = lax.broadcasted_iota(jnp.int32, (tq, 128), 1)
    first8 = (lane % NSA_DH) < (ROPE_DIM // 2)

    def rope(x):
        swapped = jnp.where(first8, pltpu.roll(x, 128 - ROPE_DIM // 2, 1), pltpu.roll(x, ROPE_DIM // 2, 1))
        return x * cs + swapped * sn

    scale = NSA_DH ** -0.5 * LOG2E
    lane_hi = lane - NSA_DH
    blk_in_chunk = (lax.broadcasted_iota(jnp.int32, (tq, 128), 0) % SEL_CHUNK) // SEL_BLOCK
    onehot = ((lane_hi >= 0) & (lane_hi % BIAS_ROWS == 0) & (lane_hi // BIAS_ROWS == blk_in_chunk)).astype(F32)
    ones_rows = jnp.ones((V_ROWS - NSA_DH, tq), BF16)
    for p in range(NSA_HEADS // 2):
        x = f_ref[:, p * 128:(p + 1) * 128]
        xn_t = (x * scale).T
        xr_t = (rope(x) * scale).T
        for e in range(2):
            hd = 2 * p + e
            g, h = hd // NSA_HPG, hd % NSA_HPG
            qn_ref[g, 0, :, h * tq:(h + 1) * tq] = xn_t[e * NSA_DH:(e + 1) * NSA_DH].astype(BF16)
            qr_ref[g, 0, :, h * tq:(h + 1) * tq] = xr_t[e * NSA_DH:(e + 1) * NSA_DH].astype(BF16)

    def kv_piece(idx, p):
        base = NSA_HEADS * NSA_DH + idx * NSA_KV + p * 128
        return f_ref[:, base:base + 128]

    for p in range(2):
        kc = kv_piece(0, p)
        vc = kv_piece(1, p)
        ks = rope(kv_piece(2, p))
        vs_t = kv_piece(3, p).T
        kw = rope(kv_piece(4, p))
        vw_t = kv_piece(5, p).T
        for e in range(2):
            g = 2 * p + e
            sl = slice(e * NSA_DH, (e + 1) * NSA_DH)
            kc_ref[g] = kc[:, sl]
            vc_ref[g] = vc[:, sl]
            ks_low = ks if e == 0 else pltpu.roll(ks, NSA_DH, 1)
            ks_ref[g] = jnp.where(lane < NSA_DH, ks_low, onehot).astype(BF16)
            kw_low = kw if e == 0 else pltpu.roll(kw, NSA_DH, 1)
            kw_ref[g] = jnp.where(lane < NSA_DH, kw_low, jnp.where(lane == NSA_DH, 1.0, 0.0)).astype(BF16)
            vst_ref[g, :NSA_DH] = vs_t[sl].astype(BF16)
            vst_ref[g, NSA_DH:] = ones_rows
            vwt_ref[g, :NSA_DH] = vw_t[sl].astype(BF16)
            vwt_ref[g, NSA_DH:] = ones_rows
    gl = f_ref[:, 2560:2688]
    gt_ref[...] = jax.nn.sigmoid(gl).T[:3 * NSA_HEADS]


def _nsaprep(feats, cos_t, sin_t, tq):
    s = feats.shape[0]
    ni = s // tq
    kv_f32 = jax.ShapeDtypeStruct((NSA_G, s, NSA_DH), F32)
    kv_b16 = jax.ShapeDtypeStruct((NSA_G, s, NSA_DH), BF16)
    ksa_b16 = jax.ShapeDtypeStruct((NSA_G, s, 2 * NSA_DH), BF16)
    kvt_b16 = jax.ShapeDtypeStruct((NSA_G, V_ROWS, s), BF16)
    q_shape = jax.ShapeDtypeStruct((NSA_G, ni, NSA_DH, NSA_HPG * tq), BF16)
    q_spec = pl.BlockSpec((NSA_G, 1, NSA_DH, NSA_HPG * tq), lambda i: (0, i, 0, 0))
    kv_spec = pl.BlockSpec((NSA_G, tq, NSA_DH), lambda i: (0, i, 0))
    ksa_spec = pl.BlockSpec((NSA_G, tq, 2 * NSA_DH), lambda i: (0, i, 0))
    kvt_spec = pl.BlockSpec((NSA_G, V_ROWS, tq), lambda i: (0, 0, i))
    return pl.pallas_call(
        functools.partial(_nsaprep_body, tq=tq),
        grid=(ni,),
        in_specs=[
            pl.BlockSpec((tq, 3072), lambda i: (i, 1)),
            pl.BlockSpec((tq, 128), lambda i: (i, 0)),
            pl.BlockSpec((tq, 128), lambda i: (i, 0)),
        ],
        out_specs=[q_spec, q_spec, kv_spec, kv_spec, ksa_spec, kvt_spec, ksa_spec, kvt_spec,
                   pl.BlockSpec((3 * NSA_HEADS, tq), lambda i: (0, i))],
        out_shape=[q_shape, q_shape, kv_f32, kv_f32, ksa_b16, kvt_b16, ksa_b16, kvt_b16,
                   jax.ShapeDtypeStruct((3 * NSA_HEADS, s), F32)],
        compiler_params=_cparams(("parallel",)),
        name="nsaprep",
    )(feats, cos_t, sin_t)


def _gelu_tanh(x):
    return 0.5 * x * (1.0 + jnp.tanh(0.7978845608028654 * (x + 0.044715 * x * x * x)))


def _compress_body(x_ref, pe1_ref, pe2_ref, w1a_ref, w1b_ref, w2_ref, o_ref, *, ncp, transpose_out):
    x = x_ref[0]
    a = _dot((x + pe1_ref[...]).astype(BF16), w1a_ref[...])
    b = _dot((x + pe2_ref[...]).astype(BF16), w1b_ref[...])
    hid = a + pltpu.roll(b, ncp - 1, 0)
    act = _gelu_tanh(hid).astype(BF16)
    if transpose_out:
        o_ref[0, :NSA_DH] = _dot_nt(w2_ref[...], act).astype(BF16)
        o_ref[0, NSA_DH:] = jnp.ones((V_ROWS - NSA_DH, ncp), BF16)
    else:
        o_ref[0] = _dot(act, w2_ref[...]).astype(BF16)


def _compress(x, pe, w1, w2, transpose_out):
    g, s, dh = x.shape
    ncp = s // CMP_STRIDE
    half = CMP_STRIDE * dh
    xr = x.reshape(g, ncp, half)
    pe1 = pe[:CMP_STRIDE].reshape(1, half)
    pe2 = pe[CMP_STRIDE:].reshape(1, half)
    w1a = w1[:half].astype(BF16)
    w1b = w1[half:].astype(BF16)
    hid = w1.shape[1]
    w2k = (w2.T if transpose_out else w2).astype(BF16)
    full = lambda a, b: pl.BlockSpec((a, b), lambda i: (0, 0))
    if transpose_out:
        out_shape = jax.ShapeDtypeStruct((g, V_ROWS, ncp), BF16)
        out_spec = pl.BlockSpec((1, V_ROWS, ncp), lambda i: (i, 0, 0))
    else:
        out_shape = jax.ShapeDtypeStruct((g, ncp, dh), BF16)
        out_spec = pl.BlockSpec((1, ncp, dh), lambda i: (i, 0, 0))
    return pl.pallas_call(
        functools.partial(_compress_body, ncp=ncp, transpose_out=transpose_out),
        grid=(g,),
        in_specs=[
            pl.BlockSpec((1, ncp, half), lambda i: (i, 0, 0)),
            full(1, half), full(1, half), full(half, hid), full(half, hid), full(*w2k.shape),
        ],
        out_specs=out_spec,
        out_shape=out_shape,
        compiler_params=_cparams(("parallel",)),
        name="compress",
    )(xr, pe1, pe2, w1a, w1b, w2k)


def _rows_bf16(row, n):
    r16 = jnp.broadcast_to(row, (16, row.shape[1])).astype(BF16)
    return jnp.concatenate([r16] * (n // 16), axis=0)


def _flash_biased(sb, vt_blk, m_ref, acc_ref, h):
    m_old = m_ref[h]
    m_new = jnp.maximum(m_old, jnp.max(sb, axis=0, keepdims=True).astype(F32))
    alpha = jnp.exp2(m_old - m_new)
    p = jnp.exp2(sb - _rows_bf16(m_new, sb.shape[0]))
    acc_ref[h] = alpha * acc_ref[h] + _dot(vt_blk, p)
    m_ref[h] = m_new


def _nsa_body(qn_ref, qr_ref, kc_ref, vct_ref, ov_ref, ks_ref, vst_ref, kw_ref, vwt_ref, gt_ref, o_ref,
              bias_ref, oc_ref, qa_ref, qw_ref, st0_ref, st1_ref, st2_ref, st3_ref, m_ref, acc_ref,
              *, tq, ncp, nsel, topn):
    st_refs = (st0_ref, st1_ref, st2_ref, st3_ref)
    g = pl.program_id(0)
    i = pl.program_id(1)
    q0 = i * tq
    bpc = tq // SEL_BLOCK
    tpos = q0 + lax.broadcasted_iota(jnp.int32, (1, tq), 1)
    heads = range(NSA_HPG)
    hcols = lambda h: slice(h * tq, (h + 1) * tq)

    row_io = lax.broadcasted_iota(jnp.int32, (tq, 1), 0)
    col_io = lax.broadcasted_iota(jnp.int32, (1, tq), 1)
    as_bias = lambda keep: jnp.where(keep, 0.0, NEG).astype(BF16)
    causal_bias = as_bias(row_io <= col_io)
    anti_bias = as_bias(row_io > col_io)

    cmp_last = 2 * CMP_STRIDE - 1
    has_cmp = jnp.where(tpos >= cmp_last, 1.0, 0.0)

    def compressed_and_select(ncc, rows):
        nck = ncc * tq
        cbias = [as_bias((cc * tq + row_io) * CMP_STRIDE + cmp_last <= tpos) for cc in range(ncc)]

        def cmp_scores(h):
            qn_h = qn_ref[0, 0, :, hcols(h)]
            return [_dot(kc_ref[0, cc * tq:(cc + 1) * tq, :], qn_h).astype(BF16) + cbias[cc]
                    for cc in range(ncc)]

        psum = None
        ahead = cmp_scores(0)
        for h in heads:
            sbs = ahead
            if h + 1 < NSA_HPG:
                ahead = cmp_scores(h + 1)
            mc = jnp.max(sbs[0], axis=0, keepdims=True)
            for sb in sbs[1:]:
                mc = jnp.maximum(mc, jnp.max(sb, axis=0, keepdims=True))
            mrows = _rows_bf16(mc.astype(F32), tq)
            pc = jnp.concatenate([jnp.exp2(sb - mrows) for sb in sbs], axis=0)
            rv = _dot(vct_ref[0, :, :nck], pc)
            inv = has_cmp / rv[NSA_DH:NSA_DH + 1]
            oc_ref[h] = rv[:NSA_DH] * inv
            pn = pc * _rows_bf16(inv, nck)
            psum = pn if psum is None else psum + pn
        imp = _dot(ov_ref[:rows, :nck], psum)

        jj = lax.broadcasted_iota(jnp.int32, (rows, tq), 0)
        jf = jj.astype(F32)
        cur = tpos // SEL_BLOCK
        forced = (jj == 0) | (jj == cur) | (jj == cur - 1)
        taken = -(2.0 ** 127)
        score = jnp.where(forced, taken, jnp.where(jj <= cur, imp, -1.0))
        for _ in range(topn - 3):
            mx = jnp.max(score, axis=0, keepdims=True)
            first = jnp.min(jnp.where(score == mx, jf, float(rows)), axis=0, keepdims=True)
            score = jnp.where(jf == first, taken, score)
        bias_ref[:rows] = jnp.where(score == taken, 0.0, NEG)

    ncc_all = ncp // tq
    nvar = 4 if ncc_all % 4 == 0 and nsel % 4 == 0 and nsel // 4 >= 2 * topn else 1
    live_blocks = (i + 1) * bpc
    for v in range(nvar):
        rows = nsel * (v + 1) // nvar
        lo = nsel * v // nvar

        @pl.when((live_blocks > lo) & (live_blocks <= rows))
        def _(v=v, rows=rows):
            compressed_and_select(ncc_all * (v + 1) // nvar, rows)

    def reset():
        m_ref[...] = jnp.full_like(m_ref, NEG)
        acc_ref[...] = jnp.zeros_like(acc_ref)

    def finish(h):
        acc = acc_ref[h]
        l = acc[NSA_DH:NSA_DH + 1]
        return acc[:NSA_DH] * jnp.where(l > 0.0, 1.0 / l, 0.0)

    row16 = lax.broadcasted_iota(jnp.int32, (BIAS_ROWS, NSA_HPG * tq), 0)

    qa_ref[:NSA_DH] = qr_ref[0, 0]

    def set_bias_rows(c):
        for b in range(bpc):
            brow = bias_ref[pl.ds(c * bpc + b, 1), :]
            brow4 = jnp.concatenate([brow] * NSA_HPG, axis=1)
            lo_row = NSA_DH + BIAS_ROWS * b
            qa_ref[lo_row:lo_row + BIAS_ROWS] = jnp.where(row16 == 0, brow4, 0.0).astype(BF16)

    def sel_scores(c, h):
        kblk = ks_ref[0, pl.ds(pl.multiple_of(c * tq, tq), tq), :]
        return _dot(kblk, qa_ref[:, hcols(h)]).astype(BF16)

    reset()
    set_bias_rows(0)
    for h in heads:
        st_refs[h][...] = sel_scores(0, h)

    def sel_chunk(c):
        set_bias_rows(c + 1)
        vblk = vst_ref[0, :, pl.ds(pl.multiple_of(c * tq, tq), tq)]
        nxt = sel_scores(c + 1, 0)
        for h in heads:
            cur_scores = st_refs[h][...]
            after = sel_scores(c + 1, h + 1) if h + 1 < NSA_HPG else None
            _flash_biased(cur_scores, vblk, m_ref, acc_ref, h)
            st_refs[h][...] = nxt
            nxt = after

    def sel_group(cg, carry):
        for u in range(SEL_UNROLL):
            sel_chunk(SEL_UNROLL * cg + u)
        return carry

    lax.fori_loop(0, i // SEL_UNROLL, sel_group, 0)
    done = (i // SEL_UNROLL) * SEL_UNROLL
    part = SEL_UNROLL // 2
    while part >= 1:
        @pl.when((i & part) != 0)
        def _(base=done, part=part):
            for u in range(part):
                sel_chunk(base + u)
        done = done + (i & part)
        part //= 2

    vblk = vst_ref[0, :, pl.ds(pl.multiple_of(q0, tq), tq)]
    for h in heads:
        _flash_biased(st_refs[h][...] + causal_bias, vblk, m_ref, acc_ref, h)
    o_s = [finish(h) for h in heads]

    reset()
    nback = WINDOW // tq
    win_bias = [causal_bias] + [None] * (nback - 1) + [anti_bias]
    qw_ref[:NSA_DH] = qr_ref[0, 0]
    qw_ref[NSA_DH + BIAS_ROWS:] = jnp.zeros((NSA_DH - BIAS_ROWS, NSA_HPG * tq), BF16)

    def stash_window(j, h):
        exists = jnp.where(i >= j, 0.0, NEG)
        if h == 0:
            qw_ref[NSA_DH:NSA_DH + BIAS_ROWS] = jnp.where(row16 == 0, exists, 0.0).astype(BF16)
        kblk = kw_ref[0, pl.ds(pl.multiple_of(jnp.maximum(i - j, 0) * tq, tq), tq), :]
        sb = _dot(kblk, qw_ref[:, hcols(h)]).astype(BF16)
        st_refs[h][...] = sb if win_bias[j] is None else sb + win_bias[j]

    for h in heads:
        stash_window(0, h)
    for j in range(nback + 1):
        vblk = vwt_ref[0, :, pl.ds(pl.multiple_of(jnp.maximum(i - j, 0) * tq, tq), tq)]
        for h in heads:
            _flash_biased(st_refs[h][...], vblk, m_ref, acc_ref, h)
            if j < nback:
                stash_window(j + 1, h)
    o_w = [finish(h) for h in heads]

    def gate(branch, h):
        return gt_ref[pl.ds(branch * NSA_HEADS + g * NSA_HPG + h, 1), :]

    out_t = [gate(0, h) * oc_ref[h] + gate(1, h) * o_s[h] + gate(2, h) * o_w[h] for h in heads]
    halves = [jnp.concatenate(out_t[2 * p:2 * p + 2], axis=0).T for p in range(NSA_HPG // 2)]
    o_ref[...] = jnp.concatenate(halves, axis=1)


def _nsa(qn, qr, kc, vct, ov, ks, vst, kw, vwt, gt, tq):
    g, ni, dh, w4 = qn.shape
    s = ks.shape[1]
    ncp = kc.shape[1]
    nsel = s // SEL_BLOCK
    topn = min(SEL_TOPK, nsel)
    assert tq == SEL_CHUNK and (tq // SEL_BLOCK) * BIAS_ROWS == dh and w4 == NSA_HPG * tq
    q_spec = pl.BlockSpec((1, 1, dh, w4), lambda a, b: (a, b, 0, 0))
    return pl.pallas_call(
        functools.partial(_nsa_body, tq=tq, ncp=ncp, nsel=nsel, topn=topn),
        grid=(g, ni),
        in_specs=[
            q_spec, q_spec,
            pl.BlockSpec((1, ncp, dh), lambda a, b: (a, 0, 0)),
            pl.BlockSpec((1, V_ROWS, ncp), lambda a, b: (a, 0, 0)),
            pl.BlockSpec((nsel, ncp), lambda a, b: (0, 0)),
            pl.BlockSpec((1, s, 2 * dh), lambda a, b: (a, 0, 0)),
            pl.BlockSpec((1, V_ROWS, s), lambda a, b: (a, 0, 0)),
            pl.BlockSpec((1, s, 2 * dh), lambda a, b: (a, 0, 0)),
            pl.BlockSpec((1, V_ROWS, s), lambda a, b: (a, 0, 0)),
            pl.BlockSpec((3 * NSA_HEADS, tq), lambda a, b: (0, b)),
        ],
        out_specs=pl.BlockSpec((tq, NSA_HPG * dh), lambda a, b: (b, a)),
        out_shape=jax.ShapeDtypeStruct((s, NSA_HEADS * dh), F32),
        scratch_shapes=[
            pltpu.VMEM((nsel, tq), F32),
            pltpu.VMEM((NSA_HPG, dh, tq), F32),
            pltpu.VMEM((2 * dh, w4), BF16),
            pltpu.VMEM((2 * dh, w4), BF16),
            pltpu.VMEM((tq, tq), BF16), pltpu.VMEM((tq, tq), BF16),
            pltpu.VMEM((tq, tq), BF16), pltpu.VMEM((tq, tq), BF16),
            pltpu.VMEM((NSA_HPG, 1, tq), F32),
            pltpu.VMEM((NSA_HPG, V_ROWS, tq), F32),
        ],
        compiler_params=_cparams(("arbitrary", "arbitrary")),
        name="nsa",
    )(qn, qr, kc, vct, ov, ks, vst, kw, vwt, gt)


def _memkv_body(mem_ref, g_ref, w_ref, k_ref, v_ref):
    kv = _dot(_rms(mem_ref[...], g_ref[...]).astype(BF16), w_ref[...])
    width = MEM_HEADS * MEM_DH
    k_ref[...] = kv[:, :width].astype(BF16)
    v_ref[...] = kv[:, width:].astype(BF16)


def _memkv(mem, g, w):
    m, d = mem.shape
    width = MEM_HEADS * MEM_DH
    shp = jax.ShapeDtypeStruct((m, width), BF16)
    return pl.pallas_call(
        _memkv_body,
        out_shape=[shp, shp],
        compiler_params=pltpu.CompilerParams(vmem_limit_bytes=VMEM_LIMIT),
        name="memkv",
    )(mem, g, w)


def _mixout_body(yn_ref, bon_ref, gate_ref, ynsa_ref, h_ref, lnw_ref, lnb_ref, wo_rw_ref, wo_nsa_ref,
                 gpost_ref, mpre_ref, wq_ref, k_ref, v_ref, wom_ref, mpost_ref, o_ref):
    y_rw = ((yn_ref[...] * lnw_ref[...] + lnb_ref[...]) + bon_ref[...]) * gate_ref[...]
    y = _dot(y_rw.astype(BF16), wo_rw_ref[...]) + _dot(ynsa_ref[...].astype(BF16), wo_nsa_ref[...])
    h2 = h_ref[...] + _rms(y, gpost_ref[...])
    q = _dot(_rms(h2, mpre_ref[...]).astype(BF16), wq_ref[...])
    scale = MEM_DH ** -0.5
    outs = []
    for hh in range(MEM_HEADS):
        sl = slice(hh * MEM_DH, (hh + 1) * MEM_DH)
        s = _dot_nt(q[:, sl].astype(BF16), k_ref[:, sl]) * scale
        p = jnp.exp(s - jnp.max(s, axis=-1, keepdims=True))
        p = p / jnp.sum(p, axis=-1, keepdims=True)
        outs.append(_dot(p.astype(BF16), v_ref[:, sl]))
    o = jnp.concatenate(outs, axis=-1).astype(BF16)
    m = _dot(o, wom_ref[...])
    o_ref[...] = h2 + _rms(m, mpost_ref[...])


def _mixout(yn, bon, gate, ynsa, h, lnw, lnb, wo_rw, wo_nsa, gpost, mpre, wq, k, v, wom, mpost, tm=256):
    s, d = h.shape
    rows = lambda n: pl.BlockSpec((tm, n), lambda i: (i, 0))
    full = lambda a: pl.BlockSpec(a.shape, lambda i: (0, 0))
    return pl.pallas_call(
        _mixout_body,
        grid=(s // tm,),
        in_specs=[rows(RW_WIDTH), rows(RW_WIDTH), rows(RW_WIDTH), rows(RW_WIDTH), rows(d),
                  full(lnw), full(lnb), full(wo_rw), full(wo_nsa), full(gpost), full(mpre), full(wq),
                  full(k), full(v), full(wom), full(mpost)],
        out_specs=rows(d),
        out_shape=jax.ShapeDtypeStruct((s, d), F32),
        compiler_params=_cparams(("parallel",)),
        name="mixout",
    )(yn, bon, gate, ynsa, h, lnw, lnb, wo_rw, wo_nsa, gpost, mpre, wq, k, v, wom, mpost)


def _rope_tables(s):
    half = ROPE_DIM // 2
    inv_freq = ROPE_THETA ** (-jnp.arange(half, dtype=F32) * 2.0 / ROPE_DIM)
    ang = jnp.arange(s, dtype=jnp.int32).astype(F32)[:, None] * inv_freq[None, :]
    cos, sin = jnp.cos(ang), jnp.sin(ang)
    ones = jnp.ones((s, NSA_DH - ROPE_DIM), F32)
    zeros = jnp.zeros((s, NSA_DH - ROPE_DIM), F32)
    cos_h = jnp.concatenate([cos, cos, ones], axis=1)
    sin_h = jnp.concatenate([-sin, sin, zeros], axis=1)
    return jnp.tile(cos_h, (1, 2)), jnp.tile(sin_h, (1, 2))


def _overlap_t(s):
    ncp = s // CMP_STRIDE
    nsel = s // SEL_BLOCK
    cmp_start = jnp.arange(ncp)[None, :] * CMP_STRIDE
    sel_start = jnp.arange(nsel)[:, None] * SEL_BLOCK
    ov = (cmp_start < sel_start + SEL_BLOCK) & (cmp_start + 2 * CMP_STRIDE - 1 >= sel_start)
    ov = ov & (jnp.arange(ncp)[None, :] < ncp - 1)
    return ov.astype(BF16)


def kernel(x, mem, ffn1_pre_g, ffn1_w_gate, ffn1_w_up, ffn1_w_down, ffn1_post_g, mix_pre_g, w_in, rw_mu, rw_w0, rw_w_up, rw_a0, rw_a_up, rw_g_up, rw_k_k, rw_k_a, rw_r_k, rw_ln_w, rw_ln_b, cmp_pe_k, cmp_w1_k, cmp_w2_k, cmp_pe_v, cmp_w1_v, cmp_w2_v, w_out, mix_post_g, mem_pre_g, mem_norm_g, mem_w_q, mem_w_kv, mem_w_o, mem_post_g, ffn2_pre_g, ffn2_w_gate, ffn2_w_up, ffn2_w_down, ffn2_post_g):
    b, s, d = x.shape
    tq = SEL_CHUNK
    assert b == 1 and d == D_MODEL and s % (CMP_STRIDE * tq) == 0
    row = lambda v: v.reshape(1, -1).astype(F32)

    def ffn_weights(wg, wu, wd):
        return wg.astype(BF16), wu.astype(BF16), wd.astype(BF16)

    h = x[0]
    h = _ffn(h, row(ffn1_pre_g), *ffn_weights(ffn1_w_gate, ffn1_w_up, ffn1_w_down), row(ffn1_post_g))

    rw_cols = 3 * RW_WIDTH + RW_LORA
    nsa_main = NSA_HEADS * NSA_DH + 6 * NSA_KV
    zc = lambda n: jnp.zeros((d, n), BF16)
    w_in_b = w_in.astype(BF16)
    w_in_r = jnp.concatenate([
        w_in_b[:, :3 * RW_WIDTH],
        w_in_b[:, rw_cols:rw_cols + nsa_main],
        w_in_b[:, rw_cols + nsa_main:], zc(128 - 3 * NSA_HEADS),
        w_in_b[:, 3 * RW_WIDTH:rw_cols], zc(RW_LORA_PAD - RW_LORA),
    ], axis=1)
    feats = _inproj(h, row(mix_pre_g), w_in_r)

    mu_a = row(rw_mu[:3 * RW_WIDTH])
    mu_l = row(jnp.pad(rw_mu[3 * RW_WIDTH:], (0, RW_LORA_PAD - RW_LORA)))
    lora_w = lambda w, off: jnp.pad(w, ((off, RW_LORA_PAD - off - w.shape[0]), (0, 0))).astype(BF16)
    at, rt, bt, kt, v, pt, gate, bonus = _rwprep(
        feats, mu_a, mu_l, row(rw_w0), row(rw_a0), row(rw_k_k), row(rw_k_a), row(rw_r_k),
        lora_w(rw_w_up, 0), lora_w(rw_a_up, 64), lora_w(rw_g_up, 128))
    yn = _rwscan(at, rt, bt, kt, v, pt)

    cos_t, sin_t = _rope_tables(s)
    qn, qr, kc, vc, ks, vst, kw, vwt, gt = _nsaprep(feats, cos_t, sin_t, tq)
    k_cmp = _compress(kc, cmp_pe_k, cmp_w1_k, cmp_w2_k, transpose_out=False)
    v_cmp_t = _compress(vc, cmp_pe_v, cmp_w1_v, cmp_w2_v, transpose_out=True)
    y_nsa = _nsa(qn, qr, k_cmp, v_cmp_t, _overlap_t(s), ks, vst, kw, vwt, gt, tq)

    mem_k, mem_v = _memkv(mem[0], row(mem_norm_g), mem_w_kv.astype(BF16))
    h = _mixout(yn, bonus, gate, y_nsa, h, row(rw_ln_w), row(rw_ln_b),
                w_out[:RW_WIDTH].astype(BF16), w_out[RW_WIDTH:].astype(BF16), row(mix_post_g),
                row(mem_pre_g), mem_w_q.astype(BF16), mem_k, mem_v, mem_w_o.astype(BF16), row(mem_post_g))

    h = _ffn(h, row(ffn2_pre_g), *ffn_weights(ffn2_w_gate, ffn2_w_up, ffn2_w_down), row(ffn2_post_g))
    return h[None]
```

```python
import functools

import jax
import jax.numpy as jnp
from jax import lax
from jax.experimental import pallas as pl
from jax.experimental.pallas import tpu as pltpu

F32 = jnp.float32
BF16 = jnp.bfloat16

D_MODEL = 2048
EPS = 1e-6

RW_HEADS = 16
RW_N = 64
RW_WIDTH = 1024
RW_LORA = 288
RW_LORA_PAD = 384
RW_GN_EPS = 64e-5
RW_CHUNK = 64
RW_GROUP = 2

NSA_HEADS = 16
NSA_G = 4
NSA_HPG = 4
NSA_DH = 64
NSA_KV = 256
CMP_STRIDE = 16
SEL_BLOCK = 64
SEL_TOPK = 16
WINDOW = 512
FORCE_BONUS = 1000.0
ROPE_THETA = 500000.0
ROPE_DIM = 16

MEM_HEADS = 4
MEM_DH = 128

LOG2E = 1.4426950408889634
EXP_NEG_HALF = 0.6065306597126334
SEL_CHUNK = 256
SEL_UNROLL = 8
BIAS_ROWS = 16
V_ROWS = NSA_DH + 16

NEG = -1e30
VMEM_LIMIT = 56 * 1024 * 1024


def _cparams(sem):
    return pltpu.CompilerParams(dimension_semantics=sem, vmem_limit_bytes=VMEM_LIMIT)


def _rms(x, g):
    return x * lax.rsqrt(jnp.mean(x * x, axis=-1, keepdims=True) + EPS) * g


def _dot(a, b):
    return jnp.dot(a, b, preferred_element_type=F32)


def _dot_nt(a, b):
    return lax.dot_general(a, b, (((1,), (1,)), ((), ())), preferred_element_type=F32)


def _split2(x):
    hi = x.astype(BF16)
    lo = (x - hi.astype(F32)).astype(BF16)
    return hi, lo


def _split3(x):
    hi = x.astype(BF16)
    r1 = x - hi.astype(F32)
    mid = r1.astype(BF16)
    lo = (r1 - mid.astype(F32)).astype(BF16)
    return hi, mid, lo


def _ffn_body(h_ref, gpre_ref, wg_ref, wu_ref, wd_ref, wgt_ref, wut_ref, wdt_ref, gpost_ref, o_ref, xn_ref,
              *, nj):
    j = pl.program_id(1)

    @pl.when(j == 0)
    def _():
        xn_ref[...] = _rms(h_ref[...], gpre_ref[...]).astype(BF16)
        o_ref[...] = jnp.zeros_like(o_ref)

    xn = xn_ref[...]

    def swiglu_part(wg, wu, wd):
        g = _dot(xn, wg)
        u = _dot(xn, wu)
        return _dot((g * jax.nn.sigmoid(g) * u).astype(BF16), wd)

    o_ref[...] += swiglu_part(wg_ref[...], wu_ref[...], wd_ref[...])

    @pl.when(j == nj - 1)
    def _():
        y = o_ref[...] + swiglu_part(wgt_ref[...], wut_ref[...], wdt_ref[...])
        o_ref[...] = h_ref[...] + 0.5 * _rms(y, gpost_ref[...])


def _ffn(h, pre_g, wg, wu, wd, post_g, tm=1024, tf=256):
    s, d = h.shape
    ff = wg.shape[1]
    nj = ff // tf
    tail = ff - nj * tf
    assert 0 < tail and tail % 128 == 0
    wgt, wut, wdt = wg[:, nj * tf:], wu[:, nj * tf:], wd[nj * tf:]
    const = lambda a: pl.BlockSpec(a.shape, lambda i, j: (0, 0))
    return pl.pallas_call(
        functools.partial(_ffn_body, nj=nj),
        grid=(s // tm, nj),
        in_specs=[
            pl.BlockSpec((tm, d), lambda i, j: (i, 0)),
            const(pre_g),
            pl.BlockSpec((d, tf), lambda i, j: (0, j)),
            pl.BlockSpec((d, tf), lambda i, j: (0, j)),
            pl.BlockSpec((tf, d), lambda i, j: (j, 0)),
            const(wgt), const(wut), const(wdt),
            const(post_g),
        ],
        out_specs=pl.BlockSpec((tm, d), lambda i, j: (i, 0)),
        out_shape=jax.ShapeDtypeStruct((s, d), F32),
        scratch_shapes=[pltpu.VMEM((tm, d), BF16)],
        compiler_params=_cparams(("parallel", "arbitrary")),
        name="ffn",
    )(h, pre_g, wg, wu, wd, wgt, wut, wdt, post_g)


def _inproj_body(h_ref, g_ref, w_ref, o_ref, xn_ref):
    @pl.when(pl.program_id(1) == 0)
    def _():
        xn_ref[...] = _rms(h_ref[...], g_ref[...]).astype(BF16)

    o_ref[...] = _dot(xn_ref[...], w_ref[...])


def _inproj(h, g, w, tm=1024, tn=2048):
    s, d = h.shape
    n = w.shape[1]
    return pl.pallas_call(
        _inproj_body,
        grid=(s // tm, n // tn),
        in_specs=[
            pl.BlockSpec((tm, d), lambda i, j: (i, 0)),
            pl.BlockSpec((1, d), lambda i, j: (0, 0)),
            pl.BlockSpec((d, tn), lambda i, j: (0, j)),
        ],
        out_specs=pl.BlockSpec((tm, tn), lambda i, j: (i, j)),
        out_shape=jax.ShapeDtypeStruct((s, n), F32),
        scratch_shapes=[pltpu.VMEM((tm, d), BF16)],
        compiler_params=_cparams(("parallel", "arbitrary")),
        name="inproj",
    )(h, g, w)


def _rwprep_body(f_ref, fp_ref, lo_ref, lop_ref, mua_ref, mul_ref, w0_ref, a0_ref, kk_ref, ka_ref,
                 rk_ref, wup_ref, aup_ref, gup_ref, e_ref, tri_ref,
                 at_ref, rt_ref, bt_ref, kt_ref, v_ref, pt_ref, g_ref, bon_ref, *, tm):
    first = pl.program_id(0) == 0

    def shifted(x, prev_blk):
        prev_last = jnp.where(first, 0.0, prev_blk[7:8, :])
        rolled = pltpu.roll(x, 1, 0)
        row = lax.broadcasted_iota(jnp.int32, x.shape, 0)
        return jnp.where(row == 0, prev_last, rolled)

    f = f_ref[...]
    fs = f + mua_ref[...] * (shifted(f, fp_ref[...]) - f)
    lo = lo_ref[...]
    los = lo + mul_ref[...] * (shifted(lo, lop_ref[...]) - lo)
    r = fs[:, :RW_WIDTH]
    k = fs[:, RW_WIDTH:2 * RW_WIDTH]
    v = fs[:, 2 * RW_WIDTH:]

    wx = _dot(jnp.tanh(los).astype(BF16), wup_ref[...])
    ax = _dot(los.astype(BF16), aup_ref[...])
    g_ref[...] = _dot(jax.nn.sigmoid(los).astype(BF16), gup_ref[...])

    ld = -EXP_NEG_HALF * jax.nn.sigmoid(w0_ref[...] + wx)
    a = jax.nn.sigmoid(a0_ref[...] + ax)

    e = e_ref[...]

    def headsum(x):
        parts = []
        for c in range(RW_WIDTH // 256):
            hi, lo_ = _split2(x[:, c * 256:(c + 1) * 256])
            parts.append(_dot(hi, e) + _dot(lo_, e))
        return jnp.concatenate(parts, axis=1)

    kk0 = k * kk_ref[...]
    kk = kk0 * lax.rsqrt(jnp.maximum(headsum(kk0 * kk0), 1e-24))
    kmod = k * (1.0 + (a - 1.0) * ka_ref[...])
    bon_ref[...] = headsum(r * kmod * rk_ref[...]) * v

    tri = tri_ref[...]
    cums = []
    for c in range(tm // RW_CHUNK):
        hi, mid, lo_ = _split3(ld[c * RW_CHUNK:(c + 1) * RW_CHUNK])
        cums.append(_dot(tri, hi) + _dot(tri, mid) + _dot(tri, lo_))
    cum = jnp.concatenate(cums, axis=0)
    e_in = jnp.exp(cum)
    e_out = jnp.exp(-cum)
    at_ref[...] = (-kk * jnp.exp(cum - ld)).astype(BF16)
    rt_ref[...] = r * e_in
    bt_ref[...] = (kk * a * e_out).astype(BF16)
    kt_ref[...] = (kmod * e_out).astype(BF16)
    v_ref[...] = v.astype(BF16)
    for c in range(tm // RW_CHUNK):
        pt_ref[c] = e_in[c * RW_CHUNK + RW_CHUNK - 1:(c + 1) * RW_CHUNK, :]


def _rwprep(feats, mu_a, mu_l, w0, a0, k_k, k_a, r_k, wup, aup, gup, tm=256):
    s = feats.shape[0]
    nb8 = tm // 8
    lora_blk = (2 * 3072 - RW_LORA_PAD) // RW_LORA_PAD
    e = (lax.broadcasted_iota(jnp.int32, (256, 256), 0) // RW_N
         == lax.broadcasted_iota(jnp.int32, (256, 256), 1) // RW_N).astype(BF16)
    tri = (lax.broadcasted_iota(jnp.int32, (RW_CHUNK, RW_CHUNK), 0)
           >= lax.broadcasted_iota(jnp.int32, (RW_CHUNK, RW_CHUNK), 1)).astype(BF16)
    row = lambda n: pl.BlockSpec((1, n), lambda i: (0, 0))
    full = lambda a, b: pl.BlockSpec((a, b), lambda i: (0, 0))
    dense_shape = jax.ShapeDtypeStruct((s, RW_WIDTH), F32)
    dense_b16 = jax.ShapeDtypeStruct((s, RW_WIDTH), BF16)
    dense_spec = pl.BlockSpec((tm, RW_WIDTH), lambda i: (i, 0))
    ncb = tm // RW_CHUNK
    return pl.pallas_call(
        functools.partial(_rwprep_body, tm=tm),
        grid=(s // tm,),
        in_specs=[
            pl.BlockSpec((tm, 3072), lambda i: (i, 0)),
            pl.BlockSpec((8, 3072), lambda i: (jnp.maximum(i * nb8 - 1, 0), 0)),
            pl.BlockSpec((tm, RW_LORA_PAD), lambda i: (i, lora_blk)),
            pl.BlockSpec((8, RW_LORA_PAD), lambda i: (jnp.maximum(i * nb8 - 1, 0), lora_blk)),
            row(3072), row(RW_LORA_PAD), row(RW_WIDTH), row(RW_WIDTH), row(RW_WIDTH), row(RW_WIDTH),
            row(RW_WIDTH),
            full(RW_LORA_PAD, RW_WIDTH), full(RW_LORA_PAD, RW_WIDTH), full(RW_LORA_PAD, RW_WIDTH),
            full(256, 256), full(RW_CHUNK, RW_CHUNK),
        ],
        out_specs=[dense_spec] * 5 + [
            pl.BlockSpec((ncb, 1, RW_WIDTH), lambda i: (i, 0, 0)),
            dense_spec,
            dense_spec,
        ],
        out_shape=[dense_b16, dense_shape, dense_b16, dense_b16, dense_b16] + [
            jax.ShapeDtypeStruct((s // RW_CHUNK, 1, RW_WIDTH), F32),
            dense_shape,
            dense_shape,
        ],
        compiler_params=_cparams(("parallel",)),
        name="rwprep",
    )(feats, feats, feats, feats, mu_a, mu_l, w0, a0, k_k, k_a, r_k, wup, aup, gup, e, tri)


def _dot_tn(a, b):
    return lax.dot_general(a, b, (((0,), (0,)), ((), ())), preferred_element_type=F32)


def _rwscan_pairs_body(at_ref, rt_ref, bt_ref, kt_ref, v_ref, pt_ref, y_ref, s_ref, *, nc):
    @pl.when(pl.program_id(0) == 0)
    def _():
        s_ref[...] = jnp.zeros_like(s_ref)

    t = RW_CHUNK
    w = RW_GROUP * RW_N
    row = lax.broadcasted_iota(jnp.int32, (t, w), 0)
    col = lax.broadcasted_iota(jnp.int32, (t, w), 1)
    colh = col % RW_N
    strict = row > colh
    incl = row >= colh
    eye_pair = (row == colh).astype(F32)
    head_of_lane = col // RW_N
    r2 = lax.broadcasted_iota(jnp.int32, (w, w), 0)
    c2 = lax.broadcasted_iota(jnp.int32, (w, w), 1)
    blk_f = ((r2 // RW_N) == (c2 // RW_N)).astype(F32)
    blk_b = blk_f.astype(BF16)
    eye_w = (r2 == c2).astype(F32)

    def bd(x):
        return jnp.concatenate([x] * RW_GROUP, axis=0) * blk_b

    def run_step():
        npair = RW_HEADS // RW_GROUP
        items = [(c, p) for c in range(nc) for p in range(npair)]
        pairs = range(len(items))
        sls = [(slice(c * t, (c + 1) * t), slice(p * w, (p + 1) * w)) for c, p in items]
        pts = [pt_ref[c][:, p * w:(p + 1) * w] for c, p in items]
        bf = lambda xs: [x.astype(BF16) for x in xs]
        a_b = [at_ref[sl] for sl in sls]
        r_f = [rt_ref[sl] for sl in sls]
        b_b = [bt_ref[sl] for sl in sls]
        k_b = [kt_ref[sl] for sl in sls]
        v_b = [v_ref[sl] for sl in sls]
        gg = [_dot_nt(jnp.concatenate([a_b[p], r_f[p].astype(BF16)], axis=0),
                      jnp.concatenate([bd(b_b[p]), bd(k_b[p])], axis=0)) for p in pairs]
        a_ab = [jnp.where(strict, g[:t, :w], 0.0) for g in gg]
        a_ak = bf([jnp.where(strict, g[:t, w:], 0.0) for g in gg])
        a_rb = bf([jnp.where(incl, g[t:, :w], 0.0) for g in gg])
        a_rk = bf([jnp.where(incl, g[t:, w:], 0.0) for g in gg])
        inv = [eye_pair + x for x in a_ab]
        pk = a_ab
        for _ in range(5):
            pkb = bf(pk)
            pk = [_dot(pkb[p], bd(pkb[p])) for p in pairs]
            pkb = bf(pk)
            inv = [inv[p] + _dot(inv[p].astype(BF16), bd(pkb[p])) for p in pairs]
        inv_b = bf(inv)
        w1 = bf([_dot(a_ak[p], bd(v_b[p])) for p in pairs])
        az = [_dot(inv_b[p], jnp.concatenate([bd(a_b[p]), bd(w1[p])], axis=1)) for p in pairs]
        atp = bf([x[:, :w] for x in az])
        z0 = bf([x[:, w:] for x in az])
        ry = [_dot(a_rb[p], jnp.concatenate([bd(atp[p]), bd(z0[p])], axis=1)) for p in pairs]
        rh = bf([r_f[p] + ry[p][:, :w] for p in pairs])
        y0 = [ry[p][:, w:] + _dot(a_rk[p], bd(v_b[p])) for p in pairs]
        m_bd = bf([(eye_w + _dot_tn(atp[p], b_b[p])) * blk_f * pts[p] for p in pairs])
        c_full = [(_dot_tn(z0[p], b_b[p]) + _dot_tn(v_b[p], k_b[p])) * blk_f * pts[p] for p in pairs]
        c_pair = [sum(x[hh * RW_N:(hh + 1) * RW_N] for hh in range(RW_GROUP)) for x in c_full]

        def headmean(x):
            out = jnp.zeros_like(x)
            for hh in range(RW_GROUP):
                mine = head_of_lane == hh
                out = jnp.where(mine, jnp.sum(jnp.where(mine, x, 0.0), axis=-1, keepdims=True), out)
            return out * (1.0 / RW_N)

        state = [s_ref[p] for p in range(npair)]
        for q in pairs:
            p = items[q][1]
            st_b = state[p].astype(BF16)
            y = _dot_nt(rh[q], bd(st_b)) + y0[q]
            state[p] = _dot(st_b, m_bd[q]) + c_pair[q]
            yc = y - headmean(y)
            y_ref[sls[q]] = yc * lax.rsqrt(headmean(yc * yc) + RW_GN_EPS)
        for p in range(npair):
            s_ref[p] = state[p]

    run_step()


def _rwscan(at, rt, bt, kt, v, pt, rows=512):
    s = at.shape[0]
    nc = rows // RW_CHUNK
    dense_spec = pl.BlockSpec((rows, RW_WIDTH), lambda i: (i, 0))
    return pl.pallas_call(
        functools.partial(_rwscan_pairs_body, nc=nc),
        grid=(s // rows,),
        in_specs=[dense_spec] * 5 + [pl.BlockSpec((nc, 1, RW_WIDTH), lambda i: (i, 0, 0))],
        out_specs=dense_spec,
        out_shape=jax.ShapeDtypeStruct((s, RW_WIDTH), F32),
        scratch_shapes=[pltpu.VMEM((RW_HEADS // RW_GROUP, RW_N, RW_GROUP * RW_N), F32)],
        compiler_params=_cparams(("arbitrary",)),
        name="rwscan",
    )(at, rt, bt, kt, v, pt)


def _nsaprep_body(f_ref, cos_ref, sin_ref, qn_ref, qr_ref, kc_ref, vc_ref, ks_ref, vst_ref, kw_ref,
                  vwt_ref, gt_ref, *, tq):
    cs = cos_ref[...]
    sn = sin_ref[...]
    lane = lax.broadcasted_iota(jnp.int32, (tq, 128), 1)
    first8 = (lane % NSA_DH) < (ROPE_DIM // 2)

    def rope(x):
        swapped = jnp.where(first8, pltpu.roll(x, 128 - ROPE_DIM // 2, 1), pltpu.roll(x, ROPE_DIM // 2, 1))
        return x * cs + swapped * sn

    scale = NSA_DH ** -0.5 * LOG2E
    lane_hi = lane - NSA_DH
    blk_in_chunk = (lax.broadcasted_iota(jnp.int32, (tq, 128), 0) % SEL_CHUNK) // SEL_BLOCK
    onehot = ((lane_hi >= 0) & (lane_hi % BIAS_ROWS == 0) & (lane_hi // BIAS_ROWS == blk_in_chunk)).astype(F32)
    ones_rows = jnp.ones((V_ROWS - NSA_DH, tq), BF16)
    for p in range(NSA_HEADS // 2):
        x = f_ref[:, p * 128:(p + 1) * 128]
        xn_t = (x * scale).T
        xr_t = (rope(x) * scale).T
        for e in range(2):
            hd = 2 * p + e
            g, h = hd // NSA_HPG, hd % NSA_HPG
            qn_ref[g, 0, :, h * tq:(h + 1) * tq] = xn_t[e * NSA_DH:(e + 1) * NSA_DH].astype(BF16)
            qr_ref[g, 0, :, h * tq:(h + 1) * tq] = xr_t[e * NSA_DH:(e + 1) * NSA_DH].astype(BF16)

    def kv_piece(idx, p):
        base = NSA_HEADS * NSA_DH + idx * NSA_KV + p * 128
        return f_ref[:, base:base + 128]

    for p in range(2):
        kc = kv_piece(0, p)
        vc = kv_piece(1, p)
        ks = rope(kv_piece(2, p))
        vs_t = kv_piece(3, p).T
        kw = rope(kv_piece(4, p))
        vw_t = kv_piece(5, p).T
        for e in range(2):
            g = 2 * p + e
            sl = slice(e * NSA_DH, (e + 1) * NSA_DH)
            kc_ref[g] = kc[:, sl]
            vc_ref[g] = vc[:, sl]
            ks_low = ks if e == 0 else pltpu.roll(ks, NSA_DH, 1)
            ks_ref[g] = jnp.where(lane < NSA_DH, ks_low, onehot).astype(BF16)
            kw_low = kw if e == 0 else pltpu.roll(kw, NSA_DH, 1)
            kw_ref[g] = jnp.where(lane < NSA_DH, kw_low, jnp.where(lane == NSA_DH, 1.0, 0.0)).astype(BF16)
            vst_ref[g, :NSA_DH] = vs_t[sl].astype(BF16)
            vst_ref[g, NSA_DH:] = ones_rows
            vwt_ref[g, :NSA_DH] = vw_t[sl].astype(BF16)
            vwt_ref[g, NSA_DH:] = ones_rows
    gl = f_ref[:, 2560:2688]
    gt_ref[...] = jax.nn.sigmoid(gl).T[:3 * NSA_HEADS]


def _nsaprep(feats, cos_t, sin_t, tq):
    s = feats.shape[0]
    ni = s // tq
    kv_f32 = jax.ShapeDtypeStruct((NSA_G, s, NSA_DH), F32)
    kv_b16 = jax.ShapeDtypeStruct((NSA_G, s, NSA_DH), BF16)
    ksa_b16 = jax.ShapeDtypeStruct((NSA_G, s, 2 * NSA_DH), BF16)
    kvt_b16 = jax.ShapeDtypeStruct((NSA_G, V_ROWS, s), BF16)
    q_shape = jax.ShapeDtypeStruct((NSA_G, ni, NSA_DH, NSA_HPG * tq), BF16)
    q_spec = pl.BlockSpec((NSA_G, 1, NSA_DH, NSA_HPG * tq), lambda i: (0, i, 0, 0))
    kv_spec = pl.BlockSpec((NSA_G, tq, NSA_DH), lambda i: (0, i, 0))
    ksa_spec = pl.BlockSpec((NSA_G, tq, 2 * NSA_DH), lambda i: (0, i, 0))
    kvt_spec = pl.BlockSpec((NSA_G, V_ROWS, tq), lambda i: (0, 0, i))
    return pl.pallas_call(
        functools.partial(_nsaprep_body, tq=tq),
        grid=(ni,),
        in_specs=[
            pl.BlockSpec((tq, 3072), lambda i: (i, 1)),
            pl.BlockSpec((tq, 128), lambda i: (i, 0)),
            pl.BlockSpec((tq, 128), lambda i: (i, 0)),
        ],
        out_specs=[q_spec, q_spec, kv_spec, kv_spec, ksa_spec, kvt_spec, ksa_spec, kvt_spec,
                   pl.BlockSpec((3 * NSA_HEADS, tq), lambda i: (0, i))],
        out_shape=[q_shape, q_shape, kv_f32, kv_f32, ksa_b16, kvt_b16, ksa_b16, kvt_b16,
                   jax.ShapeDtypeStruct((3 * NSA_HEADS, s), F32)],
        compiler_params=_cparams(("parallel",)),
        name="nsaprep",
    )(feats, cos_t, sin_t)


def _gelu_tanh(x):
    return 0.5 * x * (1.0 + jnp.tanh(0.7978845608028654 * (x + 0.044715 * x * x * x)))


def _compress_body(x_ref, pe1_ref, pe2_ref, w1a_ref, w1b_ref, w2_ref, o_ref, *, ncp, transpose_out):
    x = x_ref[0]
    a = _dot((x + pe1_ref[...]).astype(BF16), w1a_ref[...])
    b = _dot((x + pe2_ref[...]).astype(BF16), w1b_ref[...])
    hid = a + pltpu.roll(b, ncp - 1, 0)
    act = _gelu_tanh(hid).astype(BF16)
    if transpose_out:
        o_ref[0, :NSA_DH] = _dot_nt(w2_ref[...], act).astype(BF16)
        o_ref[0, NSA_DH:] = jnp.ones((V_ROWS - NSA_DH, ncp), BF16)
    else:
        o_ref[0] = _dot(act, w2_ref[...]).astype(BF16)


def _compress(x, pe, w1, w2, transpose_out):
    g, s, dh = x.shape
    ncp = s // CMP_STRIDE
    half = CMP_STRIDE * dh
    xr = x.reshape(g, ncp, half)
    pe1 = pe[:CMP_STRIDE].reshape(1, half)
    pe2 = pe[CMP_STRIDE:].reshape(1, half)
    w1a = w1[:half].astype(BF16)
    w1b = w1[half:].astype(BF16)
    hid = w1.shape[1]
    w2k = (w2.T if transpose_out else w2).astype(BF16)
    full = lambda a, b: pl.BlockSpec((a, b), lambda i: (0, 0))
    if transpose_out:
        out_shape = jax.ShapeDtypeStruct((g, V_ROWS, ncp), BF16)
        out_spec = pl.BlockSpec((1, V_ROWS, ncp), lambda i: (i, 0, 0))
    else:
        out_shape = jax.ShapeDtypeStruct((g, ncp, dh), BF16)
        out_spec = pl.BlockSpec((1, ncp, dh), lambda i: (i, 0, 0))
    return pl.pallas_call(
        functools.partial(_compress_body, ncp=ncp, transpose_out=transpose_out),
        grid=(g,),
        in_specs=[
            pl.BlockSpec((1, ncp, half), lambda i: (i, 0, 0)),
            full(1, half), full(1, half), full(half, hid), full(half, hid), full(*w2k.shape),
        ],
        out_specs=out_spec,
        out_shape=out_shape,
        compiler_params=_cparams(("parallel",)),
        name="compress",
    )(xr, pe1, pe2, w1a, w1b, w2k)


def _rows_bf16(row, n):
    r16 = jnp.broadcast_to(row, (16, row.shape[1])).astype(BF16)
    return jnp.concatenate([r16] * (n // 16), axis=0)


def _flash_biased(sb, vt_blk, m_ref, acc_ref, h):
    m_old = m_ref[h]
    m_new = jnp.maximum(m_old, jnp.max(sb, axis=0, keepdims=True).astype(F32))
    alpha = jnp.exp2(m_old - m_new)
    p = jnp.exp2(sb - _rows_bf16(m_new, sb.shape[0]))
    acc_ref[h] = alpha * acc_ref[h] + _dot(vt_blk, p)
    m_ref[h] = m_new


def _nsa_body(qn_ref, qr_ref, kc_ref, vct_ref, ov_ref, ks_ref, vst_ref, kw_ref, vwt_ref, gt_ref, o_ref,
              bias_ref, oc_ref, qa_ref, qw_ref, st0_ref, st1_ref, st2_ref, st3_ref, m_ref, acc_ref,
              *, tq, ncp, nsel, topn):
    st_refs = (st0_ref, st1_ref, st2_ref, st3_ref)
    g = pl.program_id(0)
    i = pl.program_id(1)
    q0 = i * tq
    bpc = tq // SEL_BLOCK
    tpos = q0 + lax.broadcasted_iota(jnp.int32, (1, tq), 1)
    heads = range(NSA_HPG)
    hcols = lambda h: slice(h * tq, (h + 1) * tq)

    row_io = lax.broadcasted_iota(jnp.int32, (tq, 1), 0)
    col_io = lax.broadcasted_iota(jnp.int32, (1, tq), 1)
    as_bias = lambda keep: jnp.where(keep, 0.0, NEG).astype(BF16)
    causal_bias = as_bias(row_io <= col_io)
    anti_bias = as_bias(row_io > col_io)

    cmp_last = 2 * CMP_STRIDE - 1
    has_cmp = jnp.where(tpos >= cmp_last, 1.0, 0.0)

    def compressed_and_select(ncc, rows):
        nck = ncc * tq
        cbias = [as_bias((cc * tq + row_io) * CMP_STRIDE + cmp_last <= tpos) for cc in range(ncc)]

        def cmp_scores(h):
            qn_h = qn_ref[0, 0, :, hcols(h)]
            return [_dot(kc_ref[0, cc * tq:(cc + 1) * tq, :], qn_h).astype(BF16) + cbias[cc]
                    for cc in range(ncc)]

        psum = None
        ahead = cmp_scores(0)
        for h in heads:
            sbs = ahead
            if h + 1 < NSA_HPG:
                ahead = cmp_scores(h + 1)
            mc = jnp.max(sbs[0], axis=0, keepdims=True)
            for sb in sbs[1:]:
                mc = jnp.maximum(mc, jnp.max(sb, axis=0, keepdims=True))
            mrows = _rows_bf16(mc.astype(F32), tq)
            pc = jnp.concatenate([jnp.exp2(sb - mrows) for sb in sbs], axis=0)
            rv = _dot(vct_ref[0, :, :nck], pc)
            inv = has_cmp / rv[NSA_DH:NSA_DH + 1]
            oc_ref[h] = rv[:NSA_DH] * inv
            pn = pc * _rows_bf16(inv, nck)
            psum = pn if psum is None else psum + pn
        imp = _dot(ov_ref[:rows, :nck], psum)

        jj = lax.broadcasted_iota(jnp.int32, (rows, tq), 0)
        jf = jj.astype(F32)
        cur = tpos // SEL_BLOCK
        forced = (jj == 0) | (jj == cur) | (jj == cur - 1)
        taken = -(2.0 ** 127)
        score = jnp.where(forced, taken, jnp.where(jj <= cur, imp, -1.0))
        for _ in range(topn - 3):
            mx = jnp.max(score, axis=0, keepdims=True)
            first = jnp.min(jnp.where(score == mx, jf, float(rows)), axis=0, keepdims=True)
            score = jnp.where(jf == first, taken, score)
        bias_ref[:rows] = jnp.where(score == taken, 0.0, NEG)

    ncc_all = ncp // tq
    nvar = 4 if ncc_all % 4 == 0 and nsel % 4 == 0 and nsel // 4 >= 2 * topn else 1
    live_blocks = (i + 1) * bpc
    for v in range(nvar):
        rows = nsel * (v + 1) // nvar
        lo = nsel * v // nvar

        @pl.when((live_blocks > lo) & (live_blocks <= rows))
        def _(v=v, rows=rows):
            compressed_and_select(ncc_all * (v + 1) // nvar, rows)

    def reset():
        m_ref[...] = jnp.full_like(m_ref, NEG)
        acc_ref[...] = jnp.zeros_like(acc_ref)

    def finish(h):
        acc = acc_ref[h]
        l = acc[NSA_DH:NSA_DH + 1]
        return acc[:NSA_DH] * jnp.where(l > 0.0, 1.0 / l, 0.0)

    row16 = lax.broadcasted_iota(jnp.int32, (BIAS_ROWS, NSA_HPG * tq), 0)

    qa_ref[:NSA_DH] = qr_ref[0, 0]

    def set_bias_rows(c):
        for b in range(bpc):
            brow = bias_ref[pl.ds(c * bpc + b, 1), :]
            brow4 = jnp.concatenate([brow] * NSA_HPG, axis=1)
            lo_row = NSA_DH + BIAS_ROWS * b
            qa_ref[lo_row:lo_row + BIAS_ROWS] = jnp.where(row16 == 0, brow4, 0.0).astype(BF16)

    def sel_scores(c, h):
        kblk = ks_ref[0, pl.ds(pl.multiple_of(c * tq, tq), tq), :]
        return _dot(kblk, qa_ref[:, hcols(h)]).astype(BF16)

    reset()
    set_bias_rows(0)
    for h in heads:
        st_refs[h][...] = sel_scores(0, h)

    def sel_chunk(c):
        set_bias_rows(c + 1)
        vblk = vst_ref[0, :, pl.ds(pl.multiple_of(c * tq, tq), tq)]
        nxt = sel_scores(c + 1, 0)
        for h in heads:
            cur_scores = st_refs[h][...]
            after = sel_scores(c + 1, h + 1) if h + 1 < NSA_HPG else None
            _flash_biased(cur_scores, vblk, m_ref, acc_ref, h)
            st_refs[h][...] = nxt
            nxt = after

    def sel_group(cg, carry):
        for u in range(SEL_UNROLL):
            sel_chunk(SEL_UNROLL * cg + u)
        return carry

    lax.fori_loop(0, i // SEL_UNROLL, sel_group, 0)
    done = (i // SEL_UNROLL) * SEL_UNROLL
    part = SEL_UNROLL // 2
    while part >= 1:
        @pl.when((i & part) != 0)
        def _(base=done, part=part):
            for u in range(part):
                sel_chunk(base + u)
        done = done + (i & part)
        part //= 2

    vblk = vst_ref[0, :, pl.ds(pl.multiple_of(q0, tq), tq)]
    for h in heads:
        _flash_biased(st_refs[h][...] + causal_bias, vblk, m_ref, acc_ref, h)
    o_s = [finish(h) for h in heads]

    reset()
    nback = WINDOW // tq
    win_bias = [causal_bias] + [None] * (nback - 1) + [anti_bias]
    qw_ref[:NSA_DH] = qr_ref[0, 0]
    qw_ref[NSA_DH + BIAS_ROWS:] = jnp.zeros((NSA_DH - BIAS_ROWS, NSA_HPG * tq), BF16)

    def stash_window(j, h):
        exists = jnp.where(i >= j, 0.0, NEG)
        if h == 0:
            qw_ref[NSA_DH:NSA_DH + BIAS_ROWS] = jnp.where(row16 == 0, exists, 0.0).astype(BF16)
        kblk = kw_ref[0, pl.ds(pl.multiple_of(jnp.maximum(i - j, 0) * tq, tq), tq), :]
        sb = _dot(kblk, qw_ref[:, hcols(h)]).astype(BF16)
        st_refs[h][...] = sb if win_bias[j] is None else sb + win_bias[j]

    for h in heads:
        stash_window(0, h)
    for j in range(nback + 1):
        vblk = vwt_ref[0, :, pl.ds(pl.multiple_of(jnp.maximum(i - j, 0) * tq, tq), tq)]
        for h in heads:
            _flash_biased(st_refs[h][...], vblk, m_ref, acc_ref, h)
            if j < nback:
                stash_window(j + 1, h)
    o_w = [finish(h) for h in heads]

    def gate(branch, h):
        return gt_ref[pl.ds(branch * NSA_HEADS + g * NSA_HPG + h, 1), :]

    out_t = [gate(0, h) * oc_ref[h] + gate(1, h) * o_s[h] + gate(2, h) * o_w[h] for h in heads]
    halves = [jnp.concatenate(out_t[2 * p:2 * p + 2], axis=0).T for p in range(NSA_HPG // 2)]
    o_ref[...] = jnp.concatenate(halves, axis=1)


def _nsa(qn, qr, kc, vct, ov, ks, vst, kw, vwt, gt, tq):
    g, ni, dh, w4 = qn.shape
    s = ks.shape[1]
    ncp = kc.shape[1]
    nsel = s // SEL_BLOCK
    topn = min(SEL_TOPK, nsel)
    assert tq == SEL_CHUNK and (tq // SEL_BLOCK) * BIAS_ROWS == dh and w4 == NSA_HPG * tq
    q_spec = pl.BlockSpec((1, 1, dh, w4), lambda a, b: (a, b, 0, 0))
    return pl.pallas_call(
        functools.partial(_nsa_body, tq=tq, ncp=ncp, nsel=nsel, topn=topn),
        grid=(g, ni),
        in_specs=[
            q_spec, q_spec,
            pl.BlockSpec((1, ncp, dh), lambda a, b: (a, 0, 0)),
            pl.BlockSpec((1, V_ROWS, ncp), lambda a, b: (a, 0, 0)),
            pl.BlockSpec((nsel, ncp), lambda a, b: (0, 0)),
            pl.BlockSpec((1, s, 2 * dh), lambda a, b: (a, 0, 0)),
            pl.BlockSpec((1, V_ROWS, s), lambda a, b: (a, 0, 0)),
            pl.BlockSpec((1, s, 2 * dh), lambda a, b: (a, 0, 0)),
            pl.BlockSpec((1, V_ROWS, s), lambda a, b: (a, 0, 0)),
            pl.BlockSpec((3 * NSA_HEADS, tq), lambda a, b: (0, b)),
        ],
        out_specs=pl.BlockSpec((tq, NSA_HPG * dh), lambda a, b: (b, a)),
        out_shape=jax.ShapeDtypeStruct((s, NSA_HEADS * dh), F32),
        scratch_shapes=[
            pltpu.VMEM((nsel, tq), F32),
            pltpu.VMEM((NSA_HPG, dh, tq), F32),
            pltpu.VMEM((2 * dh, w4), BF16),
            pltpu.VMEM((2 * dh, w4), BF16),
            pltpu.VMEM((tq, tq), BF16), pltpu.VMEM((tq, tq), BF16),
            pltpu.VMEM((tq, tq), BF16), pltpu.VMEM((tq, tq), BF16),
            pltpu.VMEM((NSA_HPG, 1, tq), F32),
            pltpu.VMEM((NSA_HPG, V_ROWS, tq), F32),
        ],
        compiler_params=_cparams(("arbitrary", "arbitrary")),
        name="nsa",
    )(qn, qr, kc, vct, ov, ks, vst, kw, vwt, gt)


def _memkv_body(mem_ref, g_ref, w_ref, k_ref, v_ref):
    kv = _dot(_rms(mem_ref[...], g_ref[...]).astype(BF16), w_ref[...])
    width = MEM_HEADS * MEM_DH
    k_ref[...] = kv[:, :width].astype(BF16)
    v_ref[...] = kv[:, width:].astype(BF16)


def _memkv(mem, g, w):
    m, d = mem.shape
    width = MEM_HEADS * MEM_DH
    shp = jax.ShapeDtypeStruct((m, width), BF16)
    return pl.pallas_call(
        _memkv_body,
        out_shape=[shp, shp],
        compiler_params=pltpu.CompilerParams(vmem_limit_bytes=VMEM_LIMIT),
        name="memkv",
    )(mem, g, w)


def _mixout_body(yn_ref, bon_ref, gate_ref, ynsa_ref, h_ref, lnw_ref, lnb_ref, wo_rw_ref, wo_nsa_ref,
                 gpost_ref, mpre_ref, wq_ref, k_ref, v_ref, wom_ref, mpost_ref, o_ref):
    y_rw = ((yn_ref[...] * lnw_ref[...] + lnb_ref[...]) + bon_ref[...]) * gate_ref[...]
    y = _dot(y_rw.astype(BF16), wo_rw_ref[...]) + _dot(ynsa_ref[...].astype(BF16), wo_nsa_ref[...])
    h2 = h_ref[...] + _rms(y, gpost_ref[...])
    q = _dot(_rms(h2, mpre_ref[...]).astype(BF16), wq_ref[...])
    scale = MEM_DH ** -0.5
    outs = []
    for hh in range(MEM_HEADS):
        sl = slice(hh * MEM_DH, (hh + 1) * MEM_DH)
        s = _dot_nt(q[:, sl].astype(BF16), k_ref[:, sl]) * scale
        p = jnp.exp(s - jnp.max(s, axis=-1, keepdims=True))
        p = p / jnp.sum(p, axis=-1, keepdims=True)
        outs.append(_dot(p.astype(BF16), v_ref[:, sl]))
    o = jnp.concatenate(outs, axis=-1).astype(BF16)
    m = _dot(o, wom_ref[...])
    o_ref[...] = h2 + _rms(m, mpost_ref[...])


def _mixout(yn, bon, gate, ynsa, h, lnw, lnb, wo_rw, wo_nsa, gpost, mpre, wq, k, v, wom, mpost, tm=256):
    s, d = h.shape
    rows = lambda n: pl.BlockSpec((tm, n), lambda i: (i, 0))
    full = lambda a: pl.BlockSpec(a.shape, lambda i: (0, 0))
    return pl.pallas_call(
        _mixout_body,
        grid=(s // tm,),
        in_specs=[rows(RW_WIDTH), rows(RW_WIDTH), rows(RW_WIDTH), rows(RW_WIDTH), rows(d),
                  full(lnw), full(lnb), full(wo_rw), full(wo_nsa), full(gpost), full(mpre), full(wq),
                  full(k), full(v), full(wom), full(mpost)],
        out_specs=rows(d),
        out_shape=jax.ShapeDtypeStruct((s, d), F32),
        compiler_params=_cparams(("parallel",)),
        name="mixout",
    )(yn, bon, gate, ynsa, h, lnw, lnb, wo_rw, wo_nsa, gpost, mpre, wq, k, v, wom, mpost)


def _rope_tables(s):
    half = ROPE_DIM // 2
    inv_freq = ROPE_THETA ** (-jnp.arange(half, dtype=F32) * 2.0 / ROPE_DIM)
    ang = jnp.arange(s, dtype=jnp.int32).astype(F32)[:, None] * inv_freq[None, :]
    cos, sin = jnp.cos(ang), jnp.sin(ang)
    ones = jnp.ones((s, NSA_DH - ROPE_DIM), F32)
    zeros = jnp.zeros((s, NSA_DH - ROPE_DIM), F32)
    cos_h = jnp.concatenate([cos, cos, ones], axis=1)
    sin_h = jnp.concatenate([-sin, sin, zeros], axis=1)
    return jnp.tile(cos_h, (1, 2)), jnp.tile(sin_h, (1, 2))


def _overlap_t(s):
    ncp = s // CMP_STRIDE
    nsel = s // SEL_BLOCK
    cmp_start = jnp.arange(ncp)[None, :] * CMP_STRIDE
    sel_start = jnp.arange(nsel)[:, None] * SEL_BLOCK
    ov = (cmp_start < sel_start + SEL_BLOCK) & (cmp_start + 2 * CMP_STRIDE - 1 >= sel_start)
    ov = ov & (jnp.arange(ncp)[None, :] < ncp - 1)
    return ov.astype(BF16)


def kernel(x, mem, ffn1_pre_g, ffn1_w_gate, ffn1_w_up, ffn1_w_down, ffn1_post_g, mix_pre_g, w_in, rw_mu, rw_w0, rw_w_up, rw_a0, rw_a_up, rw_g_up, rw_k_k, rw_k_a, rw_r_k, rw_ln_w, rw_ln_b, cmp_pe_k, cmp_w1_k, cmp_w2_k, cmp_pe_v, cmp_w1_v, cmp_w2_v, w_out, mix_post_g, mem_pre_g, mem_norm_g, mem_w_q, mem_w_kv, mem_w_o, mem_post_g, ffn2_pre_g, ffn2_w_gate, ffn2_w_up, ffn2_w_down, ffn2_post_g):
    b, s, d = x.shape
    tq = SEL_CHUNK
    assert b == 1 and d == D_MODEL and s % (CMP_STRIDE * tq) == 0
    row = lambda v: v.reshape(1, -1).astype(F32)

    def ffn_weights(wg, wu, wd):
        return wg.astype(BF16), wu.astype(BF16), wd.astype(BF16)

    h = x[0]
    h = _ffn(h, row(ffn1_pre_g), *ffn_weights(ffn1_w_gate, ffn1_w_up, ffn1_w_down), row(ffn1_post_g))

    rw_cols = 3 * RW_WIDTH + RW_LORA
    nsa_main = NSA_HEADS * NSA_DH + 6 * NSA_KV
    zc = lambda n: jnp.zeros((d, n), BF16)
    w_in_b = w_in.astype(BF16)
    w_in_r = jnp.concatenate([
        w_in_b[:, :3 * RW_WIDTH],
        w_in_b[:, rw_cols:rw_cols + nsa_main],
        w_in_b[:, rw_cols + nsa_main:], zc(128 - 3 * NSA_HEADS),
        w_in_b[:, 3 * RW_WIDTH:rw_cols], zc(RW_LORA_PAD - RW_LORA),
    ], axis=1)
    feats = _inproj(h, row(mix_pre_g), w_in_r)

    mu_a = row(rw_mu[:3 * RW_WIDTH])
    mu_l = row(jnp.pad(rw_mu[3 * RW_WIDTH:], (0, RW_LORA_PAD - RW_LORA)))
    lora_w = lambda w, off: jnp.pad(w, ((off, RW_LORA_PAD - off - w.shape[0]), (0, 0))).astype(BF16)
    at, rt, bt, kt, v, pt, gate, bonus = _rwprep(
        feats, mu_a, mu_l, row(rw_w0), row(rw_a0), row(rw_k_k), row(rw_k_a), row(rw_r_k),
        lora_w(rw_w_up, 0), lora_w(rw_a_up, 64), lora_w(rw_g_up, 128))
    yn = _rwscan(at, rt, bt, kt, v, pt)

    cos_t, sin_t = _rope_tables(s)
    qn, qr, kc, vc, ks, vst, kw, vwt, gt = _nsaprep(feats, cos_t, sin_t, tq)
    k_cmp = _compress(kc, cmp_pe_k, cmp_w1_k, cmp_w2_k, transpose_out=False)
    v_cmp_t = _compress(vc, cmp_pe_v, cmp_w1_v, cmp_w2_v, transpose_out=True)
    y_nsa = _nsa(qn, qr, k_cmp, v_cmp_t, _overlap_t(s), ks, vst, kw, vwt, gt, tq)

    mem_k, mem_v = _memkv(mem[0], row(mem_norm_g), mem_w_kv.astype(BF16))
    h = _mixout(yn, bonus, gate, y_nsa, h, row(rw_ln_w), row(rw_ln_b),
                w_out[:RW_WIDTH].astype(BF16), w_out[RW_WIDTH:].astype(BF16), row(mix_post_g),
                row(mem_pre_g), mem_w_q.astype(BF16), mem_k, mem_v, mem_w_o.astype(BF16), row(mem_post_g))

    h = _ffn(h, row(ffn2_pre_g), *ffn_weights(ffn2_w_gate, ffn2_w_up, ffn2_w_down), row(ffn2_post_g))
    return h[None]
```

```python
import functools

import jax
import jax.numpy as jnp
from jax import lax
from jax.experimental import pallas as pl
from jax.experimental.pallas import tpu as pltpu

F32 = jnp.float32
BF16 = jnp.bfloat16

D_MODEL = 2048
EPS = 1e-6
FFN_SLABS = 4

RW_HEADS = 16
RW_N = 64
RW_WIDTH = 1024
RW_LORA = 288
RW_LORA_PAD = 384
RW_GN_EPS = 64e-5
RW_CHUNK = 64
RW_GROUP = 2

NSA_HEADS = 16
NSA_G = 4
NSA_HPG = 4
NSA_DH = 64
NSA_KV = 256
CMP_STRIDE = 16
SEL_BLOCK = 64
SEL_TOPK = 16
WINDOW = 512
FORCE_BONUS = 1000.0
ROPE_THETA = 500000.0
ROPE_DIM = 16

MEM_HEADS = 4
MEM_DH = 128

LOG2E = 1.4426950408889634
EXP_NEG_HALF = 0.6065306597126334
SEL_CHUNK = 256
SEL_UNROLL = 8
BIAS_ROWS = 16
V_ROWS = NSA_DH + 16

NEG = -1e30
VMEM_LIMIT = 56 * 1024 * 1024


def _cparams(sem):
    return pltpu.CompilerParams(dimension_semantics=sem, vmem_limit_bytes=VMEM_LIMIT)


def _rms(x, g):
    return x * lax.rsqrt(jnp.mean(x * x, axis=-1, keepdims=True) + EPS) * g


def _dot(a, b):
    return jnp.dot(a, b, preferred_element_type=F32)


def _dot_nt(a, b):
    return lax.dot_general(a, b, (((1,), (1,)), ((), ())), preferred_element_type=F32)


def _split2(x):
    hi = x.astype(BF16)
    lo = (x - hi.astype(F32)).astype(BF16)
    return hi, lo


def _split3(x):
    hi = x.astype(BF16)
    r1 = x - hi.astype(F32)
    mid = r1.astype(BF16)
    lo = (r1 - mid.astype(F32)).astype(BF16)
    return hi, mid, lo


def _ffn_body(h_ref, gpre_ref, wg_ref, wu_ref, wd_ref, wgt_ref, wut_ref, wdt_ref, gpost_ref, o_ref, xn_ref,
              *, nj):
    j = pl.program_id(1)
    slab = h_ref.shape[0] // FFN_SLABS
    slabs = [slice(k * slab, (k + 1) * slab) for k in range(FFN_SLABS)]

    def swiglu_part(xn, wg, wu, wd):
        g = _dot(xn, wg)
        u = _dot(xn, wu)
        return _dot((g * jax.nn.sigmoid(g) * u).astype(BF16), wd)

    @pl.when(j == 0)
    def _():
        for rows in slabs:
            xn = _rms(h_ref[rows], gpre_ref[...]).astype(BF16)
            xn_ref[rows] = xn
            o_ref[rows] = swiglu_part(xn, wg_ref[...], wu_ref[...], wd_ref[...])

    @pl.when(j > 0)
    def _():
        o_ref[...] += swiglu_part(xn_ref[...], wg_ref[...], wu_ref[...], wd_ref[...])

    @pl.when(j == nj - 1)
    def _():
        for rows in slabs:
            y = o_ref[rows] + swiglu_part(xn_ref[rows], wgt_ref[...], wut_ref[...], wdt_ref[...])
            o_ref[rows] = h_ref[rows] + 0.5 * _rms(y, gpost_ref[...])


def _ffn(h, pre_g, wg, wu, wd, post_g, tm=1024, tf=256):
    s, d = h.shape
    ff = wg.shape[1]
    nj = ff // tf
    tail = ff - nj * tf
    assert 0 < tail and tail % 128 == 0
    wgt, wut, wdt = wg[:, nj * tf:], wu[:, nj * tf:], wd[nj * tf:]
    const = lambda a: pl.BlockSpec(a.shape, lambda i, j: (0, 0))
    return pl.pallas_call(
        functools.partial(_ffn_body, nj=nj),
        grid=(s // tm, nj),
        in_specs=[
            pl.BlockSpec((tm, d), lambda i, j: (i, 0)),
            const(pre_g),
            pl.BlockSpec((d, tf), lambda i, j: (0, j)),
            pl.BlockSpec((d, tf), lambda i, j: (0, j)),
            pl.BlockSpec((tf, d), lambda i, j: (j, 0)),
            const(wgt), const(wut), const(wdt),
            const(post_g),
        ],
        out_specs=pl.BlockSpec((tm, d), lambda i, j: (i, 0)),
        out_shape=jax.ShapeDtypeStruct((s, d), F32),
        scratch_shapes=[pltpu.VMEM((tm, d), BF16)],
        compiler_params=_cparams(("parallel", "arbitrary")),
        name="ffn",
    )(h, pre_g, wg, wu, wd, wgt, wut, wdt, post_g)


def _inproj_body(h_ref, g_ref, w_ref, o_ref, xn_ref):
    @pl.when(pl.program_id(1) == 0)
    def _():
        xn_ref[...] = _rms(h_ref[...], g_ref[...]).astype(BF16)

    o_ref[...] = _dot(xn_ref[...], w_ref[...])


def _inproj(h, g, w, tm=1024, tn=2048):
    s, d = h.shape
    n = w.shape[1]
    return pl.pallas_call(
        _inproj_body,
        grid=(s // tm, n // tn),
        in_specs=[
            pl.BlockSpec((tm, d), lambda i, j: (i, 0)),
            pl.BlockSpec((1, d), lambda i, j: (0, 0)),
            pl.BlockSpec((d, tn), lambda i, j: (0, j)),
        ],
        out_specs=pl.BlockSpec((tm, tn), lambda i, j: (i, j)),
        out_shape=jax.ShapeDtypeStruct((s, n), F32),
        scratch_shapes=[pltpu.VMEM((tm, d), BF16)],
        compiler_params=_cparams(("parallel", "arbitrary")),
        name="inproj",
    )(h, g, w)


def _rwprep_body(f_ref, fp_ref, lo_ref, lop_ref, mua_ref, mul_ref, w0_ref, a0_ref, kk_ref, ka_ref,
                 rk_ref, wup_ref, aup_ref, gup_ref, e_ref, tri_ref,
                 at_ref, rt_ref, bt_ref, kt_ref, v_ref, pt_ref, g_ref, bon_ref, *, tm):
    first = pl.program_id(0) == 0

    def shifted(x, prev_blk):
        prev_last = jnp.where(first, 0.0, prev_blk[7:8, :])
        rolled = pltpu.roll(x, 1, 0)
        row = lax.broadcasted_iota(jnp.int32, x.shape, 0)
        return jnp.where(row == 0, prev_last, rolled)

    f = f_ref[...]
    fs = f + mua_ref[...] * (shifted(f, fp_ref[...]) - f)
    lo = lo_ref[...]
    los = lo + mul_ref[...] * (shifted(lo, lop_ref[...]) - lo)
    r = fs[:, :RW_WIDTH]
    k = fs[:, RW_WIDTH:2 * RW_WIDTH]
    v = fs[:, 2 * RW_WIDTH:]

    wx = _dot(jnp.tanh(los).astype(BF16), wup_ref[...])
    ax = _dot(los.astype(BF16), aup_ref[...])
    g_ref[...] = _dot(jax.nn.sigmoid(los).astype(BF16), gup_ref[...])

    ld = -EXP_NEG_HALF * jax.nn.sigmoid(w0_ref[...] + wx)
    a = jax.nn.sigmoid(a0_ref[...] + ax)

    e = e_ref[...]

    def headsum(x):
        parts = []
        for c in range(RW_WIDTH // 256):
            hi, lo_ = _split2(x[:, c * 256:(c + 1) * 256])
            parts.append(_dot(hi, e) + _dot(lo_, e))
        return jnp.concatenate(parts, axis=1)

    kk0 = k * kk_ref[...]
    kk = kk0 * lax.rsqrt(jnp.maximum(headsum(kk0 * kk0), 1e-24))
    kmod = k * (1.0 + (a - 1.0) * ka_ref[...])
    bon_ref[...] = headsum(r * kmod * rk_ref[...]) * v

    tri = tri_ref[...]
    cums = []
    for c in range(tm // RW_CHUNK):
        hi, mid, lo_ = _split3(ld[c * RW_CHUNK:(c + 1) * RW_CHUNK])
        cums.append(_dot(tri, hi) + _dot(tri, mid) + _dot(tri, lo_))
    cum = jnp.concatenate(cums, axis=0)
    e_in = jnp.exp(cum)
    e_out = jnp.exp(-cum)
    at_ref[...] = (-kk * jnp.exp(cum - ld)).astype(BF16)
    rt_ref[...] = r * e_in
    bt_ref[...] = (kk * a * e_out).astype(BF16)
    kt_ref[...] = (kmod * e_out).astype(BF16)
    v_ref[...] = v.astype(BF16)
    for c in range(tm // RW_CHUNK):
        pt_ref[c] = e_in[c * RW_CHUNK + RW_CHUNK - 1:(c + 1) * RW_CHUNK, :]


def _rwprep(feats, mu_a, mu_l, w0, a0, k_k, k_a, r_k, wup, aup, gup, tm=256):
    s = feats.shape[0]
    nb8 = tm // 8
    lora_blk = (2 * 3072 - RW_LORA_PAD) // RW_LORA_PAD
    e = (lax.broadcasted_iota(jnp.int32, (256, 256), 0) // RW_N
         == lax.broadcasted_iota(jnp.int32, (256, 256), 1) // RW_N).astype(BF16)
    tri = (lax.broadcasted_iota(jnp.int32, (RW_CHUNK, RW_CHUNK), 0)
           >= lax.broadcasted_iota(jnp.int32, (RW_CHUNK, RW_CHUNK), 1)).astype(BF16)
    row = lambda n: pl.BlockSpec((1, n), lambda i: (0, 0))
    full = lambda a, b: pl.BlockSpec((a, b), lambda i: (0, 0))
    dense_shape = jax.ShapeDtypeStruct((s, RW_WIDTH), F32)
    dense_b16 = jax.ShapeDtypeStruct((s, RW_WIDTH), BF16)
    dense_spec = pl.BlockSpec((tm, RW_WIDTH), lambda i: (i, 0))
    ncb = tm // RW_CHUNK
    return pl.pallas_call(
        functools.partial(_rwprep_body, tm=tm),
        grid=(s // tm,),
        in_specs=[
            pl.BlockSpec((tm, 3072), lambda i: (i, 0)),
            pl.BlockSpec((8, 3072), lambda i: (jnp.maximum(i * nb8 - 1, 0), 0)),
            pl.BlockSpec((tm, RW_LORA_PAD), lambda i: (i, lora_blk)),
            pl.BlockSpec((8, RW_LORA_PAD), lambda i: (jnp.maximum(i * nb8 - 1, 0), lora_blk)),
            row(3072), row(RW_LORA_PAD), row(RW_WIDTH), row(RW_WIDTH), row(RW_WIDTH), row(RW_WIDTH),
            row(RW_WIDTH),
            full(RW_LORA_PAD, RW_WIDTH), full(RW_LORA_PAD, RW_WIDTH), full(RW_LORA_PAD, RW_WIDTH),
            full(256, 256), full(RW_CHUNK, RW_CHUNK),
        ],
        out_specs=[dense_spec] * 5 + [
            pl.BlockSpec((ncb, 1, RW_WIDTH), lambda i: (i, 0, 0)),
            dense_spec,
            dense_spec,
        ],
        out_shape=[dense_b16, dense_shape, dense_b16, dense_b16, dense_b16] + [
            jax.ShapeDtypeStruct((s // RW_CHUNK, 1, RW_WIDTH), F32),
            dense_shape,
            dense_shape,
        ],
        compiler_params=_cparams(("parallel",)),
        name="rwprep",
    )(feats, feats, feats, feats, mu_a, mu_l, w0, a0, k_k, k_a, r_k, wup, aup, gup, e, tri)


def _dot_tn(a, b):
    return lax.dot_general(a, b, (((0,), (0,)), ((), ())), preferred_element_type=F32)


def _rwscan_pairs_body(at_ref, rt_ref, bt_ref, kt_ref, v_ref, pt_ref, y_ref, s_ref, *, nc):
    @pl.when(pl.program_id(0) == 0)
    def _():
        s_ref[...] = jnp.zeros_like(s_ref)

    t = RW_CHUNK
    w = RW_GROUP * RW_N
    row = lax.broadcasted_iota(jnp.int32, (t, w), 0)
    col = lax.broadcasted_iota(jnp.int32, (t, w), 1)
    colh = col % RW_N
    strict = row > colh
    incl = row >= colh
    eye_pair = (row == colh).astype(F32)
    head_of_lane = col // RW_N
    r2 = lax.broadcasted_iota(jnp.int32, (w, w), 0)
    c2 = lax.broadcasted_iota(jnp.int32, (w, w), 1)
    blk_f = ((r2 // RW_N) == (c2 // RW_N)).astype(F32)
    blk_b = blk_f.astype(BF16)
    eye_w = (r2 == c2).astype(F32)

    def bd(x):
        return jnp.concatenate([x] * RW_GROUP, axis=0) * blk_b

    def run_step():
        npair = RW_HEADS // RW_GROUP
        items = [(c, p) for c in range(nc) for p in range(npair)]
        pairs = range(len(items))
        sls = [(slice(c * t, (c + 1) * t), slice(p * w, (p + 1) * w)) for c, p in items]
        pts = [pt_ref[c][:, p * w:(p + 1) * w] for c, p in items]
        bf = lambda xs: [x.astype(BF16) for x in xs]
        a_b = [at_ref[sl] for sl in sls]
        r_f = [rt_ref[sl] for sl in sls]
        b_b = [bt_ref[sl] for sl in sls]
        k_b = [kt_ref[sl] for sl in sls]
        v_b = [v_ref[sl] for sl in sls]
        gg = [_dot_nt(jnp.concatenate([a_b[p], r_f[p].astype(BF16)], axis=0),
                      jnp.concatenate([bd(b_b[p]), bd(k_b[p])], axis=0)) for p in pairs]
        a_ab = [jnp.where(strict, g[:t, :w], 0.0) for g in gg]
        a_ak = bf([jnp.where(strict, g[:t, w:], 0.0) for g in gg])
        a_rb = bf([jnp.where(incl, g[t:, :w], 0.0) for g in gg])
        a_rk = bf([jnp.where(incl, g[t:, w:], 0.0) for g in gg])
        inv = [eye_pair + x for x in a_ab]
        pk = a_ab
        for _ in range(5):
            pkb = bf(pk)
            pk = [_dot(pkb[p], bd(pkb[p])) for p in pairs]
            pkb = bf(pk)
            inv = [inv[p] + _dot(inv[p].astype(BF16), bd(pkb[p])) for p in pairs]
        inv_b = bf(inv)
        w1 = bf([_dot(a_ak[p], bd(v_b[p])) for p in pairs])
        az = [_dot(inv_b[p], jnp.concatenate([bd(a_b[p]), bd(w1[p])], axis=1)) for p in pairs]
        atp = bf([x[:, :w] for x in az])
        z0 = bf([x[:, w:] for x in az])
        ry = [_dot(a_rb[p], jnp.concatenate([bd(atp[p]), bd(z0[p])], axis=1)) for p in pairs]
        rh = bf([r_f[p] + ry[p][:, :w] for p in pairs])
        y0 = [ry[p][:, w:] + _dot(a_rk[p], bd(v_b[p])) for p in pairs]
        m_bd = bf([(eye_w + _dot_tn(atp[p], b_b[p])) * blk_f * pts[p] for p in pairs])
        c_full = [(_dot_tn(z0[p], b_b[p]) + _dot_tn(v_b[p], k_b[p])) * blk_f * pts[p] for p in pairs]
        c_pair = [sum(x[hh * RW_N:(hh + 1) * RW_N] for hh in range(RW_GROUP)) for x in c_full]

        def headmean(x):
            out = jnp.zeros_like(x)
            for hh in range(RW_GROUP):
                mine = head_of_lane == hh
                out = jnp.where(mine, jnp.sum(jnp.where(mine, x, 0.0), axis=-1, keepdims=True), out)
            return out * (1.0 / RW_N)

        state = [s_ref[p] for p in range(npair)]
        for q in pairs:
            p = items[q][1]
            st_b = state[p].astype(BF16)
            y = _dot_nt(rh[q], bd(st_b)) + y0[q]
            state[p] = _dot(st_b, m_bd[q]) + c_pair[q]
            yc = y - headmean(y)
            y_ref[sls[q]] = yc * lax.rsqrt(headmean(yc * yc) + RW_GN_EPS)
        for p in range(npair):
            s_ref[p] = state[p]

    run_step()


def _rwscan(at, rt, bt, kt, v, pt, rows=512):
    s = at.shape[0]
    nc = rows // RW_CHUNK
    dense_spec = pl.BlockSpec((rows, RW_WIDTH), lambda i: (i, 0))
    return pl.pallas_call(
        functools.partial(_rwscan_pairs_body, nc=nc),
        grid=(s // rows,),
        in_specs=[dense_spec] * 5 + [pl.BlockSpec((nc, 1, RW_WIDTH), lambda i: (i, 0, 0))],
        out_specs=dense_spec,
        out_shape=jax.ShapeDtypeStruct((s, RW_WIDTH), F32),
        scratch_shapes=[pltpu.VMEM((RW_HEADS // RW_GROUP, RW_N, RW_GROUP * RW_N), F32)],
        compiler_params=_cparams(("arbitrary",)),
        name="rwscan",
    )(at, rt, bt, kt, v, pt)


def _nsaprep_body(f_ref, cos_ref, sin_ref, qn_ref, qr_ref, kc_ref, vc_ref, ks_ref, vst_ref, kw_ref,
                  vwt_ref, gt_ref, *, tq):
    cs = cos_ref[...]
    sn = sin_ref[...]
    lane = lax.broadcasted_iota(jnp.int32, (tq, 128), 1)
    first8 = (lane % NSA_DH) < (ROPE_DIM // 2)

    def rope(x):
        swapped = jnp.where(first8, pltpu.roll(x, 128 - ROPE_DIM // 2, 1), pltpu.roll(x, ROPE_DIM // 2, 1))
        return x * cs + swapped * sn

    scale = NSA_DH ** -0.5 * LOG2E
    lane_hi = lane - NSA_DH
    blk_in_chunk = (lax.broadcasted_iota(jnp.int32, (tq, 128), 0) % SEL_CHUNK) // SEL_BLOCK
    onehot = ((lane_hi >= 0) & (lane_hi % BIAS_ROWS == 0) & (lane_hi // BIAS_ROWS == blk_in_chunk)).astype(F32)
    ones_rows = jnp.ones((V_ROWS - NSA_DH, tq), BF16)
    eye = (lax.broadcasted_iota(jnp.int32, (128, 128), 0)
           == lax.broadcasted_iota(jnp.int32, (128, 128), 1)).astype(BF16)

    def transposed_bf16(x):
        return _dot_nt(eye, x.astype(BF16)).astype(BF16)

    for p in range(NSA_HEADS // 2):
        x = f_ref[:, p * 128:(p + 1) * 128]
        xn_t = transposed_bf16(x * scale)
        xr_t = transposed_bf16(rope(x) * scale)
        for e in range(2):
            hd = 2 * p + e
            g, h = hd // NSA_HPG, hd % NSA_HPG
            qn_ref[g, 0, :, h * tq:(h + 1) * tq] = xn_t[e * NSA_DH:(e + 1) * NSA_DH]
            qr_ref[g, 0, :, h * tq:(h + 1) * tq] = xr_t[e * NSA_DH:(e + 1) * NSA_DH]

    def kv_piece(idx, p):
        base = NSA_HEADS * NSA_DH + idx * NSA_KV + p * 128
        return f_ref[:, base:base + 128]

    for p in range(2):
        kc = kv_piece(0, p)
        vc = kv_piece(1, p)
        ks = rope(kv_piece(2, p))
        vs_t = transposed_bf16(kv_piece(3, p))
        kw = rope(kv_piece(4, p))
        vw_t = transposed_bf16(kv_piece(5, p))
        for e in range(2):
            g = 2 * p + e
            sl = slice(e * NSA_DH, (e + 1) * NSA_DH)
            kc_ref[g] = kc[:, sl]
            vc_ref[g] = vc[:, sl]
            ks_low = ks if e == 0 else pltpu.roll(ks, NSA_DH, 1)
            ks_ref[g] = jnp.where(lane < NSA_DH, ks_low, onehot).astype(BF16)
            kw_low = kw if e == 0 else pltpu.roll(kw, NSA_DH, 1)
            kw_ref[g] = jnp.where(lane < NSA_DH, kw_low, jnp.where(lane == NSA_DH, 1.0, 0.0)).astype(BF16)
            vst_ref[g, :NSA_DH] = vs_t[sl]
            vst_ref[g, NSA_DH:] = ones_rows
            vwt_ref[g, :NSA_DH] = vw_t[sl]
            vwt_ref[g, NSA_DH:] = ones_rows
    gl = f_ref[:, 2560:2688]
    gt_ref[...] = jax.nn.sigmoid(gl).T[:3 * NSA_HEADS]


def _nsaprep(feats, cos_t, sin_t, tq):
    s = feats.shape[0]
    ni = s // tq
    kv_f32 = jax.ShapeDtypeStruct((NSA_G, s, NSA_DH), F32)
    kv_b16 = jax.ShapeDtypeStruct((NSA_G, s, NSA_DH), BF16)
    ksa_b16 = jax.ShapeDtypeStruct((NSA_G, s, 2 * NSA_DH), BF16)
    kvt_b16 = jax.ShapeDtypeStruct((NSA_G, V_ROWS, s), BF16)
    q_shape = jax.ShapeDtypeStruct((NSA_G, ni, NSA_DH, NSA_HPG * tq), BF16)
    q_spec = pl.BlockSpec((NSA_G, 1, NSA_DH, NSA_HPG * tq), lambda i: (0, i, 0, 0))
    kv_spec = pl.BlockSpec((NSA_G, tq, NSA_DH), lambda i: (0, i, 0))
    ksa_spec = pl.BlockSpec((NSA_G, tq, 2 * NSA_DH), lambda i: (0, i, 0))
    kvt_spec = pl.BlockSpec((NSA_G, V_ROWS, tq), lambda i: (0, 0, i))
    return pl.pallas_call(
        functools.partial(_nsaprep_body, tq=tq),
        grid=(ni,),
        in_specs=[
            pl.BlockSpec((tq, 3072), lambda i: (i, 1)),
            pl.BlockSpec((tq, 128), lambda i: (i, 0)),
            pl.BlockSpec((tq, 128), lambda i: (i, 0)),
        ],
        out_specs=[q_spec, q_spec, kv_spec, kv_spec, ksa_spec, kvt_spec, ksa_spec, kvt_spec,
                   pl.BlockSpec((3 * NSA_HEADS, tq), lambda i: (0, i))],
        out_shape=[q_shape, q_shape, kv_f32, kv_f32, ksa_b16, kvt_b16, ksa_b16, kvt_b16,
                   jax.ShapeDtypeStruct((3 * NSA_HEADS, s), F32)],
        compiler_params=_cparams(("parallel",)),
        name="nsaprep",
    )(feats, cos_t, sin_t)


def _gelu_tanh(x):
    return 0.5 * x * (1.0 + jnp.tanh(0.7978845608028654 * (x + 0.044715 * x * x * x)))


def _compress_body(x_ref, pe1_ref, pe2_ref, w1a_ref, w1b_ref, w2_ref, o_ref, *, ncp, transpose_out):
    x = x_ref[0]
    a = _dot((x + pe1_ref[...]).astype(BF16), w1a_ref[...])
    b = _dot((x + pe2_ref[...]).astype(BF16), w1b_ref[...])
    hid = a + pltpu.roll(b, ncp - 1, 0)
    act = _gelu_tanh(hid).astype(BF16)
    if transpose_out:
        o_ref[0, :NSA_DH] = _dot_nt(w2_ref[...], act).astype(BF16)
        o_ref[0, NSA_DH:] = jnp.ones((V_ROWS - NSA_DH, ncp), BF16)
    else:
        o_ref[0] = _dot(act, w2_ref[...]).astype(BF16)


def _compress(x, pe, w1, w2, transpose_out):
    g, s, dh = x.shape
    ncp = s // CMP_STRIDE
    half = CMP_STRIDE * dh
    xr = x.reshape(g, ncp, half)
    pe1 = pe[:CMP_STRIDE].reshape(1, half)
    pe2 = pe[CMP_STRIDE:].reshape(1, half)
    w1a = w1[:half].astype(BF16)
    w1b = w1[half:].astype(BF16)
    hid = w1.shape[1]
    w2k = (w2.T if transpose_out else w2).astype(BF16)
    full = lambda a, b: pl.BlockSpec((a, b), lambda i: (0, 0))
    if transpose_out:
        out_shape = jax.ShapeDtypeStruct((g, V_ROWS, ncp), BF16)
        out_spec = pl.BlockSpec((1, V_ROWS, ncp), lambda i: (i, 0, 0))
    else:
        out_shape = jax.ShapeDtypeStruct((g, ncp, dh), BF16)
        out_spec = pl.BlockSpec((1, ncp, dh), lambda i: (i, 0, 0))
    return pl.pallas_call(
        functools.partial(_compress_body, ncp=ncp, transpose_out=transpose_out),
        grid=(g,),
        in_specs=[
            pl.BlockSpec((1, ncp, half), lambda i: (i, 0, 0)),
            full(1, half), full(1, half), full(half, hid), full(half, hid), full(*w2k.shape),
        ],
        out_specs=out_spec,
        out_shape=out_shape,
        compiler_params=_cparams(("parallel",)),
        name="compress",
    )(xr, pe1, pe2, w1a, w1b, w2k)


def _rows_bf16(row, n):
    r16 = jnp.broadcast_to(row, (16, row.shape[1])).astype(BF16)
    return jnp.concatenate([r16] * (n // 16), axis=0)


def _flash_biased(sb, vt_blk, m_ref, acc_ref, h):
    m_old = m_ref[h]
    m_new = jnp.maximum(m_old, jnp.max(sb, axis=0, keepdims=True).astype(F32))
    alpha = jnp.exp2(m_old - m_new)
    p = jnp.exp2(sb - _rows_bf16(m_new, sb.shape[0]))
    acc_ref[h] = alpha * acc_ref[h] + _dot(vt_blk, p)
    m_ref[h] = m_new


def _nsa_body(qn_ref, qr_ref, kc_ref, vct_ref, ov_ref, ks_ref, vst_ref, kw_ref, vwt_ref, gt_ref, o_ref,
              bias_ref, oc_ref, qa_ref, qw_ref, st0_ref, st1_ref, st2_ref, st3_ref, m_ref, acc_ref,
              *, tq, ncp, nsel, topn):
    st_refs = (st0_ref, st1_ref, st2_ref, st3_ref)
    g = pl.program_id(0)
    i = pl.program_id(1)
    q0 = i * tq
    bpc = tq // SEL_BLOCK
    tpos = q0 + lax.broadcasted_iota(jnp.int32, (1, tq), 1)
    heads = range(NSA_HPG)
    hcols = lambda h: slice(h * tq, (h + 1) * tq)

    row_io = lax.broadcasted_iota(jnp.int32, (tq, 1), 0)
    col_io = lax.broadcasted_iota(jnp.int32, (1, tq), 1)
    as_bias = lambda keep: jnp.where(keep, 0.0, NEG).astype(BF16)
    causal_bias = as_bias(row_io <= col_io)
    anti_bias = as_bias(row_io > col_io)

    cmp_last = 2 * CMP_STRIDE - 1
    has_cmp = jnp.where(tpos >= cmp_last, 1.0, 0.0)

    def compressed_and_select(ncc, rows):
        nck = ncc * tq
        cbias = [as_bias((cc * tq + row_io) * CMP_STRIDE + cmp_last <= tpos) for cc in range(ncc)]

        def cmp_scores(h):
            qn_h = qn_ref[0, 0, :, hcols(h)]
            return [_dot(kc_ref[0, cc * tq:(cc + 1) * tq, :], qn_h).astype(BF16) + cbias[cc]
                    for cc in range(ncc)]

        psum = None
        ahead = cmp_scores(0)
        for h in heads:
            sbs = ahead
            if h + 1 < NSA_HPG:
                ahead = cmp_scores(h + 1)
            mc = jnp.max(sbs[0], axis=0, keepdims=True)
            for sb in sbs[1:]:
                mc = jnp.maximum(mc, jnp.max(sb, axis=0, keepdims=True))
            mrows = _rows_bf16(mc.astype(F32), tq)
            pc = jnp.concatenate([jnp.exp2(sb - mrows) for sb in sbs], axis=0)
            rv = _dot(vct_ref[0, :, :nck], pc)
            inv = has_cmp / rv[NSA_DH:NSA_DH + 1]
            oc_ref[h] = rv[:NSA_DH] * inv
            pn = pc * _rows_bf16(inv, nck)
            psum = pn if psum is None else psum + pn
        imp = _dot(ov_ref[:rows, :nck], psum)

        jj = lax.broadcasted_iota(jnp.int32, (rows, tq), 0)
        jf = jj.astype(F32)
        cur = tpos // SEL_BLOCK
        forced = (jj == 0) | (jj == cur) | (jj == cur - 1)
        taken = -(2.0 ** 127)
        score = jnp.where(forced, taken, jnp.where(jj <= cur, imp, -1.0))
        for _ in range(topn - 3):
            mx = jnp.max(score, axis=0, keepdims=True)
            first = jnp.min(jnp.where(score == mx, jf, float(rows)), axis=0, keepdims=True)
            score = jnp.where(jf == first, taken, score)
        bias_ref[:rows] = jnp.where(score == taken, 0.0, NEG)

    ncc_all = ncp // tq
    nvar = 4 if ncc_all % 4 == 0 and nsel % 4 == 0 and nsel // 4 >= 2 * topn else 1
    live_blocks = (i + 1) * bpc
    for v in range(nvar):
        rows = nsel * (v + 1) // nvar
        lo = nsel * v // nvar

        @pl.when((live_blocks > lo) & (live_blocks <= rows))
        def _(v=v, rows=rows):
            compressed_and_select(ncc_all * (v + 1) // nvar, rows)

    def reset():
        m_ref[...] = jnp.full_like(m_ref, NEG)
        acc_ref[...] = jnp.zeros_like(acc_ref)

    def finish(h):
        acc = acc_ref[h]
        l = acc[NSA_DH:NSA_DH + 1]
        return acc[:NSA_DH] * jnp.where(l > 0.0, 1.0 / l, 0.0)

    row16 = lax.broadcasted_iota(jnp.int32, (BIAS_ROWS, NSA_HPG * tq), 0)

    qa_ref[:NSA_DH] = qr_ref[0, 0]

    def set_bias_rows(c):
        for b in range(bpc):
            brow = bias_ref[pl.ds(c * bpc + b, 1), :]
            brow4 = jnp.concatenate([brow] * NSA_HPG, axis=1)
            lo_row = NSA_DH + BIAS_ROWS * b
            qa_ref[lo_row:lo_row + BIAS_ROWS] = jnp.where(row16 == 0, brow4, 0.0).astype(BF16)

    def sel_scores(c, h):
        kblk = ks_ref[0, pl.ds(pl.multiple_of(c * tq, tq), tq), :]
        return _dot(kblk, qa_ref[:, hcols(h)]).astype(BF16)

    reset()
    set_bias_rows(0)
    for h in heads:
        st_refs[h][...] = sel_scores(0, h)

    def sel_chunk(c):
        set_bias_rows(c + 1)
        vblk = vst_ref[0, :, pl.ds(pl.multiple_of(c * tq, tq), tq)]
        nxt = sel_scores(c + 1, 0)
        for h in heads:
            cur_scores = st_refs[h][...]
            after = sel_scores(c + 1, h + 1) if h + 1 < NSA_HPG else None
            _flash_biased(cur_scores, vblk, m_ref, acc_ref, h)
            st_refs[h][...] = nxt
            nxt = after

    def sel_group(cg, carry):
        for u in range(SEL_UNROLL):
            sel_chunk(SEL_UNROLL * cg + u)
        return carry

    lax.fori_loop(0, i // SEL_UNROLL, sel_group, 0)
    done = (i // SEL_UNROLL) * SEL_UNROLL
    part = SEL_UNROLL // 2
    while part >= 1:
        @pl.when((i & part) != 0)
        def _(base=done, part=part):
            for u in range(part):
                sel_chunk(base + u)
        done = done + (i & part)
        part //= 2

    vblk = vst_ref[0, :, pl.ds(pl.multiple_of(q0, tq), tq)]
    for h in heads:
        _flash_biased(st_refs[h][...] + causal_bias, vblk, m_ref, acc_ref, h)
    o_s = [finish(h) for h in heads]

    reset()
    nback = WINDOW // tq
    win_bias = [causal_bias] + [None] * (nback - 1) + [anti_bias]
    qw_ref[:NSA_DH] = qr_ref[0, 0]
    qw_ref[NSA_DH + BIAS_ROWS:] = jnp.zeros((NSA_DH - BIAS_ROWS, NSA_HPG * tq), BF16)

    def stash_window(j, h):
        exists = jnp.where(i >= j, 0.0, NEG)
        if h == 0:
            qw_ref[NSA_DH:NSA_DH + BIAS_ROWS] = jnp.where(row16 == 0, exists, 0.0).astype(BF16)
        kblk = kw_ref[0, pl.ds(pl.multiple_of(jnp.maximum(i - j, 0) * tq, tq), tq), :]
        sb = _dot(kblk, qw_ref[:, hcols(h)]).astype(BF16)
        st_refs[h][...] = sb if win_bias[j] is None else sb + win_bias[j]

    for h in heads:
        stash_window(0, h)
    for j in range(nback + 1):
        vblk = vwt_ref[0, :, pl.ds(pl.multiple_of(jnp.maximum(i - j, 0) * tq, tq), tq)]
        for h in heads:
            _flash_biased(st_refs[h][...], vblk, m_ref, acc_ref, h)
            if j < nback:
                stash_window(j + 1, h)
    o_w = [finish(h) for h in heads]

    def gate(branch, h):
        return gt_ref[pl.ds(branch * NSA_HEADS + g * NSA_HPG + h, 1), :]

    out_t = [gate(0, h) * oc_ref[h] + gate(1, h) * o_s[h] + gate(2, h) * o_w[h] for h in heads]
    halves = [jnp.concatenate(out_t[2 * p:2 * p + 2], axis=0).T for p in range(NSA_HPG // 2)]
    o_ref[...] = jnp.concatenate(halves, axis=1)


def _nsa(qn, qr, kc, vct, ov, ks, vst, kw, vwt, gt, tq):
    g, ni, dh, w4 = qn.shape
    s = ks.shape[1]
    ncp = kc.shape[1]
    nsel = s // SEL_BLOCK
    topn = min(SEL_TOPK, nsel)
    assert tq == SEL_CHUNK and (tq // SEL_BLOCK) * BIAS_ROWS == dh and w4 == NSA_HPG * tq
    q_spec = pl.BlockSpec((1, 1, dh, w4), lambda a, b: (a, b, 0, 0))
    return pl.pallas_call(
        functools.partial(_nsa_body, tq=tq, ncp=ncp, nsel=nsel, topn=topn),
        grid=(g, ni),
        in_specs=[
            q_spec, q_spec,
            pl.BlockSpec((1, ncp, dh), lambda a, b: (a, 0, 0)),
            pl.BlockSpec((1, V_ROWS, ncp), lambda a, b: (a, 0, 0)),
            pl.BlockSpec((nsel, ncp), lambda a, b: (0, 0)),
            pl.BlockSpec((1, s, 2 * dh), lambda a, b: (a, 0, 0)),
            pl.BlockSpec((1, V_ROWS, s), lambda a, b: (a, 0, 0)),
            pl.BlockSpec((1, s, 2 * dh), lambda a, b: (a, 0, 0)),
            pl.BlockSpec((1, V_ROWS, s), lambda a, b: (a, 0, 0)),
            pl.BlockSpec((3 * NSA_HEADS, tq), lambda a, b: (0, b)),
        ],
        out_specs=pl.BlockSpec((tq, NSA_HPG * dh), lambda a, b: (b, a)),
        out_shape=jax.ShapeDtypeStruct((s, NSA_HEADS * dh), F32),
        scratch_shapes=[
            pltpu.VMEM((nsel, tq), F32),
            pltpu.VMEM((NSA_HPG, dh, tq), F32),
            pltpu.VMEM((2 * dh, w4), BF16),
            pltpu.VMEM((2 * dh, w4), BF16),
            pltpu.VMEM((tq, tq), BF16), pltpu.VMEM((tq, tq), BF16),
            pltpu.VMEM((tq, tq), BF16), pltpu.VMEM((tq, tq), BF16),
            pltpu.VMEM((NSA_HPG, 1, tq), F32),
            pltpu.VMEM((NSA_HPG, V_ROWS, tq), F32),
        ],
        compiler_params=_cparams(("arbitrary", "arbitrary")),
        name="nsa",
    )(qn, qr, kc, vct, ov, ks, vst, kw, vwt, gt)


def _memkv_body(mem_ref, g_ref, w_ref, k_ref, v_ref):
    kv = _dot(_rms(mem_ref[...], g_ref[...]).astype(BF16), w_ref[...])
    width = MEM_HEADS * MEM_DH
    k_ref[...] = kv[:, :width].astype(BF16)
    v_ref[...] = kv[:, width:].astype(BF16)


def _memkv(mem, g, w):
    m, d = mem.shape
    width = MEM_HEADS * MEM_DH
    shp = jax.ShapeDtypeStruct((m, width), BF16)
    return pl.pallas_call(
        _memkv_body,
        out_shape=[shp, shp],
        compiler_params=pltpu.CompilerParams(vmem_limit_bytes=VMEM_LIMIT),
        name="memkv",
    )(mem, g, w)


def _mixout_body(yn_ref, bon_ref, gate_ref, ynsa_ref, h_ref, lnw_ref, lnb_ref, wo_rw_ref, wo_nsa_ref,
                 gpost_ref, mpre_ref, wq_ref, k_ref, v_ref, wom_ref, mpost_ref, o_ref):
    y_rw = ((yn_ref[...] * lnw_ref[...] + lnb_ref[...]) + bon_ref[...]) * gate_ref[...]
    y = _dot(y_rw.astype(BF16), wo_rw_ref[...]) + _dot(ynsa_ref[...].astype(BF16), wo_nsa_ref[...])
    h2 = h_ref[...] + _rms(y, gpost_ref[...])
    q = _dot(_rms(h2, mpre_ref[...]).astype(BF16), wq_ref[...])
    scale = MEM_DH ** -0.5
    outs = []
    for hh in range(MEM_HEADS):
        sl = slice(hh * MEM_DH, (hh + 1) * MEM_DH)
        s = _dot_nt(q[:, sl].astype(BF16), k_ref[:, sl]) * scale
        p = jnp.exp(s - jnp.max(s, axis=-1, keepdims=True))
        p = p / jnp.sum(p, axis=-1, keepdims=True)
        outs.append(_dot(p.astype(BF16), v_ref[:, sl]))
    o = jnp.concatenate(outs, axis=-1).astype(BF16)
    m = _dot(o, wom_ref[...])
    o_ref[...] = h2 + _rms(m, mpost_ref[...])


def _mixout(yn, bon, gate, ynsa, h, lnw, lnb, wo_rw, wo_nsa, gpost, mpre, wq, k, v, wom, mpost, tm=256):
    s, d = h.shape
    rows = lambda n: pl.BlockSpec((tm, n), lambda i: (i, 0))
    full = lambda a: pl.BlockSpec(a.shape, lambda i: (0, 0))
    return pl.pallas_call(
        _mixout_body,
        grid=(s // tm,),
        in_specs=[rows(RW_WIDTH), rows(RW_WIDTH), rows(RW_WIDTH), rows(RW_WIDTH), rows(d),
                  full(lnw), full(lnb), full(wo_rw), full(wo_nsa), full(gpost), full(mpre), full(wq),
                  full(k), full(v), full(wom), full(mpost)],
        out_specs=rows(d),
        out_shape=jax.ShapeDtypeStruct((s, d), F32),
        compiler_params=_cparams(("parallel",)),
        name="mixout",
    )(yn, bon, gate, ynsa, h, lnw, lnb, wo_rw, wo_nsa, gpost, mpre, wq, k, v, wom, mpost)


def _rope_tables(s):
    half = ROPE_DIM // 2
    inv_freq = ROPE_THETA ** (-jnp.arange(half, dtype=F32) * 2.0 / ROPE_DIM)
    ang = jnp.arange(s, dtype=jnp.int32).astype(F32)[:, None] * inv_freq[None, :]
    cos, sin = jnp.cos(ang), jnp.sin(ang)
    ones = jnp.ones((s, NSA_DH - ROPE_DIM), F32)
    zeros = jnp.zeros((s, NSA_DH - ROPE_DIM), F32)
    cos_h = jnp.concatenate([cos, cos, ones], axis=1)
    sin_h = jnp.concatenate([-sin, sin, zeros], axis=1)
    return jnp.tile(cos_h, (1, 2)), jnp.tile(sin_h, (1, 2))


def _overlap_t(s):
    ncp = s // CMP_STRIDE
    nsel = s // SEL_BLOCK
    cmp_start = jnp.arange(ncp)[None, :] * CMP_STRIDE
    sel_start = jnp.arange(nsel)[:, None] * SEL_BLOCK
    ov = (cmp_start < sel_start + SEL_BLOCK) & (cmp_start + 2 * CMP_STRIDE - 1 >= sel_start)
    ov = ov & (jnp.arange(ncp)[None, :] < ncp - 1)
    return ov.astype(BF16)


def kernel(x, mem, ffn1_pre_g, ffn1_w_gate, ffn1_w_up, ffn1_w_down, ffn1_post_g, mix_pre_g, w_in, rw_mu, rw_w0, rw_w_up, rw_a0, rw_a_up, rw_g_up, rw_k_k, rw_k_a, rw_r_k, rw_ln_w, rw_ln_b, cmp_pe_k, cmp_w1_k, cmp_w2_k, cmp_pe_v, cmp_w1_v, cmp_w2_v, w_out, mix_post_g, mem_pre_g, mem_norm_g, mem_w_q, mem_w_kv, mem_w_o, mem_post_g, ffn2_pre_g, ffn2_w_gate, ffn2_w_up, ffn2_w_down, ffn2_post_g):
    b, s, d = x.shape
    tq = SEL_CHUNK
    assert b == 1 and d == D_MODEL and s % (CMP_STRIDE * tq) == 0
    row = lambda v: v.reshape(1, -1).astype(F32)

    def ffn_weights(wg, wu, wd):
        return wg.astype(BF16), wu.astype(BF16), wd.astype(BF16)

    h = x[0]
    h = _ffn(h, row(ffn1_pre_g), *ffn_weights(ffn1_w_gate, ffn1_w_up, ffn1_w_down), row(ffn1_post_g))

    rw_cols = 3 * RW_WIDTH + RW_LORA
    nsa_main = NSA_HEADS * NSA_DH + 6 * NSA_KV
    zc = lambda n: jnp.zeros((d, n), BF16)
    w_in_b = w_in.astype(BF16)
    w_in_r = jnp.concatenate([
        w_in_b[:, :3 * RW_WIDTH],
        w_in_b[:, rw_cols:rw_cols + nsa_main],
        w_in_b[:, rw_cols + nsa_main:], zc(128 - 3 * NSA_HEADS),
        w_in_b[:, 3 * RW_WIDTH:rw_cols], zc(RW_LORA_PAD - RW_LORA),
    ], axis=1)
    feats = _inproj(h, row(mix_pre_g), w_in_r)

    mu_a = row(rw_mu[:3 * RW_WIDTH])
    mu_l = row(jnp.pad(rw_mu[3 * RW_WIDTH:], (0, RW_LORA_PAD - RW_LORA)))
    lora_w = lambda w, off: jnp.pad(w, ((off, RW_LORA_PAD - off - w.shape[0]), (0, 0))).astype(BF16)
    at, rt, bt, kt, v, pt, gate, bonus = _rwprep(
        feats, mu_a, mu_l, row(rw_w0), row(rw_a0), row(rw_k_k), row(rw_k_a), row(rw_r_k),
        lora_w(rw_w_up, 0), lora_w(rw_a_up, 64), lora_w(rw_g_up, 128))
    yn = _rwscan(at, rt, bt, kt, v, pt)

    cos_t, sin_t = _rope_tables(s)
    qn, qr, kc, vc, ks, vst, kw, vwt, gt = _nsaprep(feats, cos_t, sin_t, tq)
    k_cmp = _compress(kc, cmp_pe_k, cmp_w1_k, cmp_w2_k, transpose_out=False)
    v_cmp_t = _compress(vc, cmp_pe_v, cmp_w1_v, cmp_w2_v, transpose_out=True)
    y_nsa = _nsa(qn, qr, k_cmp, v_cmp_t, _overlap_t(s), ks, vst, kw, vwt, gt, tq)

    mem_k, mem_v = _memkv(mem[0], row(mem_norm_g), mem_w_kv.astype(BF16))
    h = _mixout(yn, bonus, gate, y_nsa, h, row(rw_ln_w), row(rw_ln_b),
                w_out[:RW_WIDTH].astype(BF16), w_out[RW_WIDTH:].astype(BF16), row(mix_post_g),
                row(mem_pre_g), mem_w_q.astype(BF16), mem_k, mem_v, mem_w_o.astype(BF16), row(mem_post_g))

    h = _ffn(h, row(ffn2_pre_g), *ffn_weights(ffn2_w_gate, ffn2_w_up, ffn2_w_down), row(ffn2_post_g))
    return h[None]
```

```python
import functools

import jax
import jax.numpy as jnp
from jax import lax
from jax.experimental import pallas as pl
from jax.experimental.pallas import tpu as pltpu

F32 = jnp.float32
BF16 = jnp.bfloat16

D_MODEL = 2048
EPS = 1e-6

RW_HEADS = 16
RW_N = 64
RW_WIDTH = 1024
RW_LORA = 288
RW_LORA_PAD = 384
RW_GN_EPS = 64e-5
RW_CHUNK = 64
RW_GROUP = 2

NSA_HEADS = 16
NSA_G = 4
NSA_HPG = 4
NSA_DH = 64
NSA_KV = 256
CMP_STRIDE = 16
SEL_BLOCK = 64
SEL_TOPK = 16
WINDOW = 512
FORCE_BONUS = 1000.0
ROPE_THETA = 500000.0
ROPE_DIM = 16

MEM_HEADS = 4
MEM_DH = 128

LOG2E = 1.4426950408889634
EXP_NEG_HALF = 0.6065306597126334
SEL_CHUNK = 256
SEL_UNROLL = 8
BIAS_ROWS = 16
V_ROWS = NSA_DH + 16

NEG = -1e30
VMEM_LIMIT = 56 * 1024 * 1024


def _cparams(sem):
    return pltpu.CompilerParams(dimension_semantics=sem, vmem_limit_bytes=VMEM_LIMIT)


def _rms(x, g):
    return x * lax.rsqrt(jnp.mean(x * x, axis=-1, keepdims=True) + EPS) * g


def _dot(a, b):
    return jnp.dot(a, b, preferred_element_type=F32)


def _dot_nt(a, b):
    return lax.dot_general(a, b, (((1,), (1,)), ((), ())), preferred_element_type=F32)


def _split2(x):
    hi = x.astype(BF16)
    lo = (x - hi.astype(F32)).astype(BF16)
    return hi, lo


def _split3(x):
    hi = x.astype(BF16)
    r1 = x - hi.astype(F32)
    mid = r1.astype(BF16)
    lo = (r1 - mid.astype(F32)).astype(BF16)
    return hi, mid, lo


def _ffn_body(h_ref, gpre_ref, wg_ref, wu_ref, wd_ref, wgt_ref, wut_ref, wdt_ref, gpost_ref, o_ref, xn_ref,
              *, nj):
    j = pl.program_id(1)

    @pl.when(j == 0)
    def _():
        xn_ref[...] = _rms(h_ref[...], gpre_ref[...]).astype(BF16)
        o_ref[...] = jnp.zeros_like(o_ref)

    xn = xn_ref[...]

    def swiglu_part(wg, wu, wd):
        g = _dot(xn, wg)
        u = _dot(xn, wu)
        return _dot((g * jax.nn.sigmoid(g) * u).astype(BF16), wd)

    o_ref[...] += swiglu_part(wg_ref[...], wu_ref[...], wd_ref[...])

    @pl.when(j == nj - 1)
    def _():
        y = o_ref[...] + swiglu_part(wgt_ref[...], wut_ref[...], wdt_ref[...])
        o_ref[...] = h_ref[...] + 0.5 * _rms(y, gpost_ref[...])


def _ffn(h, pre_g, wg, wu, wd, post_g, tm=1024, tf=256):
    s, d = h.shape
    ff = wg.shape[1]
    nj = ff // tf
    tail = ff - nj * tf
    assert 0 < tail and tail % 128 == 0 and (nj * tf) % tail == 0
    tail_blk = nj * tf // tail
    const = lambda a: pl.BlockSpec(a.shape, lambda i, j: (0, 0))
    return pl.pallas_call(
        functools.partial(_ffn_body, nj=nj),
        grid=(s // tm, nj),
        in_specs=[
            pl.BlockSpec((tm, d), lambda i, j: (i, 0)),
            const(pre_g),
            pl.BlockSpec((d, tf), lambda i, j: (0, j)),
            pl.BlockSpec((d, tf), lambda i, j: (0, j)),
            pl.BlockSpec((tf, d), lambda i, j: (j, 0)),
            pl.BlockSpec((d, tail), lambda i, j: (0, tail_blk)),
            pl.BlockSpec((d, tail), lambda i, j: (0, tail_blk)),
            pl.BlockSpec((tail, d), lambda i, j: (tail_blk, 0)),
            const(post_g),
        ],
        out_specs=pl.BlockSpec((tm, d), lambda i, j: (i, 0)),
        out_shape=jax.ShapeDtypeStruct((s, d), F32),
        scratch_shapes=[pltpu.VMEM((tm, d), BF16)],
        compiler_params=_cparams(("parallel", "arbitrary")),
        name="ffn",
    )(h, pre_g, wg, wu, wd, wg, wu, wd, post_g)


def _inproj_body(h_ref, g_ref, w_ref, o_ref, xn_ref):
    @pl.when(pl.program_id(1) == 0)
    def _():
        xn_ref[...] = _rms(h_ref[...], g_ref[...]).astype(BF16)

    o_ref[...] = _dot(xn_ref[...], w_ref[...])


def _inproj(h, g, w, tm=1024, tn=2048):
    s, d = h.shape
    n = w.shape[1]
    return pl.pallas_call(
        _inproj_body,
        grid=(s // tm, n // tn),
        in_specs=[
            pl.BlockSpec((tm, d), lambda i, j: (i, 0)),
            pl.BlockSpec((1, d), lambda i, j: (0, 0)),
            pl.BlockSpec((d, tn), lambda i, j: (0, j)),
        ],
        out_specs=pl.BlockSpec((tm, tn), lambda i, j: (i, j)),
        out_shape=jax.ShapeDtypeStruct((s, n), F32),
        scratch_shapes=[pltpu.VMEM((tm, d), BF16)],
        compiler_params=_cparams(("parallel", "arbitrary")),
        name="inproj",
    )(h, g, w)


def _rwprep_body(f_ref, fp_ref, lo_ref, lop_ref, mua_ref, mul_ref, w0_ref, a0_ref, kk_ref, ka_ref,
                 rk_ref, wup_ref, aup_ref, gup_ref, e_ref, tri_ref,
                 at_ref, rt_ref, bt_ref, kt_ref, v_ref, pt_ref, g_ref, bon_ref, *, tm):
    first = pl.program_id(0) == 0

    def shifted(x, prev_blk):
        prev_last = jnp.where(first, 0.0, prev_blk[7:8, :])
        rolled = pltpu.roll(x, 1, 0)
        row = lax.broadcasted_iota(jnp.int32, x.shape, 0)
        return jnp.where(row == 0, prev_last, rolled)

    f = f_ref[...]
    fs = f + mua_ref[...] * (shifted(f, fp_ref[...]) - f)
    lo = lo_ref[...]
    los = lo + mul_ref[...] * (shifted(lo, lop_ref[...]) - lo)
    r = fs[:, :RW_WIDTH]
    k = fs[:, RW_WIDTH:2 * RW_WIDTH]
    v = fs[:, 2 * RW_WIDTH:]

    wx = _dot(jnp.tanh(los).astype(BF16), wup_ref[...])
    ax = _dot(los.astype(BF16), aup_ref[...])
    g_ref[...] = _dot(jax.nn.sigmoid(los).astype(BF16), gup_ref[...])

    ld = -EXP_NEG_HALF * jax.nn.sigmoid(w0_ref[...] + wx)
    a = jax.nn.sigmoid(a0_ref[...] + ax)

    e = e_ref[...]

    def headsum(x):
        parts = []
        for c in range(RW_WIDTH // 256):
            hi, lo_ = _split2(x[:, c * 256:(c + 1) * 256])
            parts.append(_dot(hi, e) + _dot(lo_, e))
        return jnp.concatenate(parts, axis=1)

    kk0 = k * kk_ref[...]
    kk = kk0 * lax.rsqrt(jnp.maximum(headsum(kk0 * kk0), 1e-24))
    kmod = k * (1.0 + (a - 1.0) * ka_ref[...])
    bon_ref[...] = headsum(r * kmod * rk_ref[...]) * v

    tri = tri_ref[...]
    cums = []
    for c in range(tm // RW_CHUNK):
        hi, mid, lo_ = _split3(ld[c * RW_CHUNK:(c + 1) * RW_CHUNK])
        cums.append(_dot(tri, hi) + _dot(tri, mid) + _dot(tri, lo_))
    cum = jnp.concatenate(cums, axis=0)
    e_in = jnp.exp(cum)
    e_out = jnp.exp(-cum)
    at_ref[...] = (-kk * jnp.exp(cum - ld)).astype(BF16)
    rt_ref[...] = r * e_in
    bt_ref[...] = (kk * a * e_out).astype(BF16)
    kt_ref[...] = (kmod * e_out).astype(BF16)
    v_ref[...] = v.astype(BF16)
    for c in range(tm // RW_CHUNK):
        pt_ref[c] = e_in[c * RW_CHUNK + RW_CHUNK - 1:(c + 1) * RW_CHUNK, :]


def _rwprep(feats, mu_a, mu_l, w0, a0, k_k, k_a, r_k, wup, aup, gup, tm=256):
    s = feats.shape[0]
    nb8 = tm // 8
    lora_blk = (2 * 3072 - RW_LORA_PAD) // RW_LORA_PAD
    e = (lax.broadcasted_iota(jnp.int32, (256, 256), 0) // RW_N
         == lax.broadcasted_iota(jnp.int32, (256, 256), 1) // RW_N).astype(BF16)
    tri = (lax.broadcasted_iota(jnp.int32, (RW_CHUNK, RW_CHUNK), 0)
           >= lax.broadcasted_iota(jnp.int32, (RW_CHUNK, RW_CHUNK), 1)).astype(BF16)
    row = lambda n: pl.BlockSpec((1, n), lambda i: (0, 0))
    full = lambda a, b: pl.BlockSpec((a, b), lambda i: (0, 0))
    dense_shape = jax.ShapeDtypeStruct((s, RW_WIDTH), F32)
    dense_b16 = jax.ShapeDtypeStruct((s, RW_WIDTH), BF16)
    dense_spec = pl.BlockSpec((tm, RW_WIDTH), lambda i: (i, 0))
    ncb = tm // RW_CHUNK
    return pl.pallas_call(
        functools.partial(_rwprep_body, tm=tm),
        grid=(s // tm,),
        in_specs=[
            pl.BlockSpec((tm, 3072), lambda i: (i, 0)),
            pl.BlockSpec((8, 3072), lambda i: (jnp.maximum(i * nb8 - 1, 0), 0)),
            pl.BlockSpec((tm, RW_LORA_PAD), lambda i: (i, lora_blk)),
            pl.BlockSpec((8, RW_LORA_PAD), lambda i: (jnp.maximum(i * nb8 - 1, 0), lora_blk)),
            row(3072), row(RW_LORA_PAD), row(RW_WIDTH), row(RW_WIDTH), row(RW_WIDTH), row(RW_WIDTH),
            row(RW_WIDTH),
            full(RW_LORA_PAD, RW_WIDTH), full(RW_LORA_PAD, RW_WIDTH), full(RW_LORA_PAD, RW_WIDTH),
            full(256, 256), full(RW_CHUNK, RW_CHUNK),
        ],
        out_specs=[dense_spec] * 5 + [
            pl.BlockSpec((ncb, 1, RW_WIDTH), lambda i: (i, 0, 0)),
            dense_spec,
            dense_spec,
        ],
        out_shape=[dense_b16, dense_shape, dense_b16, dense_b16, dense_b16] + [
            jax.ShapeDtypeStruct((s // RW_CHUNK, 1, RW_WIDTH), F32),
            dense_shape,
            dense_shape,
        ],
        compiler_params=_cparams(("parallel",)),
        name="rwprep",
    )(feats, feats, feats, feats, mu_a, mu_l, w0, a0, k_k, k_a, r_k, wup, aup, gup, e, tri)


def _dot_tn(a, b):
    return lax.dot_general(a, b, (((0,), (0,)), ((), ())), preferred_element_type=F32)


def _rwscan_pairs_body(at_ref, rt_ref, bt_ref, kt_ref, v_ref, pt_ref, y_ref, s_ref, *, nc):
    @pl.when(pl.program_id(0) == 0)
    def _():
        s_ref[...] = jnp.zeros_like(s_ref)

    t = RW_CHUNK
    w = RW_GROUP * RW_N
    row = lax.broadcasted_iota(jnp.int32, (t, w), 0)
    col = lax.broadcasted_iota(jnp.int32, (t, w), 1)
    colh = col % RW_N
    strict = row > colh
    incl = row >= colh
    eye_pair = (row == colh).astype(F32)
    head_of_lane = col // RW_N
    r2 = lax.broadcasted_iota(jnp.int32, (w, w), 0)
    c2 = lax.broadcasted_iota(jnp.int32, (w, w), 1)
    blk_f = ((r2 // RW_N) == (c2 // RW_N)).astype(F32)
    blk_b = blk_f.astype(BF16)
    eye_w = (r2 == c2).astype(F32)

    def bd(x):
        return jnp.concatenate([x] * RW_GROUP, axis=0) * blk_b

    def run_step():
        npair = RW_HEADS // RW_GROUP
        items = [(c, p) for c in range(nc) for p in range(npair)]
        pairs = range(len(items))
        sls = [(slice(c * t, (c + 1) * t), slice(p * w, (p + 1) * w)) for c, p in items]
        pts = [pt_ref[c][:, p * w:(p + 1) * w] for c, p in items]
        bf = lambda xs: [x.astype(BF16) for x in xs]
        a_b = [at_ref[sl] for sl in sls]
        r_f = [rt_ref[sl] for sl in sls]
        b_b = [bt_ref[sl] for sl in sls]
        k_b = [kt_ref[sl] for sl in sls]
        v_b = [v_ref[sl] for sl in sls]
        gg = [_dot_nt(jnp.concatenate([a_b[p], r_f[p].astype(BF16)], axis=0),
                      jnp.concatenate([bd(b_b[p]), bd(k_b[p])], axis=0)) for p in pairs]
        a_ab = [jnp.where(strict, g[:t, :w], 0.0) for g in gg]
        a_ak = bf([jnp.where(strict, g[:t, w:], 0.0) for g in gg])
        a_rb = bf([jnp.where(incl, g[t:, :w], 0.0) for g in gg])
        a_rk = bf([jnp.where(incl, g[t:, w:], 0.0) for g in gg])
        inv = [eye_pair + x for x in a_ab]
        pk = a_ab
        for _ in range(5):
            pkb = bf(pk)
            pk = [_dot(pkb[p], bd(pkb[p])) for p in pairs]
            pkb = bf(pk)
            inv = [inv[p] + _dot(inv[p].astype(BF16), bd(pkb[p])) for p in pairs]
        inv_b = bf(inv)
        w1 = bf([_dot(a_ak[p], bd(v_b[p])) for p in pairs])
        az = [_dot(inv_b[p], jnp.concatenate([bd(a_b[p]), bd(w1[p])], axis=1)) for p in pairs]
        atp = bf([x[:, :w] for x in az])
        z0 = bf([x[:, w:] for x in az])
        ry = [_dot(a_rb[p], jnp.concatenate([bd(atp[p]), bd(z0[p])], axis=1)) for p in pairs]
        rh = bf([r_f[p] + ry[p][:, :w] for p in pairs])
        y0 = [ry[p][:, w:] + _dot(a_rk[p], bd(v_b[p])) for p in pairs]
        m_bd = bf([(eye_w + _dot_tn(atp[p], b_b[p])) * blk_f * pts[p] for p in pairs])
        c_full = [(_dot_tn(z0[p], b_b[p]) + _dot_tn(v_b[p], k_b[p])) * blk_f * pts[p] for p in pairs]
        c_pair = [sum(x[hh * RW_N:(hh + 1) * RW_N] for hh in range(RW_GROUP)) for x in c_full]

        def headmean(x):
            out = jnp.zeros_like(x)
            for hh in range(RW_GROUP):
                mine = head_of_lane == hh
                out = jnp.where(mine, jnp.sum(jnp.where(mine, x, 0.0), axis=-1, keepdims=True), out)
            return out * (1.0 / RW_N)

        state = [s_ref[p] for p in range(npair)]
        for q in pairs:
            p = items[q][1]
            st_b = state[p].astype(BF16)
            y = _dot_nt(rh[q], bd(st_b)) + y0[q]
            state[p] = _dot(st_b, m_bd[q]) + c_pair[q]
            yc = y - headmean(y)
            y_ref[sls[q]] = yc * lax.rsqrt(headmean(yc * yc) + RW_GN_EPS)
        for p in range(npair):
            s_ref[p] = state[p]

    run_step()


def _rwscan(at, rt, bt, kt, v, pt, rows=512):
    s = at.shape[0]
    nc = rows // RW_CHUNK
    dense_spec = pl.BlockSpec((rows, RW_WIDTH), lambda i: (i, 0))
    return pl.pallas_call(
        functools.partial(_rwscan_pairs_body, nc=nc),
        grid=(s // rows,),
        in_specs=[dense_spec] * 5 + [pl.BlockSpec((nc, 1, RW_WIDTH), lambda i: (i, 0, 0))],
        out_specs=dense_spec,
        out_shape=jax.ShapeDtypeStruct((s, RW_WIDTH), F32),
        scratch_shapes=[pltpu.VMEM((RW_HEADS // RW_GROUP, RW_N, RW_GROUP * RW_N), F32)],
        compiler_params=_cparams(("arbitrary",)),
        name="rwscan",
    )(at, rt, bt, kt, v, pt)


def _nsaprep_body(f_ref, cos_ref, sin_ref, qn_ref, qr_ref, kc_ref, vc_ref, ks_ref, vst_ref, kw_ref,
                  vwt_ref, gt_ref, *, tq):
    cs = cos_ref[...]
    sn = sin_ref[...]
    lane = lax.broadcasted_iota(jnp.int32, (tq, 128), 1)
    first8 = (lane % NSA_DH) < (ROPE_DIM // 2)

    def rope(x):
        swapped = jnp.where(first8, pltpu.roll(x, 128 - ROPE_DIM // 2, 1), pltpu.roll(x, ROPE_DIM // 2, 1))
        return x * cs + swapped * sn

    scale = NSA_DH ** -0.5 * LOG2E
    lane_hi = lane - NSA_DH
    blk_in_chunk = (lax.broadcasted_iota(jnp.int32, (tq, 128), 0) % SEL_CHUNK) // SEL_BLOCK
    onehot = ((lane_hi >= 0) & (lane_hi % BIAS_ROWS == 0) & (lane_hi // BIAS_ROWS == blk_in_chunk)).astype(F32)
    ones_rows = jnp.ones((V_ROWS - NSA_DH, tq), BF16)
    eye = (lax.broadcasted_iota(jnp.int32, (128, 128), 0)
           == lax.broadcasted_iota(jnp.int32, (128, 128), 1)).astype(BF16)

    def transposed_bf16(x):
        return _dot_nt(eye, x.astype(BF16)).astype(BF16)

    for p in range(NSA_HEADS // 2):
        x = f_ref[:, p * 128:(p + 1) * 128]
        xn_t = transposed_bf16(x * scale)
        xr_t = transposed_bf16(rope(x) * scale)
        for e in range(2):
            hd = 2 * p + e
            g, h = hd // NSA_HPG, hd % NSA_HPG
            qn_ref[g, 0, :, h * tq:(h + 1) * tq] = xn_t[e * NSA_DH:(e + 1) * NSA_DH]
            qr_ref[g, 0, :, h * tq:(h + 1) * tq] = xr_t[e * NSA_DH:(e + 1) * NSA_DH]

    def kv_piece(idx, p):
        base = NSA_HEADS * NSA_DH + idx * NSA_KV + p * 128
        return f_ref[:, base:base + 128]

    for p in range(2):
        kc = kv_piece(0, p)
        vc = kv_piece(1, p)
        ks = rope(kv_piece(2, p))
        vs_t = transposed_bf16(kv_piece(3, p))
        kw = rope(kv_piece(4, p))
        vw_t = transposed_bf16(kv_piece(5, p))
        for e in range(2):
            g = 2 * p + e
            sl = slice(e * NSA_DH, (e + 1) * NSA_DH)
            kc_ref[g] = kc[:, sl]
            vc_ref[g] = vc[:, sl]
            ks_low = ks if e == 0 else pltpu.roll(ks, NSA_DH, 1)
            ks_ref[g] = jnp.where(lane < NSA_DH, ks_low, onehot).astype(BF16)
            kw_low = kw if e == 0 else pltpu.roll(kw, NSA_DH, 1)
            kw_ref[g] = jnp.where(lane < NSA_DH, kw_low, jnp.where(lane == NSA_DH, 1.0, 0.0)).astype(BF16)
            vst_ref[g, :NSA_DH] = vs_t[sl]
            vst_ref[g, NSA_DH:] = ones_rows
            vwt_ref[g, :NSA_DH] = vw_t[sl]
            vwt_ref[g, NSA_DH:] = ones_rows
    gl = f_ref[:, 2560:2688]
    gt_ref[...] = jax.nn.sigmoid(gl).T[:3 * NSA_HEADS]


def _nsaprep(feats, cos_t, sin_t, tq):
    s = feats.shape[0]
    ni = s // tq
    kv_f32 = jax.ShapeDtypeStruct((NSA_G, s, NSA_DH), F32)
    kv_b16 = jax.ShapeDtypeStruct((NSA_G, s, NSA_DH), BF16)
    ksa_b16 = jax.ShapeDtypeStruct((NSA_G, s, 2 * NSA_DH), BF16)
    kvt_b16 = jax.ShapeDtypeStruct((NSA_G, V_ROWS, s), BF16)
    q_shape = jax.ShapeDtypeStruct((NSA_G, ni, NSA_DH, NSA_HPG * tq), BF16)
    q_spec = pl.BlockSpec((NSA_G, 1, NSA_DH, NSA_HPG * tq), lambda i: (0, i, 0, 0))
    kv_spec = pl.BlockSpec((NSA_G, tq, NSA_DH), lambda i: (0, i, 0))
    ksa_spec = pl.BlockSpec((NSA_G, tq, 2 * NSA_DH), lambda i: (0, i, 0))
    kvt_spec = pl.BlockSpec((NSA_G, V_ROWS, tq), lambda i: (0, 0, i))
    return pl.pallas_call(
        functools.partial(_nsaprep_body, tq=tq),
        grid=(ni,),
        in_specs=[
            pl.BlockSpec((tq, 3072), lambda i: (i, 1)),
            pl.BlockSpec((tq, 128), lambda i: (i, 0)),
            pl.BlockSpec((tq, 128), lambda i: (i, 0)),
        ],
        out_specs=[q_spec, q_spec, kv_spec, kv_spec, ksa_spec, kvt_spec, ksa_spec, kvt_spec,
                   pl.BlockSpec((3 * NSA_HEADS, tq), lambda i: (0, i))],
        out_shape=[q_shape, q_shape, kv_f32, kv_f32, ksa_b16, kvt_b16, ksa_b16, kvt_b16,
                   jax.ShapeDtypeStruct((3 * NSA_HEADS, s), F32)],
        compiler_params=_cparams(("parallel",)),
        name="nsaprep",
    )(feats, cos_t, sin_t)


def _gelu_tanh(x):
    return 0.5 * x * (1.0 + jnp.tanh(0.7978845608028654 * (x + 0.044715 * x * x * x)))


def _compress_body(x_ref, pe1_ref, pe2_ref, w1a_ref, w1b_ref, w2_ref, o_ref, *, ncp, transpose_out):
    x = x_ref[0]
    a = _dot((x + pe1_ref[...]).astype(BF16), w1a_ref[...])
    b = _dot((x + pe2_ref[...]).astype(BF16), w1b_ref[...])
    hid = a + pltpu.roll(b, ncp - 1, 0)
    act = _gelu_tanh(hid).astype(BF16)
    if transpose_out:
        o_ref[0, :NSA_DH] = _dot_nt(w2_ref[...], act).astype(BF16)
        o_ref[0, NSA_DH:] = jnp.ones((V_ROWS - NSA_DH, ncp), BF16)
    else:
        o_ref[0] = _dot(act, w2_ref[...]).astype(BF16)


def _compress(x, pe, w1, w2, transpose_out):
    g, s, dh = x.shape
    ncp = s // CMP_STRIDE
    half = CMP_STRIDE * dh
    xr = x.reshape(g, ncp, half)
    pe1 = pe[:CMP_STRIDE].reshape(1, half)
    pe2 = pe[CMP_STRIDE:].reshape(1, half)
    w1a = w1[:half].astype(BF16)
    w1b = w1[half:].astype(BF16)
    hid = w1.shape[1]
    w2k = (w2.T if transpose_out else w2).astype(BF16)
    full = lambda a, b: pl.BlockSpec((a, b), lambda i: (0, 0))
    if transpose_out:
        out_shape = jax.ShapeDtypeStruct((g, V_ROWS, ncp), BF16)
        out_spec = pl.BlockSpec((1, V_ROWS, ncp), lambda i: (i, 0, 0))
    else:
        out_shape = jax.ShapeDtypeStruct((g, ncp, dh), BF16)
        out_spec = pl.BlockSpec((1, ncp, dh), lambda i: (i, 0, 0))
    return pl.pallas_call(
        functools.partial(_compress_body, ncp=ncp, transpose_out=transpose_out),
        grid=(g,),
        in_specs=[
            pl.BlockSpec((1, ncp, half), lambda i: (i, 0, 0)),
            full(1, half), full(1, half), full(half, hid), full(half, hid), full(*w2k.shape),
        ],
        out_specs=out_spec,
        out_shape=out_shape,
        compiler_params=_cparams(("parallel",)),
        name="compress",
    )(xr, pe1, pe2, w1a, w1b, w2k)


def _rows_bf16(row, n):
    r16 = jnp.broadcast_to(row, (16, row.shape[1])).astype(BF16)
    return jnp.concatenate([r16] * (n // 16), axis=0)


def _flash_biased(sb, vt_blk, m_ref, acc_ref, h):
    m_old = m_ref[h]
    m_new = jnp.maximum(m_old, jnp.max(sb, axis=0, keepdims=True).astype(F32))
    alpha = jnp.exp2(m_old - m_new)
    p = jnp.exp2(sb - _rows_bf16(m_new, sb.shape[0]))
    acc_ref[h] = alpha * acc_ref[h] + _dot(vt_blk, p)
    m_ref[h] = m_new


def _nsa_body(qn_ref, qr_ref, kc_ref, vct_ref, ov_ref, ks_ref, vst_ref, kw_ref, vwt_ref, gt_ref, o_ref,
              bias_ref, oc_ref, qa_ref, qw_ref, st0_ref, st1_ref, st2_ref, st3_ref, m_ref, acc_ref,
              *, tq, ncp, nsel, topn):
    st_refs = (st0_ref, st1_ref, st2_ref, st3_ref)
    g = pl.program_id(0)
    i = pl.program_id(1)
    q0 = i * tq
    bpc = tq // SEL_BLOCK
    tpos = q0 + lax.broadcasted_iota(jnp.int32, (1, tq), 1)
    heads = range(NSA_HPG)
    hcols = lambda h: slice(h * tq, (h + 1) * tq)

    row_io = lax.broadcasted_iota(jnp.int32, (tq, 1), 0)
    col_io = lax.broadcasted_iota(jnp.int32, (1, tq), 1)
    as_bias = lambda keep: jnp.where(keep, 0.0, NEG).astype(BF16)
    causal_bias = as_bias(row_io <= col_io)
    anti_bias = as_bias(row_io > col_io)

    cmp_last = 2 * CMP_STRIDE - 1
    has_cmp = jnp.where(tpos >= cmp_last, 1.0, 0.0)

    def compressed_and_select(ncc, rows):
        nck = ncc * tq
        cbias = [as_bias((cc * tq + row_io) * CMP_STRIDE + cmp_last <= tpos) for cc in range(ncc)]

        def cmp_scores(h):
            qn_h = qn_ref[0, 0, :, hcols(h)]
            return [_dot(kc_ref[0, cc * tq:(cc + 1) * tq, :], qn_h).astype(BF16) + cbias[cc]
                    for cc in range(ncc)]

        psum = None
        ahead = cmp_scores(0)
        for h in heads:
            sbs = ahead
            if h + 1 < NSA_HPG:
                ahead = cmp_scores(h + 1)
            mc = jnp.max(sbs[0], axis=0, keepdims=True)
            for sb in sbs[1:]:
                mc = jnp.maximum(mc, jnp.max(sb, axis=0, keepdims=True))
            mrows = _rows_bf16(mc.astype(F32), tq)
            pc = jnp.concatenate([jnp.exp2(sb - mrows) for sb in sbs], axis=0)
            rv = _dot(vct_ref[0, :, :nck], pc)
            inv = has_cmp / rv[NSA_DH:NSA_DH + 1]
            oc_ref[h] = rv[:NSA_DH] * inv
            pn = pc * _rows_bf16(inv, nck)
            psum = pn if psum is None else psum + pn
        imp = _dot(ov_ref[:rows, :nck], psum)

        jj = lax.broadcasted_iota(jnp.int32, (rows, tq), 0)
        jf = jj.astype(F32)
        cur = tpos // SEL_BLOCK
        forced = (jj == 0) | (jj == cur) | (jj == cur - 1)
        taken = -(2.0 ** 127)
        score = jnp.where(forced, taken, jnp.where(jj <= cur, imp, -1.0))
        for _ in range(topn - 3):
            mx = jnp.max(score, axis=0, keepdims=True)
            first = jnp.min(jnp.where(score == mx, jf, float(rows)), axis=0, keepdims=True)
            score = jnp.where(jf == first, taken, score)
        bias_ref[:rows] = jnp.where(score == taken, 0.0, NEG)

    ncc_all = ncp // tq
    nvar = 4 if ncc_all % 4 == 0 and nsel % 4 == 0 and nsel // 4 >= 2 * topn else 1
    live_blocks = (i + 1) * bpc
    for v in range(nvar):
        rows = nsel * (v + 1) // nvar
        lo = nsel * v // nvar

        @pl.when((live_blocks > lo) & (live_blocks <= rows))
        def _(v=v, rows=rows):
            compressed_and_select(ncc_all * (v + 1) // nvar, rows)

    def reset():
        m_ref[...] = jnp.full_like(m_ref, NEG)
        acc_ref[...] = jnp.zeros_like(acc_ref)

    def finish(h):
        acc = acc_ref[h]
        l = acc[NSA_DH:NSA_DH + 1]
        return acc[:NSA_DH] * jnp.where(l > 0.0, 1.0 / l, 0.0)

    row16 = lax.broadcasted_iota(jnp.int32, (BIAS_ROWS, NSA_HPG * tq), 0)

    qa_ref[:NSA_DH] = qr_ref[0, 0]

    def set_bias_rows(c):
        for b in range(bpc):
            brow = bias_ref[pl.ds(c * bpc + b, 1), :]
            brow4 = jnp.concatenate([brow] * NSA_HPG, axis=1)
            lo_row = NSA_DH + BIAS_ROWS * b
            qa_ref[lo_row:lo_row + BIAS_ROWS] = jnp.where(row16 == 0, brow4, 0.0).astype(BF16)

    def sel_scores(c, h):
        kblk = ks_ref[0, pl.ds(pl.multiple_of(c * tq, tq), tq), :]
        return _dot(kblk, qa_ref[:, hcols(h)]).astype(BF16)

    reset()
    set_bias_rows(0)
    for h in heads:
        st_refs[h][...] = sel_scores(0, h)

    def sel_chunk(c):
        set_bias_rows(c + 1)
        vblk = vst_ref[0, :, pl.ds(pl.multiple_of(c * tq, tq), tq)]
        nxt = sel_scores(c + 1, 0)
        for h in heads:
            cur_scores = st_refs[h][...]
            after = sel_scores(c + 1, h + 1) if h + 1 < NSA_HPG else None
            _flash_biased(cur_scores, vblk, m_ref, acc_ref, h)
            st_refs[h][...] = nxt
            nxt = after

    def sel_group(cg, carry):
        for u in range(SEL_UNROLL):
            sel_chunk(SEL_UNROLL * cg + u)
        return carry

    lax.fori_loop(0, i // SEL_UNROLL, sel_group, 0)
    done = (i // SEL_UNROLL) * SEL_UNROLL
    part = SEL_UNROLL // 2
    while part >= 1:
        @pl.when((i & part) != 0)
        def _(base=done, part=part):
            for u in range(part):
                sel_chunk(base + u)
        done = done + (i & part)
        part //= 2

    vblk = vst_ref[0, :, pl.ds(pl.multiple_of(q0, tq), tq)]
    for h in heads:
        _flash_biased(st_refs[h][...] + causal_bias, vblk, m_ref, acc_ref, h)
    o_s = [finish(h) for h in heads]

    reset()
    nback = WINDOW // tq
    win_bias = [causal_bias] + [None] * (nback - 1) + [anti_bias]
    qw_ref[:NSA_DH] = qr_ref[0, 0]
    qw_ref[NSA_DH + BIAS_ROWS:] = jnp.zeros((NSA_DH - BIAS_ROWS, NSA_HPG * tq), BF16)

    def stash_window(j, h):
        exists = jnp.where(i >= j, 0.0, NEG)
        if h == 0:
            qw_ref[NSA_DH:NSA_DH + BIAS_ROWS] = jnp.where(row16 == 0, exists, 0.0).astype(BF16)
        kblk = kw_ref[0, pl.ds(pl.multiple_of(jnp.maximum(i - j, 0) * tq, tq), tq), :]
        sb = _dot(kblk, qw_ref[:, hcols(h)]).astype(BF16)
        st_refs[h][...] = sb if win_bias[j] is None else sb + win_bias[j]

    for h in heads:
        stash_window(0, h)
    for j in range(nback + 1):
        vblk = vwt_ref[0, :, pl.ds(pl.multiple_of(jnp.maximum(i - j, 0) * tq, tq), tq)]
        for h in heads:
            _flash_biased(st_refs[h][...], vblk, m_ref, acc_ref, h)
            if j < nback:
                stash_window(j + 1, h)
    o_w = [finish(h) for h in heads]

    def gate(branch, h):
        return gt_ref[pl.ds(branch * NSA_HEADS + g * NSA_HPG + h, 1), :]

    out_t = [gate(0, h) * oc_ref[h] + gate(1, h) * o_s[h] + gate(2, h) * o_w[h] for h in heads]
    halves = [jnp.concatenate(out_t[2 * p:2 * p + 2], axis=0).T for p in range(NSA_HPG // 2)]
    o_ref[...] = jnp.concatenate(halves, axis=1)


def _nsa(qn, qr, kc, vct, ov, ks, vst, kw, vwt, gt, tq):
    g, ni, dh, w4 = qn.shape
    s = ks.shape[1]
    ncp = kc.shape[1]
    nsel = s // SEL_BLOCK
    topn = min(SEL_TOPK, nsel)
    assert tq == SEL_CHUNK and (tq // SEL_BLOCK) * BIAS_ROWS == dh and w4 == NSA_HPG * tq
    q_spec = pl.BlockSpec((1, 1, dh, w4), lambda a, b: (a, b, 0, 0))
    return pl.pallas_call(
        functools.partial(_nsa_body, tq=tq, ncp=ncp, nsel=nsel, topn=topn),
        grid=(g, ni),
        in_specs=[
            q_spec, q_spec,
            pl.BlockSpec((1, ncp, dh), lambda a, b: (a, 0, 0)),
            pl.BlockSpec((1, V_ROWS, ncp), lambda a, b: (a, 0, 0)),
            pl.BlockSpec((nsel, ncp), lambda a, b: (0, 0)),
            pl.BlockSpec((1, s, 2 * dh), lambda a, b: (a, 0, 0)),
            pl.BlockSpec((1, V_ROWS, s), lambda a, b: (a, 0, 0)),
            pl.BlockSpec((1, s, 2 * dh), lambda a, b: (a, 0, 0)),
            pl.BlockSpec((1, V_ROWS, s), lambda a, b: (a, 0, 0)),
            pl.BlockSpec((3 * NSA_HEADS, tq), lambda a, b: (0, b)),
        ],
        out_specs=pl.BlockSpec((tq, NSA_HPG * dh), lambda a, b: (b, a)),
        out_shape=jax.ShapeDtypeStruct((s, NSA_HEADS * dh), F32),
        scratch_shapes=[
            pltpu.VMEM((nsel, tq), F32),
            pltpu.VMEM((NSA_HPG, dh, tq), F32),
            pltpu.VMEM((2 * dh, w4), BF16),
            pltpu.VMEM((2 * dh, w4), BF16),
            pltpu.VMEM((tq, tq), BF16), pltpu.VMEM((tq, tq), BF16),
            pltpu.VMEM((tq, tq), BF16), pltpu.VMEM((tq, tq), BF16),
            pltpu.VMEM((NSA_HPG, 1, tq), F32),
            pltpu.VMEM((NSA_HPG, V_ROWS, tq), F32),
        ],
        compiler_params=_cparams(("arbitrary", "arbitrary")),
        name="nsa",
    )(qn, qr, kc, vct, ov, ks, vst, kw, vwt, gt)


def _memkv_body(mem_ref, g_ref, w_ref, k_ref, v_ref):
    kv = _dot(_rms(mem_ref[...], g_ref[...]).astype(BF16), w_ref[...])
    width = MEM_HEADS * MEM_DH
    k_ref[...] = kv[:, :width].astype(BF16)
    v_ref[...] = kv[:, width:].astype(BF16)


def _memkv(mem, g, w):
    m, d = mem.shape
    width = MEM_HEADS * MEM_DH
    shp = jax.ShapeDtypeStruct((m, width), BF16)
    return pl.pallas_call(
        _memkv_body,
        out_shape=[shp, shp],
        compiler_params=pltpu.CompilerParams(vmem_limit_bytes=VMEM_LIMIT),
        name="memkv",
    )(mem, g, w)


def _mixout_body(yn_ref, bon_ref, gate_ref, ynsa_ref, h_ref, lnw_ref, lnb_ref, wo_rw_ref, wo_nsa_ref,
                 gpost_ref, mpre_ref, wq_ref, k_ref, v_ref, wom_ref, mpost_ref, o_ref):
    y_rw = ((yn_ref[...] * lnw_ref[...] + lnb_ref[...]) + bon_ref[...]) * gate_ref[...]
    y = _dot(y_rw.astype(BF16), wo_rw_ref[...]) + _dot(ynsa_ref[...].astype(BF16), wo_nsa_ref[...])
    h2 = h_ref[...] + _rms(y, gpost_ref[...])
    q = _dot(_rms(h2, mpre_ref[...]).astype(BF16), wq_ref[...])
    scale = MEM_DH ** -0.5
    outs = []
    for hh in range(MEM_HEADS):
        sl = slice(hh * MEM_DH, (hh + 1) * MEM_DH)
        s = _dot_nt(q[:, sl].astype(BF16), k_ref[:, sl]) * scale
        p = jnp.exp(s - jnp.max(s, axis=-1, keepdims=True))
        p = p / jnp.sum(p, axis=-1, keepdims=True)
        outs.append(_dot(p.astype(BF16), v_ref[:, sl]))
    o = jnp.concatenate(outs, axis=-1).astype(BF16)
    m = _dot(o, wom_ref[...])
    o_ref[...] = h2 + _rms(m, mpost_ref[...])


def _mixout(yn, bon, gate, ynsa, h, lnw, lnb, w_out, gpost, mpre, wq, k, v, wom, mpost, tm=256):
    s, d = h.shape
    rows = lambda n: pl.BlockSpec((tm, n), lambda i: (i, 0))
    full = lambda a: pl.BlockSpec(a.shape, lambda i: (0, 0))
    wo_rw = pl.BlockSpec((RW_WIDTH, d), lambda i: (0, 0))
    wo_nsa = pl.BlockSpec((RW_WIDTH, d), lambda i: (1, 0))
    return pl.pallas_call(
        _mixout_body,
        grid=(s // tm,),
        in_specs=[rows(RW_WIDTH), rows(RW_WIDTH), rows(RW_WIDTH), rows(RW_WIDTH), rows(d),
                  full(lnw), full(lnb), wo_rw, wo_nsa, full(gpost), full(mpre), full(wq),
                  full(k), full(v), full(wom), full(mpost)],
        out_specs=rows(d),
        out_shape=jax.ShapeDtypeStruct((s, d), F32),
        compiler_params=_cparams(("parallel",)),
        name="mixout",
    )(yn, bon, gate, ynsa, h, lnw, lnb, w_out, w_out, gpost, mpre, wq, k, v, wom, mpost)


def _rope_tables(s):
    half = ROPE_DIM // 2
    inv_freq = ROPE_THETA ** (-jnp.arange(half, dtype=F32) * 2.0 / ROPE_DIM)
    ang = jnp.arange(s, dtype=jnp.int32).astype(F32)[:, None] * inv_freq[None, :]
    cos, sin = jnp.cos(ang), jnp.sin(ang)
    ones = jnp.ones((s, NSA_DH - ROPE_DIM), F32)
    zeros = jnp.zeros((s, NSA_DH - ROPE_DIM), F32)
    cos_h = jnp.concatenate([cos, cos, ones], axis=1)
    sin_h = jnp.concatenate([-sin, sin, zeros], axis=1)
    return jnp.tile(cos_h, (1, 2)), jnp.tile(sin_h, (1, 2))


def _overlap_t(s):
    ncp = s // CMP_STRIDE
    nsel = s // SEL_BLOCK
    cmp_start = jnp.arange(ncp)[None, :] * CMP_STRIDE
    sel_start = jnp.arange(nsel)[:, None] * SEL_BLOCK
    ov = (cmp_start < sel_start + SEL_BLOCK) & (cmp_start + 2 * CMP_STRIDE - 1 >= sel_start)
    ov = ov & (jnp.arange(ncp)[None, :] < ncp - 1)
    return ov.astype(BF16)


def kernel(x, mem, ffn1_pre_g, ffn1_w_gate, ffn1_w_up, ffn1_w_down, ffn1_post_g, mix_pre_g, w_in, rw_mu, rw_w0, rw_w_up, rw_a0, rw_a_up, rw_g_up, rw_k_k, rw_k_a, rw_r_k, rw_ln_w, rw_ln_b, cmp_pe_k, cmp_w1_k, cmp_w2_k, cmp_pe_v, cmp_w1_v, cmp_w2_v, w_out, mix_post_g, mem_pre_g, mem_norm_g, mem_w_q, mem_w_kv, mem_w_o, mem_post_g, ffn2_pre_g, ffn2_w_gate, ffn2_w_up, ffn2_w_down, ffn2_post_g):
    b, s, d = x.shape
    tq = SEL_CHUNK
    assert b == 1 and d == D_MODEL and s % (CMP_STRIDE * tq) == 0
    row = lambda v: v.reshape(1, -1).astype(F32)

    def ffn_weights(wg, wu, wd):
        return wg.astype(BF16), wu.astype(BF16), wd.astype(BF16)

    h = x[0]
    h = _ffn(h, row(ffn1_pre_g), *ffn_weights(ffn1_w_gate, ffn1_w_up, ffn1_w_down), row(ffn1_post_g))

    rw_cols = 3 * RW_WIDTH + RW_LORA
    nsa_main = NSA_HEADS * NSA_DH + 6 * NSA_KV
    zc = lambda n: jnp.zeros((d, n), BF16)
    w_in_b = w_in.astype(BF16)
    w_in_r = jnp.concatenate([
        w_in_b[:, :3 * RW_WIDTH],
        w_in_b[:, rw_cols:rw_cols + nsa_main],
        w_in_b[:, rw_cols + nsa_main:], zc(128 - 3 * NSA_HEADS),
        w_in_b[:, 3 * RW_WIDTH:rw_cols], zc(RW_LORA_PAD - RW_LORA),
    ], axis=1)
    feats = _inproj(h, row(mix_pre_g), w_in_r)

    mu_a = row(rw_mu[:3 * RW_WIDTH])
    mu_l = row(jnp.pad(rw_mu[3 * RW_WIDTH:], (0, RW_LORA_PAD - RW_LORA)))
    lora_w = lambda w, off: jnp.pad(w, ((off, RW_LORA_PAD - off - w.shape[0]), (0, 0))).astype(BF16)
    at, rt, bt, kt, v, pt, gate, bonus = _rwprep(
        feats, mu_a, mu_l, row(rw_w0), row(rw_a0), row(rw_k_k), row(rw_k_a), row(rw_r_k),
        lora_w(rw_w_up, 0), lora_w(rw_a_up, 64), lora_w(rw_g_up, 128))
    yn = _rwscan(at, rt, bt, kt, v, pt)

    cos_t, sin_t = _rope_tables(s)
    qn, qr, kc, vc, ks, vst, kw, vwt, gt = _nsaprep(feats, cos_t, sin_t, tq)
    k_cmp = _compress(kc, cmp_pe_k, cmp_w1_k, cmp_w2_k, transpose_out=False)
    v_cmp_t = _compress(vc, cmp_pe_v, cmp_w1_v, cmp_w2_v, transpose_out=True)
    y_nsa = _nsa(qn, qr, k_cmp, v_cmp_t, _overlap_t(s), ks, vst, kw, vwt, gt, tq)

    mem_k, mem_v = _memkv(mem[0], row(mem_norm_g), mem_w_kv.astype(BF16))
    h = _mixout(yn, bonus, gate, y_nsa, h, row(rw_ln_w), row(rw_ln_b),
                w_out.astype(BF16), row(mix_post_g),
                row(mem_pre_g), mem_w_q.astype(BF16), mem_k, mem_v, mem_w_o.astype(BF16), row(mem_post_g))

    h = _ffn(h, row(ffn2_pre_g), *ffn_weights(ffn2_w_gate, ffn2_w_up, ffn2_w_down), row(ffn2_post_g))
    return h[None]
```

```python
import functools

import jax
import jax.numpy as jnp
from jax import lax
from jax.experimental import pallas as pl
from jax.experimental.pallas import tpu as pltpu

F32 = jnp.float32
BF16 = jnp.bfloat16

D_MODEL = 2048
EPS = 1e-6

RW_HEADS = 16
RW_N = 64
RW_WIDTH = 1024
RW_LORA = 288
RW_LORA_PAD = 384
RW_GN_EPS = 64e-5
RW_CHUNK = 64
RW_GROUP = 2

NSA_HEADS = 16
NSA_G = 4
NSA_HPG = 4
NSA_DH = 64
NSA_KV = 256
CMP_STRIDE = 16
SEL_BLOCK = 64
SEL_TOPK = 16
WINDOW = 512
FORCE_BONUS = 1000.0
ROPE_THETA = 500000.0
ROPE_DIM = 16

MEM_HEADS = 4
MEM_DH = 128

LOG2E = 1.4426950408889634
EXP_NEG_HALF = 0.6065306597126334
SEL_CHUNK = 256
SEL_UNROLL = 8
BIAS_ROWS = 16
V_ROWS = NSA_DH + 16

NEG = -1e30
VMEM_LIMIT = 56 * 1024 * 1024


def _cparams(sem):
    return pltpu.CompilerParams(dimension_semantics=sem, vmem_limit_bytes=VMEM_LIMIT)


def _rms(x, g):
    return x * lax.rsqrt(jnp.mean(x * x, axis=-1, keepdims=True) + EPS) * g


def _dot(a, b):
    return jnp.dot(a, b, preferred_element_type=F32)


def _dot_nt(a, b):
    return lax.dot_general(a, b, (((1,), (1,)), ((), ())), preferred_element_type=F32)


def _split2(x):
    hi = x.astype(BF16)
    lo = (x - hi.astype(F32)).astype(BF16)
    return hi, lo


def _split3(x):
    hi = x.astype(BF16)
    r1 = x - hi.astype(F32)
    mid = r1.astype(BF16)
    lo = (r1 - mid.astype(F32)).astype(BF16)
    return hi, mid, lo


def _ffn_body(h_ref, gpre_ref, wg_ref, wu_ref, wd_ref, wgt_ref, wut_ref, wdt_ref, gpost_ref, o_ref, xn_ref,
              *, nj):
    j = pl.program_id(1)

    @pl.when(j == 0)
    def _():
        xn_ref[...] = _rms(h_ref[...], gpre_ref[...]).astype(BF16)
        o_ref[...] = jnp.zeros_like(o_ref)

    xn = xn_ref[...]

    def swiglu_part(wg, wu, wd):
        g = _dot(xn, wg)
        u = _dot(xn, wu)
        return _dot((g * jax.nn.sigmoid(g) * u).astype(BF16), wd)

    o_ref[...] += swiglu_part(wg_ref[...], wu_ref[...], wd_ref[...])

    @pl.when(j == nj - 1)
    def _():
        y = o_ref[...] + swiglu_part(wgt_ref[...], wut_ref[...], wdt_ref[...])
        o_ref[...] = h_ref[...] + 0.5 * _rms(y, gpost_ref[...])


def _ffn(h, pre_g, wg, wu, wd, post_g, tm=1024, tf=256):
    s, d = h.shape
    ff = wg.shape[1]
    nj = ff // tf
    tail = ff - nj * tf
    assert 0 < tail and tail % 128 == 0 and (nj * tf) % tail == 0
    tail_blk = nj * tf // tail
    const = lambda a: pl.BlockSpec(a.shape, lambda i, j: (0, 0))
    return pl.pallas_call(
        functools.partial(_ffn_body, nj=nj),
        grid=(s // tm, nj),
        in_specs=[
            pl.BlockSpec((tm, d), lambda i, j: (i, 0)),
            const(pre_g),
            pl.BlockSpec((d, tf), lambda i, j: (0, j)),
            pl.BlockSpec((d, tf), lambda i, j: (0, j)),
            pl.BlockSpec((tf, d), lambda i, j: (j, 0)),
            pl.BlockSpec((d, tail), lambda i, j: (0, tail_blk)),
            pl.BlockSpec((d, tail), lambda i, j: (0, tail_blk)),
            pl.BlockSpec((tail, d), lambda i, j: (tail_blk, 0)),
            const(post_g),
        ],
        out_specs=pl.BlockSpec((tm, d), lambda i, j: (i, 0)),
        out_shape=jax.ShapeDtypeStruct((s, d), F32),
        scratch_shapes=[pltpu.VMEM((tm, d), BF16)],
        compiler_params=_cparams(("parallel", "arbitrary")),
        name="ffn",
    )(h, pre_g, wg, wu, wd, wg, wu, wd, post_g)


def _regroup_body(w_ref, o_ref):
    w = w_ref[...]
    rows = w.shape[0]
    rw_main = 3 * RW_WIDTH
    rw_cols = rw_main + RW_LORA
    nsa_main = NSA_HEADS * NSA_DH + 6 * NSA_KV
    zeros = lambda n: jnp.zeros((rows, n), F32)
    o_ref[...] = jnp.concatenate([
        w[:, :rw_main],
        w[:, rw_cols:rw_cols + nsa_main],
        w[:, rw_cols + nsa_main:], zeros(128 - 3 * NSA_HEADS),
        w[:, rw_main:rw_cols], zeros(RW_LORA_PAD - RW_LORA),
    ], axis=1).astype(BF16)


def _regroup_w_in(w_in, tr=256):
    d, cols = w_in.shape
    return pl.pallas_call(
        _regroup_body,
        grid=(d // tr,),
        in_specs=[pl.BlockSpec((tr, cols), lambda i: (i, 0))],
        out_specs=pl.BlockSpec((tr, 2 * 3072), lambda i: (i, 0)),
        out_shape=jax.ShapeDtypeStruct((d, 2 * 3072), BF16),
        compiler_params=_cparams(("parallel",)),
        name="regroup",
    )(w_in)


def _inproj_body(h_ref, g_ref, w_ref, o_ref, xn_ref):
    @pl.when(pl.program_id(1) == 0)
    def _():
        xn_ref[...] = _rms(h_ref[...], g_ref[...]).astype(BF16)

    o_ref[...] = _dot(xn_ref[...], w_ref[...])


def _inproj(h, g, w, tm=1024, tn=2048):
    s, d = h.shape
    n = w.shape[1]
    return pl.pallas_call(
        _inproj_body,
        grid=(s // tm, n // tn),
        in_specs=[
            pl.BlockSpec((tm, d), lambda i, j: (i, 0)),
            pl.BlockSpec((1, d), lambda i, j: (0, 0)),
            pl.BlockSpec((d, tn), lambda i, j: (0, j)),
        ],
        out_specs=pl.BlockSpec((tm, tn), lambda i, j: (i, j)),
        out_shape=jax.ShapeDtypeStruct((s, n), F32),
        scratch_shapes=[pltpu.VMEM((tm, d), BF16)],
        compiler_params=_cparams(("parallel", "arbitrary")),
        name="inproj",
    )(h, g, w)


def _rwprep_body(f_ref, fp_ref, lo_ref, lop_ref, mua_ref, mul_ref, w0_ref, a0_ref, kk_ref, ka_ref,
                 rk_ref, wup_ref, aup_ref, gup_ref, e_ref, tri_ref,
                 at_ref, rt_ref, bt_ref, kt_ref, v_ref, pt_ref, g_ref, bon_ref, *, tm):
    first = pl.program_id(0) == 0

    def shifted(x, prev_blk):
        prev_last = jnp.where(first, 0.0, prev_blk[7:8, :])
        rolled = pltpu.roll(x, 1, 0)
        row = lax.broadcasted_iota(jnp.int32, x.shape, 0)
        return jnp.where(row == 0, prev_last, rolled)

    f = f_ref[...]
    fs = f + mua_ref[...] * (shifted(f, fp_ref[...]) - f)
    lo = lo_ref[...]
    los = lo + mul_ref[...] * (shifted(lo, lop_ref[...]) - lo)
    r = fs[:, :RW_WIDTH]
    k = fs[:, RW_WIDTH:2 * RW_WIDTH]
    v = fs[:, 2 * RW_WIDTH:]

    wx = _dot(jnp.tanh(los).astype(BF16), wup_ref[...])
    ax = _dot(los.astype(BF16), aup_ref[...])
    g_ref[...] = _dot(jax.nn.sigmoid(los).astype(BF16), gup_ref[...])

    ld = -EXP_NEG_HALF * jax.nn.sigmoid(w0_ref[...] + wx)
    a = jax.nn.sigmoid(a0_ref[...] + ax)

    e = e_ref[...]

    def headsum(x):
        parts = []
        for c in range(RW_WIDTH // 256):
            hi, lo_ = _split2(x[:, c * 256:(c + 1) * 256])
            parts.append(_dot(hi, e) + _dot(lo_, e))
        return jnp.concatenate(parts, axis=1)

    kk0 = k * kk_ref[...]
    kk = kk0 * lax.rsqrt(jnp.maximum(headsum(kk0 * kk0), 1e-24))
    kmod = k * (1.0 + (a - 1.0) * ka_ref[...])
    bon_ref[...] = headsum(r * kmod * rk_ref[...]) * v

    tri = tri_ref[...]
    cums = []
    for c in range(tm // RW_CHUNK):
        hi, mid, lo_ = _split3(ld[c * RW_CHUNK:(c + 1) * RW_CHUNK])
        cums.append(_dot(tri, hi) + _dot(tri, mid) + _dot(tri, lo_))
    cum = jnp.concatenate(cums, axis=0)
    e_in = jnp.exp(cum)
    e_out = jnp.exp(-cum)
    at_ref[...] = (-kk * jnp.exp(cum - ld)).astype(BF16)
    rt_ref[...] = r * e_in
    bt_ref[...] = (kk * a * e_out).astype(BF16)
    kt_ref[...] = (kmod * e_out).astype(BF16)
    v_ref[...] = v.astype(BF16)
    for c in range(tm // RW_CHUNK):
        pt_ref[c] = e_in[c * RW_CHUNK + RW_CHUNK - 1:(c + 1) * RW_CHUNK, :]


def _rwprep(feats, mu_a, mu_l, w0, a0, k_k, k_a, r_k, wup, aup, gup, tm=256):
    s = feats.shape[0]
    nb8 = tm // 8
    lora_blk = (2 * 3072 - RW_LORA_PAD) // RW_LORA_PAD
    e = (lax.broadcasted_iota(jnp.int32, (256, 256), 0) // RW_N
         == lax.broadcasted_iota(jnp.int32, (256, 256), 1) // RW_N).astype(BF16)
    tri = (lax.broadcasted_iota(jnp.int32, (RW_CHUNK, RW_CHUNK), 0)
           >= lax.broadcasted_iota(jnp.int32, (RW_CHUNK, RW_CHUNK), 1)).astype(BF16)
    row = lambda n: pl.BlockSpec((1, n), lambda i: (0, 0))
    full = lambda a, b: pl.BlockSpec((a, b), lambda i: (0, 0))
    dense_shape = jax.ShapeDtypeStruct((s, RW_WIDTH), F32)
    dense_b16 = jax.ShapeDtypeStruct((s, RW_WIDTH), BF16)
    dense_spec = pl.BlockSpec((tm, RW_WIDTH), lambda i: (i, 0))
    ncb = tm // RW_CHUNK
    return pl.pallas_call(
        functools.partial(_rwprep_body, tm=tm),
        grid=(s // tm,),
        in_specs=[
            pl.BlockSpec((tm, 3072), lambda i: (i, 0)),
            pl.BlockSpec((8, 3072), lambda i: (jnp.maximum(i * nb8 - 1, 0), 0)),
            pl.BlockSpec((tm, RW_LORA_PAD), lambda i: (i, lora_blk)),
            pl.BlockSpec((8, RW_LORA_PAD), lambda i: (jnp.maximum(i * nb8 - 1, 0), lora_blk)),
            row(3072), row(RW_LORA_PAD), row(RW_WIDTH), row(RW_WIDTH), row(RW_WIDTH), row(RW_WIDTH),
            row(RW_WIDTH),
            full(RW_LORA_PAD, RW_WIDTH), full(RW_LORA_PAD, RW_WIDTH), full(RW_LORA_PAD, RW_WIDTH),
            full(256, 256), full(RW_CHUNK, RW_CHUNK),
        ],
        out_specs=[dense_spec] * 5 + [
            pl.BlockSpec((ncb, 1, RW_WIDTH), lambda i: (i, 0, 0)),
            dense_spec,
            dense_spec,
        ],
        out_shape=[dense_b16, dense_shape, dense_b16, dense_b16, dense_b16] + [
            jax.ShapeDtypeStruct((s // RW_CHUNK, 1, RW_WIDTH), F32),
            dense_shape,
            dense_shape,
        ],
        compiler_params=_cparams(("parallel",)),
        name="rwprep",
    )(feats, feats, feats, feats, mu_a, mu_l, w0, a0, k_k, k_a, r_k, wup, aup, gup, e, tri)


def _dot_tn(a, b):
    return lax.dot_general(a, b, (((0,), (0,)), ((), ())), preferred_element_type=F32)


def _rwscan_pairs_body(at_ref, rt_ref, bt_ref, kt_ref, v_ref, pt_ref, y_ref, s_ref, *, nc):
    @pl.when(pl.program_id(0) == 0)
    def _():
        s_ref[...] = jnp.zeros_like(s_ref)

    t = RW_CHUNK
    w = RW_GROUP * RW_N
    row = lax.broadcasted_iota(jnp.int32, (t, w), 0)
    col = lax.broadcasted_iota(jnp.int32, (t, w), 1)
    colh = col % RW_N
    strict = row > colh
    incl = row >= colh
    eye_pair = (row == colh).astype(F32)
    head_of_lane = col // RW_N
    r2 = lax.broadcasted_iota(jnp.int32, (w, w), 0)
    c2 = lax.broadcasted_iota(jnp.int32, (w, w), 1)
    blk_f = ((r2 // RW_N) == (c2 // RW_N)).astype(F32)
    blk_b = blk_f.astype(BF16)
    eye_w = (r2 == c2).astype(F32)

    def bd(x):
        return jnp.concatenate([x] * RW_GROUP, axis=0) * blk_b

    def run_step():
        npair = RW_HEADS // RW_GROUP
        items = [(c, p) for c in range(nc) for p in range(npair)]
        pairs = range(len(items))
        sls = [(slice(c * t, (c + 1) * t), slice(p * w, (p + 1) * w)) for c, p in items]
        pts = [pt_ref[c][:, p * w:(p + 1) * w] for c, p in items]
        bf = lambda xs: [x.astype(BF16) for x in xs]
        a_b = [at_ref[sl] for sl in sls]
        r_f = [rt_ref[sl] for sl in sls]
        b_b = [bt_ref[sl] for sl in sls]
        k_b = [kt_ref[sl] for sl in sls]
        v_b = [v_ref[sl] for sl in sls]
        gg = [_dot_nt(jnp.concatenate([a_b[p], r_f[p].astype(BF16)], axis=0),
                      jnp.concatenate([bd(b_b[p]), bd(k_b[p])], axis=0)) for p in pairs]
        a_ab = [jnp.where(strict, g[:t, :w], 0.0) for g in gg]
        a_ak = bf([jnp.where(strict, g[:t, w:], 0.0) for g in gg])
        a_rb = bf([jnp.where(incl, g[t:, :w], 0.0) for g in gg])
        a_rk = bf([jnp.where(incl, g[t:, w:], 0.0) for g in gg])
        inv = [eye_pair + x for x in a_ab]
        pk = a_ab
        for _ in range(5):
            pkb = bf(pk)
            pk = [_dot(pkb[p], bd(pkb[p])) for p in pairs]
            pkb = bf(pk)
            inv = [inv[p] + _dot(inv[p].astype(BF16), bd(pkb[p])) for p in pairs]
        inv_b = bf(inv)
        w1 = bf([_dot(a_ak[p], bd(v_b[p])) for p in pairs])
        az = [_dot(inv_b[p], jnp.concatenate([bd(a_b[p]), bd(w1[p])], axis=1)) for p in pairs]
        atp = bf([x[:, :w] for x in az])
        z0 = bf([x[:, w:] for x in az])
        ry = [_dot(a_rb[p], jnp.concatenate([bd(atp[p]), bd(z0[p])], axis=1)) for p in pairs]
        rh = bf([r_f[p] + ry[p][:, :w] for p in pairs])
        y0 = [ry[p][:, w:] + _dot(a_rk[p], bd(v_b[p])) for p in pairs]
        m_bd = bf([(eye_w + _dot_tn(atp[p], b_b[p])) * blk_f * pts[p] for p in pairs])
        c_full = [(_dot_tn(z0[p], b_b[p]) + _dot_tn(v_b[p], k_b[p])) * blk_f * pts[p] for p in pairs]
        c_pair = [sum(x[hh * RW_N:(hh + 1) * RW_N] for hh in range(RW_GROUP)) for x in c_full]

        def headmean(x):
            out = jnp.zeros_like(x)
            for hh in range(RW_GROUP):
                mine = head_of_lane == hh
                out = jnp.where(mine, jnp.sum(jnp.where(mine, x, 0.0), axis=-1, keepdims=True), out)
            return out * (1.0 / RW_N)

        state = [s_ref[p] for p in range(npair)]
        for q in pairs:
            p = items[q][1]
            st_b = state[p].astype(BF16)
            y = _dot_nt(rh[q], bd(st_b)) + y0[q]
            state[p] = _dot(st_b, m_bd[q]) + c_pair[q]
            yc = y - headmean(y)
            y_ref[sls[q]] = yc * lax.rsqrt(headmean(yc * yc) + RW_GN_EPS)
        for p in range(npair):
            s_ref[p] = state[p]

    run_step()


def _rwscan(at, rt, bt, kt, v, pt, rows=512):
    s = at.shape[0]
    nc = rows // RW_CHUNK
    dense_spec = pl.BlockSpec((rows, RW_WIDTH), lambda i: (i, 0))
    return pl.pallas_call(
        functools.partial(_rwscan_pairs_body, nc=nc),
        grid=(s // rows,),
        in_specs=[dense_spec] * 5 + [pl.BlockSpec((nc, 1, RW_WIDTH), lambda i: (i, 0, 0))],
        out_specs=dense_spec,
        out_shape=jax.ShapeDtypeStruct((s, RW_WIDTH), F32),
        scratch_shapes=[pltpu.VMEM((RW_HEADS // RW_GROUP, RW_N, RW_GROUP * RW_N), F32)],
        compiler_params=_cparams(("arbitrary",)),
        name="rwscan",
    )(at, rt, bt, kt, v, pt)


def _nsaprep_body(f_ref, cos_ref, sin_ref, qn_ref, qr_ref, kc_ref, vc_ref, ks_ref, vst_ref, kw_ref,
                  vwt_ref, gt_ref, *, tq):
    cs = cos_ref[...]
    sn = sin_ref[...]
    lane = lax.broadcasted_iota(jnp.int32, (tq, 128), 1)
    first8 = (lane % NSA_DH) < (ROPE_DIM // 2)

    def rope(x):
        swapped = jnp.where(first8, pltpu.roll(x, 128 - ROPE_DIM // 2, 1), pltpu.roll(x, ROPE_DIM // 2, 1))
        return x * cs + swapped * sn

    scale = NSA_DH ** -0.5 * LOG2E
    lane_hi = lane - NSA_DH
    blk_in_chunk = (lax.broadcasted_iota(jnp.int32, (tq, 128), 0) % SEL_CHUNK) // SEL_BLOCK
    onehot = ((lane_hi >= 0) & (lane_hi % BIAS_ROWS == 0) & (lane_hi // BIAS_ROWS == blk_in_chunk)).astype(F32)
    ones_rows = jnp.ones((V_ROWS - NSA_DH, tq), BF16)
    eye = (lax.broadcasted_iota(jnp.int32, (128, 128), 0)
           == lax.broadcasted_iota(jnp.int32, (128, 128), 1)).astype(BF16)

    def transposed_bf16(x):
        return _dot_nt(eye, x.astype(BF16)).astype(BF16)

    for p in range(NSA_HEADS // 2):
        x = f_ref[:, p * 128:(p + 1) * 128]
        xn_t = transposed_bf16(x * scale)
        xr_t = transposed_bf16(rope(x) * scale)
        for e in range(2):
            hd = 2 * p + e
            g, h = hd // NSA_HPG, hd % NSA_HPG
            qn_ref[g, 0, :, h * tq:(h + 1) * tq] = xn_t[e * NSA_DH:(e + 1) * NSA_DH]
            qr_ref[g, 0, :, h * tq:(h + 1) * tq] = xr_t[e * NSA_DH:(e + 1) * NSA_DH]

    def kv_piece(idx, p):
        base = NSA_HEADS * NSA_DH + idx * NSA_KV + p * 128
        return f_ref[:, base:base + 128]

    for p in range(2):
        kc = kv_piece(0, p)
        vc = kv_piece(1, p)
        ks = rope(kv_piece(2, p))
        vs_t = transposed_bf16(kv_piece(3, p))
        kw = rope(kv_piece(4, p))
        vw_t = transposed_bf16(kv_piece(5, p))
        for e in range(2):
            g = 2 * p + e
            sl = slice(e * NSA_DH, (e + 1) * NSA_DH)
            kc_ref[g] = kc[:, sl]
            vc_ref[g] = vc[:, sl]
            ks_low = ks if e == 0 else pltpu.roll(ks, NSA_DH, 1)
            ks_ref[g] = jnp.where(lane < NSA_DH, ks_low, onehot).astype(BF16)
            kw_low = kw if e == 0 else pltpu.roll(kw, NSA_DH, 1)
            kw_ref[g] = jnp.where(lane < NSA_DH, kw_low, jnp.where(lane == NSA_DH, 1.0, 0.0)).astype(BF16)
            vst_ref[g, :NSA_DH] = vs_t[sl]
            vst_ref[g, NSA_DH:] = ones_rows
            vwt_ref[g, :NSA_DH] = vw_t[sl]
            vwt_ref[g, NSA_DH:] = ones_rows
    gl = f_ref[:, 2560:2688]
    gt_ref[...] = jax.nn.sigmoid(gl).T[:3 * NSA_HEADS]


def _nsaprep(feats, cos_t, sin_t, tq):
    s = feats.shape[0]
    ni = s // tq
    kv_f32 = jax.ShapeDtypeStruct((NSA_G, s, NSA_DH), F32)
    kv_b16 = jax.ShapeDtypeStruct((NSA_G, s, NSA_DH), BF16)
    ksa_b16 = jax.ShapeDtypeStruct((NSA_G, s, 2 * NSA_DH), BF16)
    kvt_b16 = jax.ShapeDtypeStruct((NSA_G, V_ROWS, s), BF16)
    q_shape = jax.ShapeDtypeStruct((NSA_G, ni, NSA_DH, NSA_HPG * tq), BF16)
    q_spec = pl.BlockSpec((NSA_G, 1, NSA_DH, NSA_HPG * tq), lambda i: (0, i, 0, 0))
    kv_spec = pl.BlockSpec((NSA_G, tq, NSA_DH), lambda i: (0, i, 0))
    ksa_spec = pl.BlockSpec((NSA_G, tq, 2 * NSA_DH), lambda i: (0, i, 0))
    kvt_spec = pl.BlockSpec((NSA_G, V_ROWS, tq), lambda i: (0, 0, i))
    return pl.pallas_call(
        functools.partial(_nsaprep_body, tq=tq),
        grid=(ni,),
        in_specs=[
            pl.BlockSpec((tq, 3072), lambda i: (i, 1)),
            pl.BlockSpec((tq, 128), lambda i: (i, 0)),
            pl.BlockSpec((tq, 128), lambda i: (i, 0)),
        ],
        out_specs=[q_spec, q_spec, kv_spec, kv_spec, ksa_spec, kvt_spec, ksa_spec, kvt_spec,
                   pl.BlockSpec((3 * NSA_HEADS, tq), lambda i: (0, i))],
        out_shape=[q_shape, q_shape, kv_f32, kv_f32, ksa_b16, kvt_b16, ksa_b16, kvt_b16,
                   jax.ShapeDtypeStruct((3 * NSA_HEADS, s), F32)],
        compiler_params=_cparams(("parallel",)),
        name="nsaprep",
    )(feats, cos_t, sin_t)


def _gelu_tanh(x):
    return 0.5 * x * (1.0 + jnp.tanh(0.7978845608028654 * (x + 0.044715 * x * x * x)))


def _compress_body(x_ref, pe1_ref, pe2_ref, w1a_ref, w1b_ref, w2_ref, o_ref, *, ncp, transpose_out):
    x = x_ref[0]
    a = _dot((x + pe1_ref[...]).astype(BF16), w1a_ref[...])
    b = _dot((x + pe2_ref[...]).astype(BF16), w1b_ref[...])
    hid = a + pltpu.roll(b, ncp - 1, 0)
    act = _gelu_tanh(hid).astype(BF16)
    if transpose_out:
        o_ref[0, :NSA_DH] = _dot_nt(w2_ref[...], act).astype(BF16)
        o_ref[0, NSA_DH:] = jnp.ones((V_ROWS - NSA_DH, ncp), BF16)
    else:
        o_ref[0] = _dot(act, w2_ref[...]).astype(BF16)


def _compress(x, pe, w1, w2, transpose_out):
    g, s, dh = x.shape
    ncp = s // CMP_STRIDE
    half = CMP_STRIDE * dh
    xr = x.reshape(g, ncp, half)
    pe1 = pe[:CMP_STRIDE].reshape(1, half)
    pe2 = pe[CMP_STRIDE:].reshape(1, half)
    w1a = w1[:half].astype(BF16)
    w1b = w1[half:].astype(BF16)
    hid = w1.shape[1]
    w2k = (w2.T if transpose_out else w2).astype(BF16)
    full = lambda a, b: pl.BlockSpec((a, b), lambda i: (0, 0))
    if transpose_out:
        out_shape = jax.ShapeDtypeStruct((g, V_ROWS, ncp), BF16)
        out_spec = pl.BlockSpec((1, V_ROWS, ncp), lambda i: (i, 0, 0))
    else:
        out_shape = jax.ShapeDtypeStruct((g, ncp, dh), BF16)
        out_spec = pl.BlockSpec((1, ncp, dh), lambda i: (i, 0, 0))
    return pl.pallas_call(
        functools.partial(_compress_body, ncp=ncp, transpose_out=transpose_out),
        grid=(g,),
        in_specs=[
            pl.BlockSpec((1, ncp, half), lambda i: (i, 0, 0)),
            full(1, half), full(1, half), full(half, hid), full(half, hid), full(*w2k.shape),
        ],
        out_specs=out_spec,
        out_shape=out_shape,
        compiler_params=_cparams(("parallel",)),
        name="compress",
    )(xr, pe1, pe2, w1a, w1b, w2k)


def _rows_bf16(row, n):
    r16 = jnp.broadcast_to(row, (16, row.shape[1])).astype(BF16)
    return jnp.concatenate([r16] * (n // 16), axis=0)


def _flash_biased(sb, vt_blk, m_ref, acc_ref, h):
    m_old = m_ref[h]
    m_new = jnp.maximum(m_old, jnp.max(sb, axis=0, keepdims=True).astype(F32))
    alpha = jnp.exp2(m_old - m_new)
    p = jnp.exp2(sb - _rows_bf16(m_new, sb.shape[0]))
    acc_ref[h] = alpha * acc_ref[h] + _dot(vt_blk, p)
    m_ref[h] = m_new


def _nsa_body(qn_ref, qr_ref, kc_ref, vct_ref, ov_ref, ks_ref, vst_ref, kw_ref, vwt_ref, gt_ref, o_ref,
              bias_ref, oc_ref, qa_ref, qw_ref, st0_ref, st1_ref, st2_ref, st3_ref, m_ref, acc_ref,
              *, tq, ncp, nsel, topn):
    st_refs = (st0_ref, st1_ref, st2_ref, st3_ref)
    g = pl.program_id(0)
    i = pl.program_id(1)
    q0 = i * tq
    bpc = tq // SEL_BLOCK
    tpos = q0 + lax.broadcasted_iota(jnp.int32, (1, tq), 1)
    heads = range(NSA_HPG)
    hcols = lambda h: slice(h * tq, (h + 1) * tq)

    row_io = lax.broadcasted_iota(jnp.int32, (tq, 1), 0)
    col_io = lax.broadcasted_iota(jnp.int32, (1, tq), 1)
    as_bias = lambda keep: jnp.where(keep, 0.0, NEG).astype(BF16)
    causal_bias = as_bias(row_io <= col_io)
    anti_bias = as_bias(row_io > col_io)

    cmp_last = 2 * CMP_STRIDE - 1
    has_cmp = jnp.where(tpos >= cmp_last, 1.0, 0.0)

    def compressed_and_select(ncc, rows):
        nck = ncc * tq
        cbias = [as_bias((cc * tq + row_io) * CMP_STRIDE + cmp_last <= tpos) for cc in range(ncc)]

        def cmp_scores(h):
            qn_h = qn_ref[0, 0, :, hcols(h)]
            return [_dot(kc_ref[0, cc * tq:(cc + 1) * tq, :], qn_h).astype(BF16) + cbias[cc]
                    for cc in range(ncc)]

        psum = None
        ahead = cmp_scores(0)
        for h in heads:
            sbs = ahead
            if h + 1 < NSA_HPG:
                ahead = cmp_scores(h + 1)
            mc = jnp.max(sbs[0], axis=0, keepdims=True)
            for sb in sbs[1:]:
                mc = jnp.maximum(mc, jnp.max(sb, axis=0, keepdims=True))
            mrows = _rows_bf16(mc.astype(F32), tq)
            pc = jnp.concatenate([jnp.exp2(sb - mrows) for sb in sbs], axis=0)
            rv = _dot(vct_ref[0, :, :nck], pc)
            inv = has_cmp / rv[NSA_DH:NSA_DH + 1]
            oc_ref[h] = rv[:NSA_DH] * inv
            pn = pc * _rows_bf16(inv, nck)
            psum = pn if psum is None else psum + pn
        imp = _dot(ov_ref[:rows, :nck], psum)

        jj = lax.broadcasted_iota(jnp.int32, (rows, tq), 0)
        jf = jj.astype(F32)
        cur = tpos // SEL_BLOCK
        forced = (jj == 0) | (jj == cur) | (jj == cur - 1)
        taken = -(2.0 ** 127)
        score = jnp.where(forced, taken, jnp.where(jj <= cur, imp, -1.0))
        for _ in range(topn - 3):
            mx = jnp.max(score, axis=0, keepdims=True)
            first = jnp.min(jnp.where(score == mx, jf, float(rows)), axis=0, keepdims=True)
            score = jnp.where(jf == first, taken, score)
        bias_ref[:rows] = jnp.where(score == taken, 0.0, NEG)

    ncc_all = ncp // tq
    nvar = 4 if ncc_all % 4 == 0 and nsel % 4 == 0 and nsel // 4 >= 2 * topn else 1
    live_blocks = (i + 1) * bpc
    for v in range(nvar):
        rows = nsel * (v + 1) // nvar
        lo = nsel * v // nvar

        @pl.when((live_blocks > lo) & (live_blocks <= rows))
        def _(v=v, rows=rows):
            compressed_and_select(ncc_all * (v + 1) // nvar, rows)

    def reset():
        m_ref[...] = jnp.full_like(m_ref, NEG)
        acc_ref[...] = jnp.zeros_like(acc_ref)

    def finish(h):
        acc = acc_ref[h]
        l = acc[NSA_DH:NSA_DH + 1]
        return acc[:NSA_DH] * jnp.where(l > 0.0, 1.0 / l, 0.0)

    row16 = lax.broadcasted_iota(jnp.int32, (BIAS_ROWS, NSA_HPG * tq), 0)

    qa_ref[:NSA_DH] = qr_ref[0, 0]

    def set_bias_rows(c):
        for b in range(bpc):
            brow = bias_ref[pl.ds(c * bpc + b, 1), :]
            brow4 = jnp.concatenate([brow] * NSA_HPG, axis=1)
            lo_row = NSA_DH + BIAS_ROWS * b
            qa_ref[lo_row:lo_row + BIAS_ROWS] = jnp.where(row16 == 0, brow4, 0.0).astype(BF16)

    def sel_scores(c, h):
        kblk = ks_ref[0, pl.ds(pl.multiple_of(c * tq, tq), tq), :]
        return _dot(kblk, qa_ref[:, hcols(h)]).astype(BF16)

    reset()
    set_bias_rows(0)
    for h in heads:
        st_refs[h][...] = sel_scores(0, h)

    def sel_chunk(c):
        set_bias_rows(c + 1)
        vblk = vst_ref[0, :, pl.ds(pl.multiple_of(c * tq, tq), tq)]
        nxt = sel_scores(c + 1, 0)
        for h in heads:
            cur_scores = st_refs[h][...]
            after = sel_scores(c + 1, h + 1) if h + 1 < NSA_HPG else None
            _flash_biased(cur_scores, vblk, m_ref, acc_ref, h)
            st_refs[h][...] = nxt
            nxt = after

    def sel_group(cg, carry):
        for u in range(SEL_UNROLL):
            sel_chunk(SEL_UNROLL * cg + u)
        return carry

    lax.fori_loop(0, i // SEL_UNROLL, sel_group, 0)
    done = (i // SEL_UNROLL) * SEL_UNROLL
    part = SEL_UNROLL // 2
    while part >= 1:
        @pl.when((i & part) != 0)
        def _(base=done, part=part):
            for u in range(part):
                sel_chunk(base + u)
        done = done + (i & part)
        part //= 2

    vblk = vst_ref[0, :, pl.ds(pl.multiple_of(q0, tq), tq)]
    for h in heads:
        _flash_biased(st_refs[h][...] + causal_bias, vblk, m_ref, acc_ref, h)
    o_s = [finish(h) for h in heads]

    reset()
    nback = WINDOW // tq
    win_bias = [causal_bias] + [None] * (nback - 1) + [anti_bias]
    qw_ref[:NSA_DH] = qr_ref[0, 0]
    qw_ref[NSA_DH + BIAS_ROWS:] = jnp.zeros((NSA_DH - BIAS_ROWS, NSA_HPG * tq), BF16)

    def stash_window(j, h):
        exists = jnp.where(i >= j, 0.0, NEG)
        if h == 0:
            qw_ref[NSA_DH:NSA_DH + BIAS_ROWS] = jnp.where(row16 == 0, exists, 0.0).astype(BF16)
        kblk = kw_ref[0, pl.ds(pl.multiple_of(jnp.maximum(i - j, 0) * tq, tq), tq), :]
        sb = _dot(kblk, qw_ref[:, hcols(h)]).astype(BF16)
        st_refs[h][...] = sb if win_bias[j] is None else sb + win_bias[j]

    for h in heads:
        stash_window(0, h)
    for j in range(nback + 1):
        vblk = vwt_ref[0, :, pl.ds(pl.multiple_of(jnp.maximum(i - j, 0) * tq, tq), tq)]
        for h in heads:
            _flash_biased(st_refs[h][...], vblk, m_ref, acc_ref, h)
            if j < nback:
                stash_window(j + 1, h)
    o_w = [finish(h) for h in heads]

    def gate(branch, h):
        return gt_ref[pl.ds(branch * NSA_HEADS + g * NSA_HPG + h, 1), :]

    out_t = [gate(0, h) * oc_ref[h] + gate(1, h) * o_s[h] + gate(2, h) * o_w[h] for h in heads]
    halves = [jnp.concatenate(out_t[2 * p:2 * p + 2], axis=0).T for p in range(NSA_HPG // 2)]
    o_ref[...] = jnp.concatenate(halves, axis=1)


def _nsa(qn, qr, kc, vct, ov, ks, vst, kw, vwt, gt, tq):
    g, ni, dh, w4 = qn.shape
    s = ks.shape[1]
    ncp = kc.shape[1]
    nsel = s // SEL_BLOCK
    topn = min(SEL_TOPK, nsel)
    assert tq == SEL_CHUNK and (tq // SEL_BLOCK) * BIAS_ROWS == dh and w4 == NSA_HPG * tq
    assert NSA_HPG < FORCE_BONUS
    q_spec = pl.BlockSpec((1, 1, dh, w4), lambda a, b: (a, b, 0, 0))
    return pl.pallas_call(
        functools.partial(_nsa_body, tq=tq, ncp=ncp, nsel=nsel, topn=topn),
        grid=(g, ni),
        in_specs=[
            q_spec, q_spec,
            pl.BlockSpec((1, ncp, dh), lambda a, b: (a, 0, 0)),
            pl.BlockSpec((1, V_ROWS, ncp), lambda a, b: (a, 0, 0)),
            pl.BlockSpec((nsel, ncp), lambda a, b: (0, 0)),
            pl.BlockSpec((1, s, 2 * dh), lambda a, b: (a, 0, 0)),
            pl.BlockSpec((1, V_ROWS, s), lambda a, b: (a, 0, 0)),
            pl.BlockSpec((1, s, 2 * dh), lambda a, b: (a, 0, 0)),
            pl.BlockSpec((1, V_ROWS, s), lambda a, b: (a, 0, 0)),
            pl.BlockSpec((3 * NSA_HEADS, tq), lambda a, b: (0, b)),
        ],
        out_specs=pl.BlockSpec((tq, NSA_HPG * dh), lambda a, b: (b, a)),
        out_shape=jax.ShapeDtypeStruct((s, NSA_HEADS * dh), F32),
        scratch_shapes=[
            pltpu.VMEM((nsel, tq), F32),
            pltpu.VMEM((NSA_HPG, dh, tq), F32),
            pltpu.VMEM((2 * dh, w4), BF16),
            pltpu.VMEM((2 * dh, w4), BF16),
            pltpu.VMEM((tq, tq), BF16), pltpu.VMEM((tq, tq), BF16),
            pltpu.VMEM((tq, tq), BF16), pltpu.VMEM((tq, tq), BF16),
            pltpu.VMEM((NSA_HPG, 1, tq), F32),
            pltpu.VMEM((NSA_HPG, V_ROWS, tq), F32),
        ],
        compiler_params=_cparams(("arbitrary", "arbitrary")),
        name="nsa",
    )(qn, qr, kc, vct, ov, ks, vst, kw, vwt, gt)


def _memkv_body(mem_ref, g_ref, w_ref, k_ref, v_ref):
    kv = _dot(_rms(mem_ref[...], g_ref[...]).astype(BF16), w_ref[...])
    width = MEM_HEADS * MEM_DH
    k_ref[...] = kv[:, :width].astype(BF16)
    v_ref[...] = kv[:, width:].astype(BF16)


def _memkv(mem, g, w):
    m, d = mem.shape
    width = MEM_HEADS * MEM_DH
    shp = jax.ShapeDtypeStruct((m, width), BF16)
    return pl.pallas_call(
        _memkv_body,
        out_shape=[shp, shp],
        compiler_params=pltpu.CompilerParams(vmem_limit_bytes=VMEM_LIMIT),
        name="memkv",
    )(mem, g, w)


def _mixout_body(yn_ref, bon_ref, gate_ref, ynsa_ref, h_ref, lnw_ref, lnb_ref, wo_rw_ref, wo_nsa_ref,
                 gpost_ref, mpre_ref, wq_ref, k_ref, v_ref, wom_ref, mpost_ref, o_ref):
    y_rw = ((yn_ref[...] * lnw_ref[...] + lnb_ref[...]) + bon_ref[...]) * gate_ref[...]
    y = _dot(y_rw.astype(BF16), wo_rw_ref[...]) + _dot(ynsa_ref[...].astype(BF16), wo_nsa_ref[...])
    h2 = h_ref[...] + _rms(y, gpost_ref[...])
    q = _dot(_rms(h2, mpre_ref[...]).astype(BF16), wq_ref[...])
    scale = MEM_DH ** -0.5
    outs = []
    for hh in range(MEM_HEADS):
        sl = slice(hh * MEM_DH, (hh + 1) * MEM_DH)
        s = _dot_nt(q[:, sl].astype(BF16), k_ref[:, sl]) * scale
        p = jnp.exp(s - jnp.max(s, axis=-1, keepdims=True))
        p = p * (1.0 / jnp.sum(p, axis=-1, keepdims=True))
        outs.append(_dot(p.astype(BF16), v_ref[:, sl]))
    o = jnp.concatenate(outs, axis=-1).astype(BF16)
    m = _dot(o, wom_ref[...])
    o_ref[...] = h2 + _rms(m, mpost_ref[...])


def _mixout(yn, bon, gate, ynsa, h, lnw, lnb, w_out, gpost, mpre, wq, k, v, wom, mpost, tm=256):
    s, d = h.shape
    rows = lambda n: pl.BlockSpec((tm, n), lambda i: (i, 0))
    full = lambda a: pl.BlockSpec(a.shape, lambda i: (0, 0))
    wo_rw = pl.BlockSpec((RW_WIDTH, d), lambda i: (0, 0))
    wo_nsa = pl.BlockSpec((RW_WIDTH, d), lambda i: (1, 0))
    return pl.pallas_call(
        _mixout_body,
        grid=(s // tm,),
        in_specs=[rows(RW_WIDTH), rows(RW_WIDTH), rows(RW_WIDTH), rows(RW_WIDTH), rows(d),
                  full(lnw), full(lnb), wo_rw, wo_nsa, full(gpost), full(mpre), full(wq),
                  full(k), full(v), full(wom), full(mpost)],
        out_specs=rows(d),
        out_shape=jax.ShapeDtypeStruct((s, d), F32),
        compiler_params=_cparams(("parallel",)),
        name="mixout",
    )(yn, bon, gate, ynsa, h, lnw, lnb, w_out, w_out, gpost, mpre, wq, k, v, wom, mpost)


def _rope_tables(s):
    half = ROPE_DIM // 2
    inv_freq = ROPE_THETA ** (-jnp.arange(half, dtype=F32) * 2.0 / ROPE_DIM)
    ang = jnp.arange(s, dtype=jnp.int32).astype(F32)[:, None] * inv_freq[None, :]
    cos, sin = jnp.cos(ang), jnp.sin(ang)
    ones = jnp.ones((s, NSA_DH - ROPE_DIM), F32)
    zeros = jnp.zeros((s, NSA_DH - ROPE_DIM), F32)
    cos_h = jnp.concatenate([cos, cos, ones], axis=1)
    sin_h = jnp.concatenate([-sin, sin, zeros], axis=1)
    return jnp.tile(cos_h, (1, 2)), jnp.tile(sin_h, (1, 2))


def _overlap_t(s):
    ncp = s // CMP_STRIDE
    nsel = s // SEL_BLOCK
    cmp_start = jnp.arange(ncp)[None, :] * CMP_STRIDE
    sel_start = jnp.arange(nsel)[:, None] * SEL_BLOCK
    ov = (cmp_start < sel_start + SEL_BLOCK) & (cmp_start + 2 * CMP_STRIDE - 1 >= sel_start)
    ov = ov & (jnp.arange(ncp)[None, :] < ncp - 1)
    return ov.astype(BF16)


def kernel(x, mem, ffn1_pre_g, ffn1_w_gate, ffn1_w_up, ffn1_w_down, ffn1_post_g, mix_pre_g, w_in, rw_mu, rw_w0, rw_w_up, rw_a0, rw_a_up, rw_g_up, rw_k_k, rw_k_a, rw_r_k, rw_ln_w, rw_ln_b, cmp_pe_k, cmp_w1_k, cmp_w2_k, cmp_pe_v, cmp_w1_v, cmp_w2_v, w_out, mix_post_g, mem_pre_g, mem_norm_g, mem_w_q, mem_w_kv, mem_w_o, mem_post_g, ffn2_pre_g, ffn2_w_gate, ffn2_w_up, ffn2_w_down, ffn2_post_g):
    b, s, d = x.shape
    tq = SEL_CHUNK
    assert b == 1 and d == D_MODEL and s % (CMP_STRIDE * tq) == 0
    row = lambda v: v.reshape(1, -1).astype(F32)

    def ffn_weights(wg, wu, wd):
        return wg.astype(BF16), wu.astype(BF16), wd.astype(BF16)

    h = x[0]
    h = _ffn(h, row(ffn1_pre_g), *ffn_weights(ffn1_w_gate, ffn1_w_up, ffn1_w_down), row(ffn1_post_g))

    feats = _inproj(h, row(mix_pre_g), _regroup_w_in(w_in))

    mu_a = row(rw_mu[:3 * RW_WIDTH])
    mu_l = row(jnp.pad(rw_mu[3 * RW_WIDTH:], (0, RW_LORA_PAD - RW_LORA)))
    lora_w = lambda w, off: jnp.pad(w, ((off, RW_LORA_PAD - off - w.shape[0]), (0, 0))).astype(BF16)
    at, rt, bt, kt, v, pt, gate, bonus = _rwprep(
        feats, mu_a, mu_l, row(rw_w0), row(rw_a0), row(rw_k_k), row(rw_k_a), row(rw_r_k),
        lora_w(rw_w_up, 0), lora_w(rw_a_up, 64), lora_w(rw_g_up, 128))
    yn = _rwscan(at, rt, bt, kt, v, pt)

    cos_t, sin_t = _rope_tables(s)
    qn, qr, kc, vc, ks, vst, kw, vwt, gt = _nsaprep(feats, cos_t, sin_t, tq)
    k_cmp = _compress(kc, cmp_pe_k, cmp_w1_k, cmp_w2_k, transpose_out=False)
    v_cmp_t = _compress(vc, cmp_pe_v, cmp_w1_v, cmp_w2_v, transpose_out=True)
    y_nsa = _nsa(qn, qr, k_cmp, v_cmp_t, _overlap_t(s), ks, vst, kw, vwt, gt, tq)

    mem_k, mem_v = _memkv(mem[0], row(mem_norm_g), mem_w_kv.astype(BF16))
    h = _mixout(yn, bonus, gate, y_nsa, h, row(rw_ln_w), row(rw_ln_b),
                w_out.astype(BF16), row(mix_post_g),
                row(mem_pre_g), mem_w_q.astype(BF16), mem_k, mem_v, mem_w_o.astype(BF16), row(mem_post_g))

    h = _ffn(h, row(ffn2_pre_g), *ffn_weights(ffn2_w_gate, ffn2_w_up, ffn2_w_down), row(ffn2_post_g))
    return h[None]
```
